```python
import math
import jax, jax.numpy as jnp
from jax import lax
import numpy as np


D_MODEL = 2048
BATCH = 4
SEQ = 2048
DEPTH = 1
DEC_BATCH = 16
DEC_SEQ = 64
PAST_LEN = 2048

CHUNK = 64
N_META = 16
MIX_WIDTH = D_MODEL
POOL_WIDTH = MIX_WIDTH // 2
POOL_WINDOWS = (2, 4, 8, 16)
POOL_GROUPS = len(POOL_WINDOWS)
POOL_GROUP_DIM = POOL_WIDTH // POOL_GROUPS
POOL_STATE = max(POOL_WINDOWS) - 1
ATTN_WIDTH = MIX_WIDTH - POOL_WIDTH
DIFF_HEADS = 8
V_DIM = ATTN_WIDTH // DIFF_HEADS
HALF_DIM = V_DIM // 2
QK_DIM = 2 * HALF_DIM
QK_WIDTH = DIFF_HEADS * QK_DIM
IN_WIDTH = POOL_WIDTH + 2 * QK_WIDTH + ATTN_WIDTH
ATT_SCALE = HALF_DIM ** -0.5
D_FF = -(-8 * D_MODEL // (3 * 256)) * 256
NUM_BUCKETS = 32
MAX_DISTANCE = 128
QB = 128
EPS = 1e-6
SUBLN_EPS = 1e-5
NEG_INF = -1e30

kernel_name = "hymba_pool_diffattn_stream_step"


def _rmsnorm(x, g, eps=EPS):
    xf = x.astype(jnp.float32)
    y = xf * lax.rsqrt(jnp.mean(xf * xf, axis=-1, keepdims=True) + eps)
    return (y * g.astype(jnp.float32)).astype(x.dtype)


def _chunk_id(idx):
    return jnp.where(idx < N_META, -1, (idx - N_META) // CHUNK)


def _t5_bucket(rel):
    half = NUM_BUCKETS // 2
    max_exact = half // 2
    ret = jnp.where(rel > 0, half, 0)
    n = jnp.abs(rel)
    nf = jnp.maximum(n, 1).astype(jnp.float32)
    large = max_exact + (jnp.log(nf / max_exact) / math.log(MAX_DISTANCE / max_exact)
                         * (half - max_exact)).astype(jnp.int32)
    large = jnp.minimum(large, half - 1)
    return ret + jnp.where(n < max_exact, n, large)


def _rel_bias(rel, table):
    return jnp.transpose(table[_t5_bucket(rel)], (2, 0, 1))


def _project(h, g_mix, w_in):
    b, t, _ = h.shape
    proj = _rmsnorm(h, g_mix) @ w_in
    u = proj[..., :POOL_WIDTH]
    q = proj[..., POOL_WIDTH:POOL_WIDTH + QK_WIDTH].reshape(b, t, DIFF_HEADS, 2, HALF_DIM)
    k = proj[..., POOL_WIDTH + QK_WIDTH:POOL_WIDTH + 2 * QK_WIDTH].reshape(b, t, DIFF_HEADS, QK_DIM)
    v = proj[..., POOL_WIDTH + 2 * QK_WIDTH:].reshape(b, t, DIFF_HEADS, V_DIM)
    return u, q, k, v


def _pool_mixer(u, w_pool, pool_scale):
    b, L, _ = u.shape
    uf = u.astype(jnp.float32)
    cs = jnp.concatenate([jnp.zeros((b, 1, POOL_WIDTH), jnp.float32), jnp.cumsum(uf, axis=1)], axis=1)
    idx = jnp.arange(L)
    means = []
    for g, w in enumerate(POOL_WINDOWS):
        csg = cs[..., g * POOL_GROUP_DIM:(g + 1) * POOL_GROUP_DIM]
        lo = jnp.maximum(idx + 1 - w, 0)
        cnt = (idx + 1 - lo).astype(jnp.float32)
        means.append((csg[:, 1:] - csg[:, lo]) / cnt[None, :, None])
    p = (jnp.concatenate(means, axis=-1) - uf).astype(u.dtype)
    p = p.reshape(b, L, POOL_GROUPS, POOL_GROUP_DIM)
    y = jnp.einsum('blgc,gcd->blgd', p, w_pool).reshape(b, L, POOL_WIDTH)
    return y * pool_scale


def _diff_attention(q, k, v, bias, mask, lam, g_subln, lambda_init):
    b, tq = q.shape[:2]
    lk = k.shape[1]
    k2 = k.reshape(b, lk, DIFF_HEADS, 2, HALF_DIM)
    logits = (jnp.einsum('bqhmd,bkhmd->bhmqk', q, k2).astype(jnp.float32) * ATT_SCALE
              + bias[None, :, None].astype(jnp.float32))
    logits = jnp.where(mask, logits, NEG_INF)
    a = jax.nn.softmax(logits, axis=-1)
    wts = a[:, :, 0] - lam * a[:, :, 1]
    o = jnp.einsum('bhqk,bkhe->bqhe', wts.astype(v.dtype), v)
    o = _rmsnorm(o, g_subln, SUBLN_EPS) * (1.0 - lambda_init)
    return o.reshape(b, tq, ATTN_WIDTH)


def _prompt_attention(q, k, v, rel_table, lam, g_subln, lambda_init):
    b, L = q.shape[:2]
    nb = -(-L // QB)
    Lp = nb * QB
    q_pad = jnp.pad(q, ((0, 0), (0, Lp - L), (0, 0), (0, 0), (0, 0)))
    kidx = jnp.arange(L)
    kchunk = _chunk_id(kidx)

    def block(bi):
        qb = lax.dynamic_slice_in_dim(q_pad, bi * QB, QB, axis=1)
        qidx = bi * QB + jnp.arange(QB)
        mask = kchunk[None, :] <= _chunk_id(qidx)[:, None]
        bias = _rel_bias(kidx[None, :] - qidx[:, None], rel_table)
        return _diff_attention(qb, k, v, bias, mask, lam, g_subln, lambda_init)

    out = lax.map(block, jnp.arange(nb))
    return jnp.transpose(out, (1, 0, 2, 3)).reshape(b, Lp, ATTN_WIDTH)[:, :L]


def _swiglu(h, g_ffn, w_gate, w_up, w_down):
    n = _rmsnorm(h, g_ffn)
    return h + (jax.nn.silu(n @ w_gate) * (n @ w_up)) @ w_down


def setup_inputs(seed: int = 0) -> dict:
    key = jax.random.key(seed)
    ks = jax.random.split(key, 24)
    f32 = jnp.float32
    nrm = lambda k, s, sc: jax.random.normal(k, s, f32) * sc
    gain = lambda k, s: 1.0 + 0.1 * jax.random.normal(k, s, f32)
    return {
        "x_prompt": nrm(ks[0], (BATCH, SEQ, D_MODEL), 1.0),
        "x_sample": nrm(ks[1], (DEC_BATCH, DEC_SEQ, D_MODEL), 1.0),
        "cache_k": nrm(ks[2], (DEPTH, DEC_BATCH, N_META + PAST_LEN, DIFF_HEADS, QK_DIM), 1.0),
        "cache_v": nrm(ks[3], (DEPTH, DEC_BATCH, N_META + PAST_LEN, DIFF_HEADS, V_DIM), 1.0),
        "state_pool": nrm(ks[4], (DEPTH, DEC_BATCH, POOL_STATE, POOL_WIDTH), 1.0),
        "meta": nrm(ks[5], (N_META, D_MODEL), 1.0),
        "g_mix": gain(ks[6], (DEPTH, D_MODEL)),
        "w_in": nrm(ks[7], (DEPTH, D_MODEL, IN_WIDTH), D_MODEL ** -0.5),
        "w_pool": nrm(ks[8], (DEPTH, POOL_GROUPS, POOL_GROUP_DIM, POOL_GROUP_DIM), POOL_GROUP_DIM ** -0.5),
        "pool_scale": gain(ks[9], (DEPTH, POOL_WIDTH)),
        "lambda_q1": nrm(ks[10], (DEPTH, HALF_DIM), 0.1),
        "lambda_k1": nrm(ks[11], (DEPTH, HALF_DIM), 0.1),
        "lambda_q2": nrm(ks[12], (DEPTH, HALF_DIM), 0.1),
        "lambda_k2": nrm(ks[13], (DEPTH, HALF_DIM), 0.1),
        "g_subln": gain(ks[14], (DEPTH, V_DIM)),
        "w_out": nrm(ks[15], (DEPTH, MIX_WIDTH, D_MODEL), MIX_WIDTH ** -0.5),
        "g_ffn": gain(ks[16], (DEPTH, D_MODEL)),
        "w_gate": nrm(ks[17], (DEPTH, D_MODEL, D_FF), D_MODEL ** -0.5),
        "w_up": nrm(ks[18], (DEPTH, D_MODEL, D_FF), D_MODEL ** -0.5),
        "w_down": nrm(ks[19], (DEPTH, D_FF, D_MODEL), D_FF ** -0.5),
        "rel_bias": nrm(ks[20], (NUM_BUCKETS, DIFF_HEADS), 0.5),
        "g_final": gain(ks[21], (D_MODEL,)),
    }


def reference(x_prompt, x_sample, cache_k, cache_v, state_pool, meta, g_mix, w_in, w_pool,
              pool_scale, lambda_q1, lambda_k1, lambda_q2, lambda_k2, g_subln, w_out, g_ffn,
              w_gate, w_up, w_down, rel_bias, g_final):
    f32 = jnp.float32
    b = x_prompt.shape[0]
    h_p = jnp.concatenate(
        [jnp.broadcast_to(meta[None].astype(x_prompt.dtype), (b, N_META, D_MODEL)), x_prompt], axis=1)
    h_s = x_sample
    kp_l, vp_l, pp_l, ks_l, vs_l, ps_l = [], [], [], [], [], []
    for l in range(DEPTH):
        lambda_init = 0.8 - 0.6 * math.exp(-0.3 * l)
        lam = (jnp.exp(jnp.sum(lambda_q1[l].astype(f32) * lambda_k1[l].astype(f32)))
               - jnp.exp(jnp.sum(lambda_q2[l].astype(f32) * lambda_k2[l].astype(f32)))
               + lambda_init)

        u_p, q_p, k_p, v_p = _project(h_p, g_mix[l], w_in[l])
        a_p = _pool_mixer(u_p, w_pool[l], pool_scale[l])
        b_p = _prompt_attention(q_p, k_p, v_p, rel_bias, lam, g_subln[l], lambda_init)
        h_p = h_p + jnp.concatenate([a_p, b_p], axis=-1) @ w_out[l]
        h_p = _swiglu(h_p, g_ffn[l], w_gate[l], w_up[l], w_down[l])

        u_s, q_s, k_s, v_s = _project(h_s, g_mix[l], w_in[l])
        t = h_s.shape[1]
        ctx = jnp.concatenate([state_pool[l].astype(u_s.dtype), u_s], axis=1)
        a_s = _pool_mixer(ctx, w_pool[l], pool_scale[l])[:, POOL_STATE:]
        k_all = jnp.concatenate([cache_k[l].astype(k_s.dtype), k_s], axis=1)
        v_all = jnp.concatenate([cache_v[l].astype(v_s.dtype), v_s], axis=1)
        lk = k_all.shape[1]
        kidx = jnp.arange(lk)
        qidx = lk - t + jnp.arange(t)
        mask = _chunk_id(kidx)[None, :] <= _chunk_id(qidx)[:, None]
        bias = _rel_bias(kidx[None, :] - qidx[:, None], rel_bias)
        b_s = _diff_attention(q_s, k_all, v_all, bias, mask, lam, g_subln[l], lambda_init)
        h_s = h_s + jnp.concatenate([a_s, b_s], axis=-1) @ w_out[l]
        h_s = _swiglu(h_s, g_ffn[l], w_gate[l], w_up[l], w_down[l])

        kp_l.append(k_p)
        vp_l.append(v_p)
        pp_l.append(u_p[:, -POOL_STATE:])
        ks_l.append(k_s)
        vs_l.append(v_s)
        ps_l.append(ctx[:, -POOL_STATE:])

    y_prompt = _rmsnorm(h_p[:, N_META:], g_final)
    y_sample = _rmsnorm(h_s, g_final)
    k_prompt = jnp.stack(kp_l, axis=0)
    v_prompt = jnp.stack(vp_l, axis=0)
    pool_prompt = jnp.stack(pp_l, axis=0)
    k_sample = jnp.stack(ks_l, axis=0)
    v_sample = jnp.stack(vs_l, axis=0)
    pool_sample = jnp.stack(ps_l, axis=0)
    return (y_prompt, y_sample, k_prompt, v_prompt, pool_prompt, k_sample, v_sample, pool_sample)
```

```python
import functools
import math

import jax
import jax.numpy as jnp
from jax import lax
from jax.experimental import pallas as pl
from jax.experimental.pallas import tpu as pltpu

F32 = jnp.float32
BF16 = jnp.bfloat16

D_MODEL = 2048
CHUNK = 64
N_META = 16
POOL_WINDOWS = (2, 4, 8, 16)
POOL_GROUP_DIM = 256
POOL_WIDTH = 1024
POOL_STATE = 15
HEADS = 8
HEAD_DIM = 128
HALF_DIM = 64
ATTN_WIDTH = 1024
ATT_SCALE = HALF_DIM ** -0.5
D_FF = 5632
EPS = 1e-6
SUBLN_EPS = 1e-5
NEG_INF = -1e30
LAMBDA_INIT = 0.8 - 0.6 * math.exp(-0.3 * 0)

_BUCKET_THRESHOLDS = (1, 2, 3, 4, 5, 6, 7, 8, 12, 16, 23, 32, 46, 64, 91)

Q_TILE = 256
PROJ_ROWS = 512
FFN_ROWS = 512
FFN_COLS = 512
SAMPLE_TAIL = 272
VMEM_LIMIT = 56 * 1024 * 1024


def _rms(x, g, eps):
    ms = jnp.mean(x * x, axis=-1, keepdims=True)
    return x * lax.rsqrt(ms + eps) * g


def _dot(a, b):
    return jnp.dot(a, b, preferred_element_type=F32)


def _dot_nt(a, b):
    return lax.dot_general(a, b, (((1,), (1,)), ((), ())), preferred_element_type=F32)


def _resident(shape):
    nd = len(shape)
    return pl.BlockSpec(shape, lambda *_: (0,) * nd, pipeline_mode=pl.Buffered(1))


def _smem():
    return pl.BlockSpec(memory_space=pltpu.SMEM)


def _bias_of_rel(rel, table_ref, h):
    n = jnp.abs(rel)
    neg = jnp.full(rel.shape, table_ref[0, h], F32)
    pos = jnp.full(rel.shape, table_ref[16, h], F32)
    for b, thr in enumerate(_BUCKET_THRESHOLDS, start=1):
        ge = n >= thr
        neg = jnp.where(ge, table_ref[b, h], neg)
        pos = jnp.where(ge, table_ref[16 + b, h], pos)
    return jnp.where(rel > 0, pos, neg)


def _bias_kernel(table_ref, lq1_ref, lk1_ref, lq2_ref, lk2_ref,
                 diag_ref, meta_ref, tail_ref, lam_ref):
    h = pl.program_id(0)
    t = Q_TILE
    qi = lax.broadcasted_iota(jnp.int32, (t, t), 0)
    kj = lax.broadcasted_iota(jnp.int32, (t, t), 1)
    visible = (kj // CHUNK) <= (qi // CHUNK)
    diag_ref[0, 0] = jnp.where(visible, _bias_of_rel(kj - qi, table_ref, h), NEG_INF)
    diag_ref[0, 1] = _bias_of_rel(kj - t - qi, table_ref, h)
    qm = lax.broadcasted_iota(jnp.int32, (t, N_META), 0)
    km = lax.broadcasted_iota(jnp.int32, (t, N_META), 1)
    meta_ref[0] = _bias_of_rel(km - N_META - qm, table_ref, h)
    qs = lax.broadcasted_iota(jnp.int32, tail_ref.shape[1:], 0)
    ks = lax.broadcasted_iota(jnp.int32, tail_ref.shape[1:], 1)
    tail_ref[0] = _bias_of_rel(ks - SAMPLE_TAIL - qs, table_ref, h)
    s1 = jnp.sum(lq1_ref[...] * lk1_ref[...], axis=-1, keepdims=True)
    s2 = jnp.sum(lq2_ref[...] * lk2_ref[...], axis=-1, keepdims=True)
    lam_ref[...] = jnp.exp(s1) - jnp.exp(s2) + LAMBDA_INIT


def _bias_tables(rel_bias, lq1, lk1, lq2, lk2, n_new):
    tail_w = SAMPLE_TAIL + n_new
    vec = pl.BlockSpec((1, HALF_DIM), lambda h: (0, 0))
    return pl.pallas_call(
        _bias_kernel,
        grid=(HEADS,),
        in_specs=[_smem(), vec, vec, vec, vec],
        out_specs=[
            pl.BlockSpec((1, 2, Q_TILE, Q_TILE), lambda h: (h, 0, 0, 0)),
            pl.BlockSpec((1, Q_TILE, N_META), lambda h: (h, 0, 0)),
            pl.BlockSpec((1, n_new, tail_w), lambda h: (h, 0, 0)),
            pl.BlockSpec((1, 1), lambda h: (0, 0)),
        ],
        out_shape=[
            jax.ShapeDtypeStruct((HEADS, 2, Q_TILE, Q_TILE), F32),
            jax.ShapeDtypeStruct((HEADS, Q_TILE, N_META), F32),
            jax.ShapeDtypeStruct((HEADS, n_new, tail_w), F32),
            jax.ShapeDtypeStruct((1, 1), F32),
        ],
        name="bias_tables",
    )(rel_bias, lq1, lk1, lq2, lk2)


def _proj_kernel(x_ref, g_ref, w_ref, u_ref, q_ref, k_ref, v_ref):
    xn = _rms(x_ref[...], g_ref[...], EPS).astype(BF16)
    w = POOL_WIDTH
    u_ref[...] = _dot(xn, w_ref[:, 0 * w:1 * w])
    q_ref[...] = (_dot(xn, w_ref[:, 1 * w:2 * w]) * ATT_SCALE).astype(BF16)
    k_ref[...] = _dot(xn, w_ref[:, 2 * w:3 * w])
    v_ref[...] = _dot(xn, w_ref[:, 3 * w:4 * w])


def _project(x, g, w_bf16, rows):
    n = x.shape[0]
    w = POOL_WIDTH
    row_in = pl.BlockSpec((rows, D_MODEL), lambda i: (i, 0))
    row_out = pl.BlockSpec((rows, w), lambda i: (i, 0))
    return pl.pallas_call(
        _proj_kernel,
        grid=(n // rows,),
        in_specs=[row_in, _resident((1, D_MODEL)), _resident(w_bf16.shape)],
        out_specs=[row_out, row_out, row_out, row_out],
        out_shape=[
            jax.ShapeDtypeStruct((n, w), F32),
            jax.ShapeDtypeStruct((n, w), BF16),
            jax.ShapeDtypeStruct((n, w), F32),
            jax.ShapeDtypeStruct((n, w), F32),
        ],
        compiler_params=pltpu.CompilerParams(
            dimension_semantics=("parallel",), vmem_limit_bytes=VMEM_LIMIT),
        name="in_proj",
    )(x, g, w_bf16)


def _stack_maps(q):
    lane = lax.broadcasted_iota(jnp.int32, q.shape, 1)
    zero = jnp.zeros_like(q)
    return jnp.concatenate(
        [jnp.where(lane < HALF_DIM, q, zero), jnp.where(lane >= HALF_DIM, q, zero)], axis=0)


def _finish_head(o, gs):
    return (_rms(o, gs, SUBLN_EPS) * (1.0 - LAMBDA_INIT)).astype(BF16)


def _attn_prompt_kernel(cfar_ref, lam_ref, q_ref, k_ref, v_ref, diag_ref, meta_ref, gs_ref,
                        o_ref, m_ref, l_ref, acc_ref):
    h = pl.program_id(1)
    t = pl.program_id(2)
    tq = Q_TILE
    q2 = _stack_maps(q_ref[...])
    cfar = cfar_ref[h]

    def both(b):
        return jnp.concatenate([b, b], axis=0)

    def keys(start, size):
        return (k_ref[0, pl.ds(start, size), :].astype(BF16),
                v_ref[0, pl.ds(start, size), :].astype(BF16))

    def update(s, vt):
        m_old = m_ref[...]
        m_new = jnp.maximum(m_old, jnp.max(s, axis=-1, keepdims=True))
        alpha = jnp.exp(m_old - m_new)
        p = jnp.exp(s - m_new)
        l_ref[...] = alpha * l_ref[...] + jnp.sum(p, axis=-1, keepdims=True)
        acc_ref[...] = alpha * acc_ref[...] + _dot(p.astype(BF16), vt)
        m_ref[...] = m_new

    km, vm = keys(0, N_META)
    s = _dot_nt(q2, km) + both(jnp.where(t == 0, meta_ref[0], cfar))
    m0 = jnp.max(s, axis=-1, keepdims=True)
    p = jnp.exp(s - m0)
    m_ref[...] = m0
    l_ref[...] = jnp.sum(p, axis=-1, keepdims=True)
    acc_ref[...] = _dot(p.astype(BF16), vm)

    def far_tile(j, carry):
        kt, vt = keys(pl.multiple_of(N_META + j * tq, 8), tq)
        update(_dot_nt(q2, kt) + cfar, vt)
        return carry

    lax.fori_loop(0, t - 1, far_tile, 0)

    @pl.when(t >= 1)
    def _():
        kt, vt = keys(pl.multiple_of(N_META + (t - 1) * tq, 8), tq)
        update(_dot_nt(q2, kt) + both(diag_ref[0, 1]), vt)

    kt, vt = keys(pl.multiple_of(N_META + t * tq, 8), tq)
    update(_dot_nt(q2, kt) + both(diag_ref[0, 0]), vt)

    o = acc_ref[...] * (1.0 / l_ref[...])
    o = o[:tq] - lam_ref[0, 0] * o[tq:]
    o_ref[...] = _finish_head(o, gs_ref[...])


def _attn_prompt(q, k_all, v_all, diag, meta_b, cfar, lam, gs, batch, seq):
    tq = Q_TILE
    nt = seq // tq
    kv = pl.BlockSpec((1, N_META + seq, HEAD_DIM), lambda b, h, t: (b, 0, h))
    qo = pl.BlockSpec((tq, HEAD_DIM), lambda b, h, t: (b * nt + t, h))
    return pl.pallas_call(
        _attn_prompt_kernel,
        grid=(batch, HEADS, nt),
        in_specs=[
            _smem(), _smem(), qo, kv, kv,
            pl.BlockSpec((1, 2, tq, tq), lambda b, h, t: (h, 0, 0, 0)),
            pl.BlockSpec((1, tq, N_META), lambda b, h, t: (h, 0, 0)),
            pl.BlockSpec((1, HEAD_DIM), lambda b, h, t: (0, 0)),
        ],
        out_specs=qo,
        out_shape=jax.ShapeDtypeStruct((batch * seq, ATTN_WIDTH), BF16),
        scratch_shapes=[
            pltpu.VMEM((2 * tq, 1), F32),
            pltpu.VMEM((2 * tq, 1), F32),
            pltpu.VMEM((2 * tq, HEAD_DIM), F32),
        ],
        compiler_params=pltpu.CompilerParams(
            dimension_semantics=("parallel", "parallel", "arbitrary"),
            vmem_limit_bytes=VMEM_LIMIT),
        name="attn_prompt",
    )(cfar, lam, q, k_all, v_all, diag, meta_b, gs)


def _attn_sample_kernel(cfar_ref, lam_ref, q_ref, kn_ref, vn_ref, ck_ref, cv_ref, tail_ref, gs_ref,
                        o_ref):
    t = q_ref.shape[0]
    n_cache = ck_ref.shape[1]
    split = n_cache - SAMPLE_TAIL
    lam = lam_ref[0, 0]
    for h in range(HEADS):
        cs = slice(h * HEAD_DIM, (h + 1) * HEAD_DIM)
        q2 = _stack_maps(q_ref[:, cs])
        ka = ck_ref[0, 0:split, cs].astype(BF16)
        va = cv_ref[0, 0:split, cs].astype(BF16)
        kt = jnp.concatenate([ck_ref[0, split:n_cache, cs].astype(BF16), kn_ref[:, cs].astype(BF16)], axis=0)
        vt = jnp.concatenate([cv_ref[0, split:n_cache, cs].astype(BF16), vn_ref[:, cs].astype(BF16)], axis=0)
        bt = tail_ref[h]
        sa = _dot_nt(q2, ka) + cfar_ref[h]
        st = _dot_nt(q2, kt) + jnp.concatenate([bt, bt], axis=0)
        m = jnp.maximum(jnp.max(sa, axis=-1, keepdims=True), jnp.max(st, axis=-1, keepdims=True))
        pa = jnp.exp(sa - m)
        pt = jnp.exp(st - m)
        r = 1.0 / (jnp.sum(pa, axis=-1, keepdims=True) + jnp.sum(pt, axis=-1, keepdims=True))
        r1 = r[:t]
        r2 = lam * r[t:]
        wa = (pa[:t] * r1 - pa[t:] * r2).astype(BF16)
        wt = (pt[:t] * r1 - pt[t:] * r2).astype(BF16)
        o = _dot(wa, va) + _dot(wt, vt)
        o_ref[:, cs] = _finish_head(o, gs_ref[...])


def _attn_sample(q, k_new, v_new, cache_k, cache_v, tail_b, cfar, lam, gs, batch, t):
    n_cache = cache_k.shape[1]
    row = pl.BlockSpec((t, ATTN_WIDTH), lambda b: (b, 0))
    cache = pl.BlockSpec((1, n_cache, ATTN_WIDTH), lambda b: (b, 0, 0))
    return pl.pallas_call(
        _attn_sample_kernel,
        grid=(batch,),
        in_specs=[_smem(), _smem(), row, row, row, cache, cache,
                  _resident(tail_b.shape), _resident((1, HEAD_DIM))],
        out_specs=row,
        out_shape=jax.ShapeDtypeStruct((batch * t, ATTN_WIDTH), BF16),
        compiler_params=pltpu.CompilerParams(
            dimension_semantics=("parallel",), vmem_limit_bytes=VMEM_LIMIT),
        name="attn_sample",
    )(cfar, lam, q, k_new, v_new, cache_k, cache_v, tail_b, gs)


def _pool_mix(ext, wp_ref, scale):
    outs = []
    for g, w in enumerate(POOL_WINDOWS):
        xg = ext[:, g * POOL_GROUP_DIM:(g + 1) * POOL_GROUP_DIM]
        s = xg
        span = 1
        while span < w:
            s = s + pltpu.roll(s, span, axis=0)
            span *= 2
        p = (s[N_META:] * (1.0 / w) - xg[N_META:]).astype(BF16)
        outs.append(_dot(p, wp_ref[g]))
    return jnp.concatenate(outs, axis=-1) * scale


def _mix_out(a, b_ref, x_ref, wo_ref, o_ref):
    mix = jnp.concatenate([a.astype(BF16), b_ref[...]], axis=-1)
    o_ref[...] = x_ref[...] + _dot(mix, wo_ref[...])


def _out_prompt_kernel(u_ref, halo_ref, umeta_ref, b_ref, x_ref, wp_ref, ps_ref, wo_ref, o_ref):
    first = pl.program_id(1) == 0
    halo = jnp.where(first, umeta_ref[...], halo_ref[...])
    ext = jnp.concatenate([halo, u_ref[...]], axis=0)
    _mix_out(_pool_mix(ext, wp_ref, ps_ref[...]), b_ref, x_ref, wo_ref, o_ref)


def _out_sample_kernel(u_ref, state_ref, b_ref, x_ref, wp_ref, ps_ref, wo_ref, o_ref):
    n_seq, t = state_ref.shape[0], u_ref.shape[0] // state_ref.shape[0]
    parts = []
    for s in range(n_seq):
        ext = jnp.concatenate([state_ref[s], u_ref[s * t:(s + 1) * t, :]], axis=0)
        parts.append(_pool_mix(ext, wp_ref, ps_ref[...]))
    _mix_out(jnp.concatenate(parts, axis=0), b_ref, x_ref, wo_ref, o_ref)


def _out_common_specs(rows, idx):
    return (pl.BlockSpec((rows, ATTN_WIDTH), idx), pl.BlockSpec((rows, D_MODEL), idx))


def _out_prompt(u, u_meta, b, x, wp, ps, wo, batch, seq):
    rows = PROJ_ROWS
    nt = seq // rows
    idx = lambda bi, i: (bi * nt + i, 0)
    halo_blocks = rows // N_META
    halo_idx = lambda bi, i: (jnp.maximum((bi * nt + i) * halo_blocks - 1, 0), 0)
    b_spec, x_spec = _out_common_specs(rows, idx)
    return pl.pallas_call(
        _out_prompt_kernel,
        grid=(batch, nt),
        in_specs=[
            pl.BlockSpec((rows, POOL_WIDTH), idx),
            pl.BlockSpec((N_META, POOL_WIDTH), halo_idx),
            _resident(u_meta.shape), b_spec, x_spec,
            _resident(wp.shape), _resident(ps.shape), _resident(wo.shape),
        ],
        out_specs=x_spec,
        out_shape=jax.ShapeDtypeStruct(x.shape, F32),
        compiler_params=pltpu.CompilerParams(
            dimension_semantics=("parallel", "parallel"), vmem_limit_bytes=VMEM_LIMIT),
        name="out_proj_prompt",
    )(u, u, u_meta, b, x, wp, ps, wo)


def _out_sample(u, state16, b, x, wp, ps, wo, t):
    rows = PROJ_ROWS
    n_seq = rows // t
    idx = lambda i: (i, 0)
    b_spec, x_spec = _out_common_specs(rows, idx)
    return pl.pallas_call(
        _out_sample_kernel,
        grid=(x.shape[0] // rows,),
        in_specs=[
            pl.BlockSpec((rows, POOL_WIDTH), idx),
            pl.BlockSpec((n_seq, N_META, POOL_WIDTH), lambda i: (i, 0, 0)),
            b_spec, x_spec,
            _resident(wp.shape), _resident(ps.shape), _resident(wo.shape),
        ],
        out_specs=x_spec,
        out_shape=jax.ShapeDtypeStruct(x.shape, F32),
        compiler_params=pltpu.CompilerParams(
            dimension_semantics=("parallel",), vmem_limit_bytes=VMEM_LIMIT),
        name="out_proj_sample",
    )(u, state16, b, x, wp, ps, wo)


def _ffn_kernel(h_ref, gf_ref, wg_ref, wu_ref, wd_ref, gfin_ref, y_ref, n_ref):
    f = pl.program_id(1)

    @pl.when(f == 0)
    def _():
        h = h_ref[...]
        n_ref[...] = _rms(h, gf_ref[...], EPS).astype(BF16)
        y_ref[...] = h

    n = n_ref[...]
    g = _dot(n, wg_ref[...])
    u = _dot(n, wu_ref[...])
    act = (g * (1.0 / (1.0 + jnp.exp(-g))) * u).astype(BF16)
    y_ref[...] += _dot(act, wd_ref[...])

    @pl.when(f == pl.num_programs(1) - 1)
    def _():
        y_ref[...] = _rms(y_ref[...], gfin_ref[...], EPS)


def _ffn(h, gf, wg, wu, wd, gfin):
    rows, cols = FFN_ROWS, FFN_COLS
    row = pl.BlockSpec((rows, D_MODEL), lambda i, f: (i, 0))
    vec = pl.BlockSpec((1, D_MODEL), lambda i, f: (0, 0))
    return pl.pallas_call(
        _ffn_kernel,
        grid=(h.shape[0] // rows, D_FF // cols),
        in_specs=[
            row, vec,
            pl.BlockSpec((D_MODEL, cols), lambda i, f: (0, f)),
            pl.BlockSpec((D_MODEL, cols), lambda i, f: (0, f)),
            pl.BlockSpec((cols, D_MODEL), lambda i, f: (f, 0)),
            vec,
        ],
        out_specs=row,
        out_shape=jax.ShapeDtypeStruct(h.shape, F32),
        scratch_shapes=[pltpu.VMEM((rows, D_MODEL), BF16)],
        compiler_params=pltpu.CompilerParams(
            dimension_semantics=("parallel", "arbitrary"), vmem_limit_bytes=VMEM_LIMIT),
        name="swiglu",
    )(h, gf, wg, wu, wd, gfin)


def kernel(x_prompt, x_sample, cache_k, cache_v, state_pool, meta, g_mix, w_in, w_pool,
           pool_scale, lambda_q1, lambda_k1, lambda_q2, lambda_k2, g_subln, w_out, g_ffn,
           w_gate, w_up, w_down, rel_bias, g_final):
    batch, seq, d = x_prompt.shape
    dec_batch, t, _ = x_sample.shape
    depth = w_in.shape[0]
    assert depth == 1 and d == D_MODEL and meta.shape == (N_META, D_MODEL)
    assert seq % PROJ_ROWS == 0 and seq % Q_TILE == 0 and PROJ_ROWS % t == 0
    assert (dec_batch * t) % PROJ_ROWS == 0 and t >= POOL_STATE
    n_cache = cache_k.shape[2]
    assert (n_cache - N_META) % CHUNK == 0 and n_cache > SAMPLE_TAIL

    w_in_b = w_in[0].astype(BF16)
    w_pool_b = w_pool[0].astype(BF16)
    w_out_b = w_out[0].astype(BF16)
    w_gate_b = w_gate[0].astype(BF16)
    w_up_b = w_up[0].astype(BF16)
    w_down_b = w_down[0].astype(BF16)
    g_mix2 = g_mix[0].reshape(1, D_MODEL)
    g_ffn2 = g_ffn[0].reshape(1, D_MODEL)
    g_fin2 = g_final.reshape(1, D_MODEL)
    g_sub2 = g_subln[0].reshape(1, HEAD_DIM)
    ps2 = pool_scale[0].reshape(1, POOL_WIDTH)

    diag_b, meta_b, tail_b, lam = _bias_tables(
        rel_bias, lambda_q1, lambda_k1, lambda_q2, lambda_k2, t)
    cfar = rel_bias[len(_BUCKET_THRESHOLDS)]

    xp = x_prompt.reshape(batch * seq, D_MODEL)
    xs = x_sample.reshape(dec_batch * t, D_MODEL)
    u_m, _, k_m, v_m = _project(meta, g_mix2, w_in_b, N_META)
    u_p, q_p, k_p, v_p = _project(xp, g_mix2, w_in_b, PROJ_ROWS)
    u_s, q_s, k_s, v_s = _project(xs, g_mix2, w_in_b, PROJ_ROWS)

    def with_meta(m, frames):
        m = jnp.broadcast_to(m[None], (batch, N_META, ATTN_WIDTH))
        return jnp.concatenate([m, frames.reshape(batch, seq, ATTN_WIDTH)], axis=1)

    k_all = with_meta(k_m, k_p)
    v_all = with_meta(v_m, v_p)

    b_p = _attn_prompt(q_p, k_all, v_all, diag_b, meta_b, cfar, lam, g_sub2, batch, seq)
    b_s = _attn_sample(q_s, k_s, v_s,
                       cache_k[0].reshape(dec_batch, n_cache, ATTN_WIDTH),
                       cache_v[0].reshape(dec_batch, n_cache, ATTN_WIDTH),
                       tail_b, cfar, lam, g_sub2, dec_batch, t)

    h_p = _out_prompt(u_p, u_m, b_p, xp, w_pool_b, ps2, w_out_b, batch, seq)
    state16 = jnp.pad(state_pool[0], ((0, 0), (N_META - POOL_STATE, 0), (0, 0)))
    h_s = _out_sample(u_s, state16, b_s, xs, w_pool_b, ps2, w_out_b, t)

    y_p = _ffn(h_p, g_ffn2, w_gate_b, w_up_b, w_down_b, g_fin2)
    y_s = _ffn(h_s, g_ffn2, w_gate_b, w_up_b, w_down_b, g_fin2)

    hd = (HEADS, HEAD_DIM)
    return (
        y_p.reshape(batch, seq, D_MODEL),
        y_s.reshape(dec_batch, t, D_MODEL),
        k_all.reshape(1, batch, N_META + seq, *hd),
        v_all.reshape(1, batch, N_META + seq, *hd),
        u_p.reshape(batch, seq, POOL_WIDTH)[None, :, seq - POOL_STATE:],
        k_s.reshape(1, dec_batch, t, *hd),
        v_s.reshape(1, dec_batch, t, *hd),
        u_s.reshape(dec_batch, t, POOL_WIDTH)[None, :, t - POOL_STATE:],
    )
```

```python
import functools
import math

import jax
import jax.numpy as jnp
from jax import lax
from jax.experimental import pallas as pl
from jax.experimental.pallas import tpu as pltpu

F32 = jnp.float32
BF16 = jnp.bfloat16

D_MODEL = 2048
CHUNK = 64
N_META = 16
POOL_WINDOWS = (2, 4, 8, 16)
POOL_GROUP_DIM = 256
POOL_WIDTH = 1024
POOL_STATE = 15
HEADS = 8
HEAD_DIM = 128
HALF_DIM = 64
ATTN_WIDTH = 1024
ATT_SCALE = HALF_DIM ** -0.5
D_FF = 5632
EPS = 1e-6
SUBLN_EPS = 1e-5
NEG_INF = -1e30
LAMBDA_INIT = 0.8 - 0.6 * math.exp(-0.3 * 0)

_BUCKET_THRESHOLDS = (1, 2, 3, 4, 5, 6, 7, 8, 12, 16, 23, 32, 46, 64, 91)

Q_TILE = 256
PROJ_ROWS = 512
FFN_ROWS = 512
FFN_COLS = 512
SAMPLE_TAIL = 272
VMEM_LIMIT = 56 * 1024 * 1024


def _rms(x, g, eps):
    ms = jnp.mean(x * x, axis=-1, keepdims=True)
    return x * lax.rsqrt(ms + eps) * g


def _dot(a, b):
    return jnp.dot(a, b, preferred_element_type=F32)


def _dot_nt(a, b):
    return lax.dot_general(a, b, (((1,), (1,)), ((), ())), preferred_element_type=F32)


def _resident(shape):
    nd = len(shape)
    return pl.BlockSpec(shape, lambda *_: (0,) * nd, pipeline_mode=pl.Buffered(1))


def _smem():
    return pl.BlockSpec(memory_space=pltpu.SMEM)


def _bias_of_rel(rel, table_ref, h):
    n = jnp.abs(rel)
    neg = jnp.full(rel.shape, table_ref[0, h], F32)
    pos = jnp.full(rel.shape, table_ref[16, h], F32)
    for b, thr in enumerate(_BUCKET_THRESHOLDS, start=1):
        ge = n >= thr
        neg = jnp.where(ge, table_ref[b, h], neg)
        pos = jnp.where(ge, table_ref[16 + b, h], pos)
    return jnp.where(rel > 0, pos, neg)


def _bias_kernel(table_ref, lq1_ref, lk1_ref, lq2_ref, lk2_ref,
                 diag_ref, meta_ref, tail_ref, lam_ref):
    h = pl.program_id(0)
    t = Q_TILE
    qi = lax.broadcasted_iota(jnp.int32, (t, t), 0)
    kj = lax.broadcasted_iota(jnp.int32, (t, t), 1)
    visible = (kj // CHUNK) <= (qi // CHUNK)
    diag_ref[0, 0] = jnp.where(visible, _bias_of_rel(kj - qi, table_ref, h), NEG_INF)
    diag_ref[0, 1] = _bias_of_rel(kj - t - qi, table_ref, h)
    qm = lax.broadcasted_iota(jnp.int32, (t, N_META), 0)
    km = lax.broadcasted_iota(jnp.int32, (t, N_META), 1)
    meta_ref[0] = _bias_of_rel(km - N_META - qm, table_ref, h)
    qs = lax.broadcasted_iota(jnp.int32, tail_ref.shape[1:], 0)
    ks = lax.broadcasted_iota(jnp.int32, tail_ref.shape[1:], 1)
    tail_ref[0] = _bias_of_rel(ks - SAMPLE_TAIL - qs, table_ref, h)
    s1 = jnp.sum(lq1_ref[...] * lk1_ref[...], axis=-1, keepdims=True)
    s2 = jnp.sum(lq2_ref[...] * lk2_ref[...], axis=-1, keepdims=True)
    lam_ref[...] = jnp.exp(s1) - jnp.exp(s2) + LAMBDA_INIT


def _bias_tables(rel_bias, lq1, lk1, lq2, lk2, n_new):
    tail_w = SAMPLE_TAIL + n_new
    vec = pl.BlockSpec((1, HALF_DIM), lambda h: (0, 0))
    return pl.pallas_call(
        _bias_kernel,
        grid=(HEADS,),
        in_specs=[_smem(), vec, vec, vec, vec],
        out_specs=[
            pl.BlockSpec((1, 2, Q_TILE, Q_TILE), lambda h: (h, 0, 0, 0)),
            pl.BlockSpec((1, Q_TILE, N_META), lambda h: (h, 0, 0)),
            pl.BlockSpec((1, n_new, tail_w), lambda h: (h, 0, 0)),
            pl.BlockSpec((1, 1), lambda h: (0, 0)),
        ],
        out_shape=[
            jax.ShapeDtypeStruct((HEADS, 2, Q_TILE, Q_TILE), F32),
            jax.ShapeDtypeStruct((HEADS, Q_TILE, N_META), F32),
            jax.ShapeDtypeStruct((HEADS, n_new, tail_w), F32),
            jax.ShapeDtypeStruct((1, 1), F32),
        ],
        name="bias_tables",
    )(rel_bias, lq1, lk1, lq2, lk2)


def _proj_kernel(x_ref, g_ref, w_ref, u_ref, q_ref, k_ref, v_ref):
    xn = _rms(x_ref[...], g_ref[...], EPS).astype(BF16)
    w = POOL_WIDTH
    u_ref[...] = _dot(xn, w_ref[:, 0 * w:1 * w])
    q_ref[...] = (_dot(xn, w_ref[:, 1 * w:2 * w]) * ATT_SCALE).astype(BF16)
    k_ref[...] = _dot(xn, w_ref[:, 2 * w:3 * w])
    v_ref[...] = _dot(xn, w_ref[:, 3 * w:4 * w])


def _project(x, g, w_bf16, rows):
    n = x.shape[0]
    w = POOL_WIDTH
    row_in = pl.BlockSpec((rows, D_MODEL), lambda i: (i, 0))
    row_out = pl.BlockSpec((rows, w), lambda i: (i, 0))
    return pl.pallas_call(
        _proj_kernel,
        grid=(n // rows,),
        in_specs=[row_in, _resident((1, D_MODEL)), _resident(w_bf16.shape)],
        out_specs=[row_out, row_out, row_out, row_out],
        out_shape=[
            jax.ShapeDtypeStruct((n, w), F32),
            jax.ShapeDtypeStruct((n, w), BF16),
            jax.ShapeDtypeStruct((n, w), F32),
            jax.ShapeDtypeStruct((n, w), F32),
        ],
        compiler_params=pltpu.CompilerParams(
            dimension_semantics=("parallel",), vmem_limit_bytes=VMEM_LIMIT),
        name="in_proj",
    )(x, g, w_bf16)


def _stack_maps(q):
    lane = lax.broadcasted_iota(jnp.int32, q.shape, 1)
    zero = jnp.zeros_like(q)
    return jnp.concatenate(
        [jnp.where(lane < HALF_DIM, q, zero), jnp.where(lane >= HALF_DIM, q, zero)], axis=0)


def _finish_head(o, gs):
    return (_rms(o, gs, SUBLN_EPS) * (1.0 - LAMBDA_INIT)).astype(BF16)


def _attn_prompt_kernel(cfar_ref, lam_ref, q_ref, k_ref, v_ref, diag_ref, meta_ref, gs_ref,
                        o_ref, kb_ref, vb_ref, s_ref):
    h = pl.program_id(1)
    tq = Q_TILE
    cfar = cfar_ref[h]
    lam = lam_ref[0, 0]
    kb_ref[...] = k_ref[0].astype(BF16)
    vb_ref[...] = v_ref[0].astype(BF16)
    for c in range(q_ref.shape[0] // tq):
        slot = c % 2
        n = (c + 1) * tq
        q2 = _stack_maps(q_ref[c * tq:(c + 1) * tq, :])
        for j in range(c + 1):
            d = _dot_nt(q2, kb_ref[N_META + j * tq:N_META + (j + 1) * tq, :])
            cols = slice(j * tq, (j + 1) * tq)
            if j >= c - 1:
                bias = diag_ref[0, c - j]
                s_ref[slot, :tq, cols] = d[:tq] + bias
                s_ref[slot, tq:, cols] = d[tq:] + bias
            else:
                s_ref[slot, :, cols] = d + cfar
        sm = _dot_nt(q2, kb_ref[0:N_META, :])
        if c == 0:
            sm = sm + jnp.concatenate([meta_ref[0], meta_ref[0]], axis=0)
        else:
            sm = sm + cfar
        m = jnp.maximum(jnp.max(s_ref[slot, :, :n], axis=-1, keepdims=True),
                        jnp.max(sm, axis=-1, keepdims=True))
        p = jnp.exp(s_ref[slot, :, :n] - m)
        pm = jnp.exp(sm - m)
        l = jnp.sum(p, axis=-1, keepdims=True) + jnp.sum(pm, axis=-1, keepdims=True)
        s_ref[slot, :, :n] = p
        r = 1.0 / l
        r1 = r[:tq]
        r2 = lam * r[tq:]
        w = (s_ref[slot, :tq, :n] * r1 - s_ref[slot, tq:, :n] * r2).astype(BF16)
        wm = (pm[:tq] * r1 - pm[tq:] * r2).astype(BF16)
        o = _dot(w, vb_ref[N_META:N_META + n, :]) + _dot(wm, vb_ref[0:N_META, :])
        o_ref[c * tq:(c + 1) * tq, :] = _finish_head(o, gs_ref[...])


def _attn_prompt(q, k_all, v_all, diag, meta_b, cfar, lam, gs, batch, seq):
    tq = Q_TILE
    kv = pl.BlockSpec((1, N_META + seq, HEAD_DIM), lambda b, h: (b, 0, h))
    qo = pl.BlockSpec((seq, HEAD_DIM), lambda b, h: (b, h))
    return pl.pallas_call(
        _attn_prompt_kernel,
        grid=(batch, HEADS),
        in_specs=[
            _smem(), _smem(), qo, kv, kv,
            pl.BlockSpec((1, 2, tq, tq), lambda b, h: (h, 0, 0, 0)),
            pl.BlockSpec((1, tq, N_META), lambda b, h: (h, 0, 0)),
            pl.BlockSpec((1, HEAD_DIM), lambda b, h: (0, 0)),
        ],
        out_specs=qo,
        out_shape=jax.ShapeDtypeStruct((batch * seq, ATTN_WIDTH), BF16),
        scratch_shapes=[
            pltpu.VMEM((N_META + seq, HEAD_DIM), BF16),
            pltpu.VMEM((N_META + seq, HEAD_DIM), BF16),
            pltpu.VMEM((2, 2 * tq, seq), F32),
        ],
        compiler_params=pltpu.CompilerParams(
            dimension_semantics=("parallel", "parallel"), vmem_limit_bytes=VMEM_LIMIT),
        name="attn_prompt",
    )(cfar, lam, q, k_all, v_all, diag, meta_b, gs)


def _attn_sample_kernel(cfar_ref, lam_ref, q_ref, kn_ref, vn_ref, ck_ref, cv_ref, tail_ref, gs_ref,
                        o_ref):
    t = q_ref.shape[0]
    n_cache = ck_ref.shape[1]
    split = n_cache - SAMPLE_TAIL
    lam = lam_ref[0, 0]
    for h in range(HEADS):
        cs = slice(h * HEAD_DIM, (h + 1) * HEAD_DIM)
        q2 = _stack_maps(q_ref[:, cs])
        ka = ck_ref[0, 0:split, cs].astype(BF16)
        va = cv_ref[0, 0:split, cs].astype(BF16)
        kt = jnp.concatenate([ck_ref[0, split:n_cache, cs].astype(BF16), kn_ref[:, cs].astype(BF16)], axis=0)
        vt = jnp.concatenate([cv_ref[0, split:n_cache, cs].astype(BF16), vn_ref[:, cs].astype(BF16)], axis=0)
        bt = tail_ref[h]
        sa = _dot_nt(q2, ka) + cfar_ref[h]
        st = _dot_nt(q2, kt) + jnp.concatenate([bt, bt], axis=0)
        m = jnp.maximum(jnp.max(sa, axis=-1, keepdims=True), jnp.max(st, axis=-1, keepdims=True))
        pa = jnp.exp(sa - m)
        pt = jnp.exp(st - m)
        r = 1.0 / (jnp.sum(pa, axis=-1, keepdims=True) + jnp.sum(pt, axis=-1, keepdims=True))
        r1 = r[:t]
        r2 = lam * r[t:]
        wa = (pa[:t] * r1 - pa[t:] * r2).astype(BF16)
        wt = (pt[:t] * r1 - pt[t:] * r2).astype(BF16)
        o = _dot(wa, va) + _dot(wt, vt)
        o_ref[:, cs] = _finish_head(o, gs_ref[...])


def _attn_sample(q, k_new, v_new, cache_k, cache_v, tail_b, cfar, lam, gs, batch, t):
    n_cache = cache_k.shape[1]
    row = pl.BlockSpec((t, ATTN_WIDTH), lambda b: (b, 0))
    cache = pl.BlockSpec((1, n_cache, ATTN_WIDTH), lambda b: (b, 0, 0))
    return pl.pallas_call(
        _attn_sample_kernel,
        grid=(batch,),
        in_specs=[_smem(), _smem(), row, row, row, cache, cache,
                  _resident(tail_b.shape), _resident((1, HEAD_DIM))],
        out_specs=row,
        out_shape=jax.ShapeDtypeStruct((batch * t, ATTN_WIDTH), BF16),
        compiler_params=pltpu.CompilerParams(
            dimension_semantics=("parallel",), vmem_limit_bytes=VMEM_LIMIT),
        name="attn_sample",
    )(cfar, lam, q, k_new, v_new, cache_k, cache_v, tail_b, gs)


def _pool_mix(ext, wp_ref, scale):
    outs = []
    for g, w in enumerate(POOL_WINDOWS):
        xg = ext[:, g * POOL_GROUP_DIM:(g + 1) * POOL_GROUP_DIM]
        s = xg
        span = 1
        while span < w:
            s = s + pltpu.roll(s, span, axis=0)
            span *= 2
        p = (s[N_META:] * (1.0 / w) - xg[N_META:]).astype(BF16)
        outs.append(_dot(p, wp_ref[g]))
    return jnp.concatenate(outs, axis=-1) * scale


def _mix_out(a, b_ref, x_ref, wo_ref, o_ref):
    mix = jnp.concatenate([a.astype(BF16), b_ref[...]], axis=-1)
    o_ref[...] = x_ref[...] + _dot(mix, wo_ref[...])


def _out_prompt_kernel(u_ref, halo_ref, umeta_ref, b_ref, x_ref, wp_ref, ps_ref, wo_ref, o_ref):
    first = pl.program_id(1) == 0
    halo = jnp.where(first, umeta_ref[...], halo_ref[...])
    ext = jnp.concatenate([halo, u_ref[...]], axis=0)
    _mix_out(_pool_mix(ext, wp_ref, ps_ref[...]), b_ref, x_ref, wo_ref, o_ref)


def _out_sample_kernel(u_ref, state_ref, b_ref, x_ref, wp_ref, ps_ref, wo_ref, o_ref):
    n_seq, t = state_ref.shape[0], u_ref.shape[0] // state_ref.shape[0]
    parts = []
    for s in range(n_seq):
        ext = jnp.concatenate([state_ref[s], u_ref[s * t:(s + 1) * t, :]], axis=0)
        parts.append(_pool_mix(ext, wp_ref, ps_ref[...]))
    _mix_out(jnp.concatenate(parts, axis=0), b_ref, x_ref, wo_ref, o_ref)


def _out_common_specs(rows, idx):
    return (pl.BlockSpec((rows, ATTN_WIDTH), idx), pl.BlockSpec((rows, D_MODEL), idx))


def _out_prompt(u, u_meta, b, x, wp, ps, wo, batch, seq):
    rows = PROJ_ROWS
    nt = seq // rows
    idx = lambda bi, i: (bi * nt + i, 0)
    halo_blocks = rows // N_META
    halo_idx = lambda bi, i: (jnp.maximum((bi * nt + i) * halo_blocks - 1, 0), 0)
    b_spec, x_spec = _out_common_specs(rows, idx)
    return pl.pallas_call(
        _out_prompt_kernel,
        grid=(batch, nt),
        in_specs=[
            pl.BlockSpec((rows, POOL_WIDTH), idx),
            pl.BlockSpec((N_META, POOL_WIDTH), halo_idx),
            _resident(u_meta.shape), b_spec, x_spec,
            _resident(wp.shape), _resident(ps.shape), _resident(wo.shape),
        ],
        out_specs=x_spec,
        out_shape=jax.ShapeDtypeStruct(x.shape, F32),
        compiler_params=pltpu.CompilerParams(
            dimension_semantics=("parallel", "parallel"), vmem_limit_bytes=VMEM_LIMIT),
        name="out_proj_prompt",
    )(u, u, u_meta, b, x, wp, ps, wo)


def _out_sample(u, state16, b, x, wp, ps, wo, t):
    rows = PROJ_ROWS
    n_seq = rows // t
    idx = lambda i: (i, 0)
    b_spec, x_spec = _out_common_specs(rows, idx)
    return pl.pallas_call(
        _out_sample_kernel,
        grid=(x.shape[0] // rows,),
        in_specs=[
            pl.BlockSpec((rows, POOL_WIDTH), idx),
            pl.BlockSpec((n_seq, N_META, POOL_WIDTH), lambda i: (i, 0, 0)),
            b_spec, x_spec,
            _resident(wp.shape), _resident(ps.shape), _resident(wo.shape),
        ],
        out_specs=x_spec,
        out_shape=jax.ShapeDtypeStruct(x.shape, F32),
        compiler_params=pltpu.CompilerParams(
            dimension_semantics=("parallel",), vmem_limit_bytes=VMEM_LIMIT),
        name="out_proj_sample",
    )(u, state16, b, x, wp, ps, wo)


def _ffn_kernel(h_ref, gf_ref, wg_ref, wu_ref, wd_ref, gfin_ref, y_ref, n_ref):
    f = pl.program_id(1)

    @pl.when(f == 0)
    def _():
        h = h_ref[...]
        n_ref[...] = _rms(h, gf_ref[...], EPS).astype(BF16)
        y_ref[...] = h

    n = n_ref[...]
    g = _dot(n, wg_ref[...])
    u = _dot(n, wu_ref[...])
    act = (g * (1.0 / (1.0 + jnp.exp(-g))) * u).astype(BF16)
    y_ref[...] += _dot(act, wd_ref[...])

    @pl.when(f == pl.num_programs(1) - 1)
    def _():
        y_ref[...] = _rms(y_ref[...], gfin_ref[...], EPS)


def _ffn(h, gf, wg, wu, wd, gfin):
    rows, cols = FFN_ROWS, FFN_COLS
    row = pl.BlockSpec((rows, D_MODEL), lambda i, f: (i, 0))
    vec = pl.BlockSpec((1, D_MODEL), lambda i, f: (0, 0))
    return pl.pallas_call(
        _ffn_kernel,
        grid=(h.shape[0] // rows, D_FF // cols),
        in_specs=[
            row, vec,
            pl.BlockSpec((D_MODEL, cols), lambda i, f: (0, f)),
            pl.BlockSpec((D_MODEL, cols), lambda i, f: (0, f)),
            pl.BlockSpec((cols, D_MODEL), lambda i, f: (f, 0)),
            vec,
        ],
        out_specs=row,
        out_shape=jax.ShapeDtypeStruct(h.shape, F32),
        scratch_shapes=[pltpu.VMEM((rows, D_MODEL), BF16)],
        compiler_params=pltpu.CompilerParams(
            dimension_semantics=("parallel", "arbitrary"), vmem_limit_bytes=VMEM_LIMIT),
        name="swiglu",
    )(h, gf, wg, wu, wd, gfin)


def kernel(x_prompt, x_sample, cache_k, cache_v, state_pool, meta, g_mix, w_in, w_pool,
           pool_scale, lambda_q1, lambda_k1, lambda_q2, lambda_k2, g_subln, w_out, g_ffn,
           w_gate, w_up, w_down, rel_bias, g_final):
    batch, seq, d = x_prompt.shape
    dec_batch, t, _ = x_sample.shape
    depth = w_in.shape[0]
    assert depth == 1 and d == D_MODEL and meta.shape == (N_META, D_MODEL)
    assert seq % PROJ_ROWS == 0 and seq % Q_TILE == 0 and PROJ_ROWS % t == 0
    assert (dec_batch * t) % PROJ_ROWS == 0 and t >= POOL_STATE
    n_cache = cache_k.shape[2]
    assert (n_cache - N_META) % CHUNK == 0 and n_cache > SAMPLE_TAIL

    w_in_b = w_in[0].astype(BF16)
    w_pool_b = w_pool[0].astype(BF16)
    w_out_b = w_out[0].astype(BF16)
    w_gate_b = w_gate[0].astype(BF16)
    w_up_b = w_up[0].astype(BF16)
    w_down_b = w_down[0].astype(BF16)
    g_mix2 = g_mix[0].reshape(1, D_MODEL)
    g_ffn2 = g_ffn[0].reshape(1, D_MODEL)
    g_fin2 = g_final.reshape(1, D_MODEL)
    g_sub2 = g_subln[0].reshape(1, HEAD_DIM)
    ps2 = pool_scale[0].reshape(1, POOL_WIDTH)

    diag_b, meta_b, tail_b, lam = _bias_tables(
        rel_bias, lambda_q1, lambda_k1, lambda_q2, lambda_k2, t)
    cfar = rel_bias[len(_BUCKET_THRESHOLDS)]

    xp = x_prompt.reshape(batch * seq, D_MODEL)
    xs = x_sample.reshape(dec_batch * t, D_MODEL)
    u_m, _, k_m, v_m = _project(meta, g_mix2, w_in_b, N_META)
    u_p, q_p, k_p, v_p = _project(xp, g_mix2, w_in_b, PROJ_ROWS)
    u_s, q_s, k_s, v_s = _project(xs, g_mix2, w_in_b, PROJ_ROWS)

    def with_meta(m, frames):
        m = jnp.broadcast_to(m[None], (batch, N_META, ATTN_WIDTH))
        return jnp.concatenate([m, frames.reshape(batch, seq, ATTN_WIDTH)], axis=1)

    k_all = with_meta(k_m, k_p)
    v_all = with_meta(v_m, v_p)

    b_p = _attn_prompt(q_p, k_all, v_all, diag_b, meta_b, cfar, lam, g_sub2, batch, seq)
    b_s = _attn_sample(q_s, k_s, v_s,
                       cache_k[0].reshape(dec_batch, n_cache, ATTN_WIDTH),
                       cache_v[0].reshape(dec_batch, n_cache, ATTN_WIDTH),
                       tail_b, cfar, lam, g_sub2, dec_batch, t)

    h_p = _out_prompt(u_p, u_m, b_p, xp, w_pool_b, ps2, w_out_b, batch, seq)
    state16 = jnp.pad(state_pool[0], ((0, 0), (N_META - POOL_STATE, 0), (0, 0)))
    h_s = _out_sample(u_s, state16, b_s, xs, w_pool_b, ps2, w_out_b, t)

    y_p = _ffn(h_p, g_ffn2, w_gate_b, w_up_b, w_down_b, g_fin2)
    y_s = _ffn(h_s, g_ffn2, w_gate_b, w_up_b, w_down_b, g_fin2)

    hd = (HEADS, HEAD_DIM)
    return (
        y_p.reshape(batch, seq, D_MODEL),
        y_s.reshape(dec_batch, t, D_MODEL),
        k_all.reshape(1, batch, N_META + seq, *hd),
        v_all.reshape(1, batch, N_META + seq, *hd),
        u_p.reshape(batch, seq, POOL_WIDTH)[None, :, seq - POOL_STATE:],
        k_s.reshape(1, dec_batch, t, *hd),
        v_s.reshape(1, dec_batch, t, *hd),
        u_s.reshape(dec_batch, t, POOL_WIDTH)[None, :, t - POOL_STATE:],
    )
```

```python
import functools
import math

import jax
import jax.numpy as jnp
from jax import lax
from jax.experimental import pallas as pl
from jax.experimental.pallas import tpu as pltpu

F32 = jnp.float32
BF16 = jnp.bfloat16

D_MODEL = 2048
CHUNK = 64
N_META = 16
POOL_WINDOWS = (2, 4, 8, 16)
POOL_GROUP_DIM = 256
POOL_WIDTH = 1024
POOL_STATE = 15
HEADS = 8
HEAD_DIM = 128
HALF_DIM = 64
ATTN_WIDTH = 1024
ATT_SCALE = HALF_DIM ** -0.5
D_FF = 5632
EPS = 1e-6
SUBLN_EPS = 1e-5
NEG_INF = -1e30
LAMBDA_INIT = 0.8 - 0.6 * math.exp(-0.3 * 0)

_BUCKET_THRESHOLDS = (1, 2, 3, 4, 5, 6, 7, 8, 12, 16, 23, 32, 46, 64, 91)

Q_TILE = 256
PROJ_ROWS = 512
FFN_ROWS = 512
FFN_COLS = 512
SAMPLE_TAIL = 272
VMEM_LIMIT = 56 * 1024 * 1024


def _rms(x, g, eps):
    ms = jnp.mean(x * x, axis=-1, keepdims=True)
    return x * lax.rsqrt(ms + eps) * g


def _dot(a, b):
    return jnp.dot(a, b, preferred_element_type=F32)


def _dot_nt(a, b):
    return lax.dot_general(a, b, (((1,), (1,)), ((), ())), preferred_element_type=F32)


def _resident(shape):
    nd = len(shape)
    return pl.BlockSpec(shape, lambda *_: (0,) * nd, pipeline_mode=pl.Buffered(1))


def _smem():
    return pl.BlockSpec(memory_space=pltpu.SMEM)


def _bias_of_rel(rel, table_ref, h):
    n = jnp.abs(rel)
    neg = jnp.full(rel.shape, table_ref[0, h], F32)
    pos = jnp.full(rel.shape, table_ref[16, h], F32)
    for b, thr in enumerate(_BUCKET_THRESHOLDS, start=1):
        ge = n >= thr
        neg = jnp.where(ge, table_ref[b, h], neg)
        pos = jnp.where(ge, table_ref[16 + b, h], pos)
    return jnp.where(rel > 0, pos, neg)


def _bias_kernel(table_ref, lq1_ref, lk1_ref, lq2_ref, lk2_ref,
                 diag_ref, meta_ref, tail_ref, lam_ref):
    h = pl.program_id(0)
    t = Q_TILE
    qi = lax.broadcasted_iota(jnp.int32, (t, t), 0)
    kj = lax.broadcasted_iota(jnp.int32, (t, t), 1)
    visible = (kj // CHUNK) <= (qi // CHUNK)
    diag_ref[0, 0] = jnp.where(visible, _bias_of_rel(kj - qi, table_ref, h), NEG_INF)
    diag_ref[0, 1] = _bias_of_rel(kj - t - qi, table_ref, h)
    qm = lax.broadcasted_iota(jnp.int32, (t, N_META), 0)
    km = lax.broadcasted_iota(jnp.int32, (t, N_META), 1)
    meta_ref[0] = _bias_of_rel(km - N_META - qm, table_ref, h)
    qs = lax.broadcasted_iota(jnp.int32, tail_ref.shape[1:], 0)
    ks = lax.broadcasted_iota(jnp.int32, tail_ref.shape[1:], 1)
    tail_ref[0] = _bias_of_rel(ks - SAMPLE_TAIL - qs, table_ref, h)
    s1 = jnp.sum(lq1_ref[...] * lk1_ref[...], axis=-1, keepdims=True)
    s2 = jnp.sum(lq2_ref[...] * lk2_ref[...], axis=-1, keepdims=True)
    lam_ref[...] = jnp.exp(s1) - jnp.exp(s2) + LAMBDA_INIT


def _bias_tables(rel_bias, lq1, lk1, lq2, lk2, n_new):
    tail_w = SAMPLE_TAIL + n_new
    vec = pl.BlockSpec((1, HALF_DIM), lambda h: (0, 0))
    return pl.pallas_call(
        _bias_kernel,
        grid=(HEADS,),
        in_specs=[_smem(), vec, vec, vec, vec],
        out_specs=[
            pl.BlockSpec((1, 2, Q_TILE, Q_TILE), lambda h: (h, 0, 0, 0)),
            pl.BlockSpec((1, Q_TILE, N_META), lambda h: (h, 0, 0)),
            pl.BlockSpec((1, n_new, tail_w), lambda h: (h, 0, 0)),
            pl.BlockSpec((1, 1), lambda h: (0, 0)),
        ],
        out_shape=[
            jax.ShapeDtypeStruct((HEADS, 2, Q_TILE, Q_TILE), F32),
            jax.ShapeDtypeStruct((HEADS, Q_TILE, N_META), F32),
            jax.ShapeDtypeStruct((HEADS, n_new, tail_w), F32),
            jax.ShapeDtypeStruct((1, 1), F32),
        ],
        name="bias_tables",
    )(rel_bias, lq1, lk1, lq2, lk2)


def _store_heads(o_ref, x):
    rows = x.shape[0]
    for h in range(HEADS):
        o_ref[pl.ds(h, rows, stride=HEADS), :] = x[:, h * HEAD_DIM:(h + 1) * HEAD_DIM]


def _proj_kernel(x_ref, g_ref, w_ref, *rest):
    u_ref, q_ref, k_ref, v_ref = rest[-4:]
    xn = _rms(x_ref[...], g_ref[...], EPS).astype(BF16)
    w = POOL_WIDTH
    u_ref[...] = _dot(xn, w_ref[:, 0 * w:1 * w])
    q_ref[...] = (_dot(xn, w_ref[:, 1 * w:2 * w]) * ATT_SCALE).astype(BF16)
    _store_heads(k_ref, _dot(xn, w_ref[:, 2 * w:3 * w]))
    _store_heads(v_ref, _dot(xn, w_ref[:, 3 * w:4 * w]))


def _proj_call(grid, x_spec, uq_spec, kv_spec, n, kv_shape, w_bf16, extra_specs=(), aliases=None):
    kv = jax.ShapeDtypeStruct(kv_shape, F32)
    return pl.pallas_call(
        _proj_kernel,
        grid=grid,
        in_specs=[x_spec, _resident((1, D_MODEL)), _resident(w_bf16.shape), *extra_specs],
        out_specs=[uq_spec, uq_spec, kv_spec, kv_spec],
        out_shape=[jax.ShapeDtypeStruct((n, POOL_WIDTH), F32),
                   jax.ShapeDtypeStruct((n, POOL_WIDTH), BF16), kv, kv],
        input_output_aliases=aliases or {},
        compiler_params=pltpu.CompilerParams(
            dimension_semantics=("arbitrary",) * len(grid), vmem_limit_bytes=VMEM_LIMIT),
        name="in_proj",
    )


def _project_rows(x, g, w_bf16, rows):
    n = x.shape[0]
    return _proj_call(
        (n // rows,),
        pl.BlockSpec((rows, D_MODEL), lambda i: (i, 0)),
        pl.BlockSpec((rows, POOL_WIDTH), lambda i: (i, 0)),
        pl.BlockSpec((rows * HEADS, HEAD_DIM), lambda i: (i, 0)),
        n, (n * HEADS, HEAD_DIM), w_bf16)(x, g, w_bf16)


def _project_meta(meta, g, w_bf16, batch, seq_rows):
    same = lambda b: (0, 0)
    return _proj_call(
        (batch,),
        pl.BlockSpec((N_META, D_MODEL), same),
        pl.BlockSpec((N_META, POOL_WIDTH), same),
        pl.BlockSpec((None, N_META * HEADS, HEAD_DIM), lambda b: (b, 0, 0)),
        N_META, (batch, seq_rows * HEADS, HEAD_DIM), w_bf16)(meta, g, w_bf16)


def _project_prompt(x, g, w_bf16, k_buf, v_buf, rows, batch, seq):
    nt = seq // rows
    row_idx = lambda b, i: (b * nt + i, 0)
    any_spec = pl.BlockSpec(memory_space=pl.ANY)
    kv_spec = pl.BlockSpec(
        (pl.Squeezed(), pl.Element(rows * HEADS), pl.Element(HEAD_DIM)),
        lambda b, i: (b, (N_META + i * rows) * HEADS, 0))
    return _proj_call(
        (batch, nt),
        pl.BlockSpec((rows, D_MODEL), row_idx),
        pl.BlockSpec((rows, POOL_WIDTH), row_idx),
        kv_spec, batch * seq, k_buf.shape, w_bf16,
        extra_specs=(any_spec, any_spec), aliases={3: 2, 4: 3})(x, g, w_bf16, k_buf, v_buf)


def _stack_maps(q):
    lane = lax.broadcasted_iota(jnp.int32, q.shape, 1)
    zero = jnp.zeros_like(q)
    return jnp.concatenate(
        [jnp.where(lane < HALF_DIM, q, zero), jnp.where(lane >= HALF_DIM, q, zero)], axis=0)


def _finish_head(o, gs):
    return (_rms(o, gs, SUBLN_EPS) * (1.0 - LAMBDA_INIT)).astype(BF16)


def _attn_prompt_kernel(cfar_ref, lam_ref, q_ref, k_ref, v_ref, diag_ref, meta_ref, gs_ref,
                        o_ref, kb_ref, vb_ref, s_ref):
    h = pl.program_id(1)
    tq = Q_TILE
    cfar = cfar_ref[h]
    lam = lam_ref[0, 0]
    head_rows = pl.ds(h, kb_ref.shape[0], stride=HEADS)
    kb_ref[...] = k_ref[0, head_rows, :].astype(BF16)
    vb_ref[...] = v_ref[0, head_rows, :].astype(BF16)
    for c in range(q_ref.shape[0] // tq):
        slot = c % 2
        n = (c + 1) * tq
        q2 = _stack_maps(q_ref[c * tq:(c + 1) * tq, :])
        for j in range(c + 1):
            d = _dot_nt(q2, kb_ref[N_META + j * tq:N_META + (j + 1) * tq, :])
            cols = slice(j * tq, (j + 1) * tq)
            if j >= c - 1:
                bias = diag_ref[0, c - j]
                s_ref[slot, :tq, cols] = d[:tq] + bias
                s_ref[slot, tq:, cols] = d[tq:] + bias
            else:
                s_ref[slot, :, cols] = d + cfar
        sm = _dot_nt(q2, kb_ref[0:N_META, :])
        if c == 0:
            sm = sm + jnp.concatenate([meta_ref[0], meta_ref[0]], axis=0)
        else:
            sm = sm + cfar
        m = jnp.maximum(jnp.max(s_ref[slot, :, :n], axis=-1, keepdims=True),
                        jnp.max(sm, axis=-1, keepdims=True))
        p = jnp.exp(s_ref[slot, :, :n] - m)
        pm = jnp.exp(sm - m)
        l = jnp.sum(p, axis=-1, keepdims=True) + jnp.sum(pm, axis=-1, keepdims=True)
        s_ref[slot, :, :n] = p
        r = 1.0 / l
        r1 = r[:tq]
        r2 = lam * r[tq:]
        w = (s_ref[slot, :tq, :n] * r1 - s_ref[slot, tq:, :n] * r2).astype(BF16)
        wm = (pm[:tq] * r1 - pm[tq:] * r2).astype(BF16)
        o = _dot(w, vb_ref[N_META:N_META + n, :]) + _dot(wm, vb_ref[0:N_META, :])
        o_ref[c * tq:(c + 1) * tq, :] = _finish_head(o, gs_ref[...])


def _attn_prompt(q, k_all, v_all, diag, meta_b, cfar, lam, gs, batch, seq):
    tq = Q_TILE
    kv = pl.BlockSpec((1, (N_META + seq) * HEADS, HEAD_DIM), lambda b, h: (b, 0, 0))
    qo = pl.BlockSpec((seq, HEAD_DIM), lambda b, h: (b, h))
    return pl.pallas_call(
        _attn_prompt_kernel,
        grid=(batch, HEADS),
        in_specs=[
            _smem(), _smem(), qo, kv, kv,
            pl.BlockSpec((1, 2, tq, tq), lambda b, h: (h, 0, 0, 0)),
            pl.BlockSpec((1, tq, N_META), lambda b, h: (h, 0, 0)),
            pl.BlockSpec((1, HEAD_DIM), lambda b, h: (0, 0)),
        ],
        out_specs=qo,
        out_shape=jax.ShapeDtypeStruct((batch * seq, ATTN_WIDTH), BF16),
        scratch_shapes=[
            pltpu.VMEM((N_META + seq, HEAD_DIM), BF16),
            pltpu.VMEM((N_META + seq, HEAD_DIM), BF16),
            pltpu.VMEM((2, 2 * tq, seq), F32),
        ],
        compiler_params=pltpu.CompilerParams(
            dimension_semantics=("parallel", "parallel"), vmem_limit_bytes=VMEM_LIMIT),
        name="attn_prompt",
    )(cfar, lam, q, k_all, v_all, diag, meta_b, gs)


def _attn_sample_kernel(cfar_ref, lam_ref, q_ref, kn_ref, vn_ref, ck_ref, cv_ref, tail_ref, gs_ref,
                        o_ref):
    t = q_ref.shape[0]
    n_cache = ck_ref.shape[1] // HEADS
    split = n_cache - SAMPLE_TAIL
    lam = lam_ref[0, 0]
    for h in range(HEADS):
        cs = slice(h * HEAD_DIM, (h + 1) * HEAD_DIM)
        q2 = _stack_maps(q_ref[:, cs])
        far = pl.ds(h, split, stride=HEADS)
        near = pl.ds(split * HEADS + h, SAMPLE_TAIL, stride=HEADS)
        new = pl.ds(h, t, stride=HEADS)
        ka = ck_ref[0, far, :].astype(BF16)
        va = cv_ref[0, far, :].astype(BF16)
        kt = jnp.concatenate([ck_ref[0, near, :].astype(BF16), kn_ref[new, :].astype(BF16)], axis=0)
        vt = jnp.concatenate([cv_ref[0, near, :].astype(BF16), vn_ref[new, :].astype(BF16)], axis=0)
        bt = tail_ref[h]
        sa = _dot_nt(q2, ka) + cfar_ref[h]
        st = _dot_nt(q2, kt) + jnp.concatenate([bt, bt], axis=0)
        m = jnp.maximum(jnp.max(sa, axis=-1, keepdims=True), jnp.max(st, axis=-1, keepdims=True))
        pa = jnp.exp(sa - m)
        pt = jnp.exp(st - m)
        r = 1.0 / (jnp.sum(pa, axis=-1, keepdims=True) + jnp.sum(pt, axis=-1, keepdims=True))
        r1 = r[:t]
        r2 = lam * r[t:]
        wa = (pa[:t] * r1 - pa[t:] * r2).astype(BF16)
        wt = (pt[:t] * r1 - pt[t:] * r2).astype(BF16)
        o = _dot(wa, va) + _dot(wt, vt)
        o_ref[:, cs] = _finish_head(o, gs_ref[...])


def _attn_sample(q, k_new, v_new, cache_k, cache_v, tail_b, cfar, lam, gs, batch, t):
    row = pl.BlockSpec((t, ATTN_WIDTH), lambda b: (b, 0))
    new = pl.BlockSpec((t * HEADS, HEAD_DIM), lambda b: (b, 0))
    cache = pl.BlockSpec((1,) + cache_k.shape[1:], lambda b: (b, 0, 0))
    return pl.pallas_call(
        _attn_sample_kernel,
        grid=(batch,),
        in_specs=[_smem(), _smem(), row, new, new, cache, cache,
                  _resident(tail_b.shape), _resident((1, HEAD_DIM))],
        out_specs=row,
        out_shape=jax.ShapeDtypeStruct((batch * t, ATTN_WIDTH), BF16),
        compiler_params=pltpu.CompilerParams(
            dimension_semantics=("parallel",), vmem_limit_bytes=VMEM_LIMIT),
        name="attn_sample",
    )(cfar, lam, q, k_new, v_new, cache_k, cache_v, tail_b, gs)


def _pool_mix(ext, wp_ref, scale):
    outs = []
    for g, w in enumerate(POOL_WINDOWS):
        xg = ext[:, g * POOL_GROUP_DIM:(g + 1) * POOL_GROUP_DIM]
        s = xg
        span = 1
        while span < w:
            s = s + pltpu.roll(s, span, axis=0)
            span *= 2
        p = (s[N_META:] * (1.0 / w) - xg[N_META:]).astype(BF16)
        outs.append(_dot(p, wp_ref[g]))
    return jnp.concatenate(outs, axis=-1) * scale


def _mix_out(a, b_ref, x_ref, wo_ref, o_ref):
    mix = jnp.concatenate([a.astype(BF16), b_ref[...]], axis=-1)
    o_ref[...] = x_ref[...] + _dot(mix, wo_ref[...])


def _out_prompt_kernel(u_ref, halo_ref, umeta_ref, b_ref, x_ref, wp_ref, ps_ref, wo_ref, o_ref):
    first = pl.program_id(1) == 0
    halo = jnp.where(first, umeta_ref[...], halo_ref[...])
    ext = jnp.concatenate([halo, u_ref[...]], axis=0)
    _mix_out(_pool_mix(ext, wp_ref, ps_ref[...]), b_ref, x_ref, wo_ref, o_ref)


def _out_sample_kernel(u_ref, state_ref, b_ref, x_ref, wp_ref, ps_ref, wo_ref, o_ref):
    n_seq, t = state_ref.shape[0], u_ref.shape[0] // state_ref.shape[0]
    parts = []
    for s in range(n_seq):
        ext = jnp.concatenate([state_ref[s], u_ref[s * t:(s + 1) * t, :]], axis=0)
        parts.append(_pool_mix(ext, wp_ref, ps_ref[...]))
    _mix_out(jnp.concatenate(parts, axis=0), b_ref, x_ref, wo_ref, o_ref)


def _out_common_specs(rows, idx):
    return (pl.BlockSpec((rows, ATTN_WIDTH), idx), pl.BlockSpec((rows, D_MODEL), idx))


def _out_prompt(u, u_meta, b, x, wp, ps, wo, batch, seq):
    rows = PROJ_ROWS
    nt = seq // rows
    idx = lambda bi, i: (bi * nt + i, 0)
    halo_blocks = rows // N_META
    halo_idx = lambda bi, i: (jnp.maximum((bi * nt + i) * halo_blocks - 1, 0), 0)
    b_spec, x_spec = _out_common_specs(rows, idx)
    return pl.pallas_call(
        _out_prompt_kernel,
        grid=(batch, nt),
        in_specs=[
            pl.BlockSpec((rows, POOL_WIDTH), idx),
            pl.BlockSpec((N_META, POOL_WIDTH), halo_idx),
            _resident(u_meta.shape), b_spec, x_spec,
            _resident(wp.shape), _resident(ps.shape), _resident(wo.shape),
        ],
        out_specs=x_spec,
        out_shape=jax.ShapeDtypeStruct(x.shape, F32),
        compiler_params=pltpu.CompilerParams(
            dimension_semantics=("parallel", "parallel"), vmem_limit_bytes=VMEM_LIMIT),
        name="out_proj_prompt",
    )(u, u, u_meta, b, x, wp, ps, wo)


def _out_sample(u, state16, b, x, wp, ps, wo, t):
    rows = PROJ_ROWS
    n_seq = rows // t
    idx = lambda i: (i, 0)
    b_spec, x_spec = _out_common_specs(rows, idx)
    return pl.pallas_call(
        _out_sample_kernel,
        grid=(x.shape[0] // rows,),
        in_specs=[
            pl.BlockSpec((rows, POOL_WIDTH), idx),
            pl.BlockSpec((n_seq, N_META, POOL_WIDTH), lambda i: (i, 0, 0)),
            b_spec, x_spec,
            _resident(wp.shape), _resident(ps.shape), _resident(wo.shape),
        ],
        out_specs=x_spec,
        out_shape=jax.ShapeDtypeStruct(x.shape, F32),
        compiler_params=pltpu.CompilerParams(
            dimension_semantics=("parallel",), vmem_limit_bytes=VMEM_LIMIT),
        name="out_proj_sample",
    )(u, state16, b, x, wp, ps, wo)


def _ffn_kernel(h_ref, gf_ref, wg_ref, wu_ref, wd_ref, gfin_ref, y_ref, n_ref):
    f = pl.program_id(1)

    @pl.when(f == 0)
    def _():
        h = h_ref[...]
        n_ref[...] = _rms(h, gf_ref[...], EPS).astype(BF16)
        y_ref[...] = h

    n = n_ref[...]
    g = _dot(n, wg_ref[...])
    u = _dot(n, wu_ref[...])
    act = (g * (1.0 / (1.0 + jnp.exp(-g))) * u).astype(BF16)
    y_ref[...] += _dot(act, wd_ref[...])

    @pl.when(f == pl.num_programs(1) - 1)
    def _():
        y_ref[...] = _rms(y_ref[...], gfin_ref[...], EPS)


def _ffn(h, gf, wg, wu, wd, gfin):
    rows, cols = FFN_ROWS, FFN_COLS
    row = pl.BlockSpec((rows, D_MODEL), lambda i, f: (i, 0))
    vec = pl.BlockSpec((1, D_MODEL), lambda i, f: (0, 0))
    return pl.pallas_call(
        _ffn_kernel,
        grid=(h.shape[0] // rows, D_FF // cols),
        in_specs=[
            row, vec,
            pl.BlockSpec((D_MODEL, cols), lambda i, f: (0, f)),
            pl.BlockSpec((D_MODEL, cols), lambda i, f: (0, f)),
            pl.BlockSpec((cols, D_MODEL), lambda i, f: (f, 0)),
            vec,
        ],
        out_specs=row,
        out_shape=jax.ShapeDtypeStruct(h.shape, F32),
        scratch_shapes=[pltpu.VMEM((rows, D_MODEL), BF16)],
        compiler_params=pltpu.CompilerParams(
            dimension_semantics=("parallel", "arbitrary"), vmem_limit_bytes=VMEM_LIMIT),
        name="swiglu",
    )(h, gf, wg, wu, wd, gfin)


def kernel(x_prompt, x_sample, cache_k, cache_v, state_pool, meta, g_mix, w_in, w_pool,
           pool_scale, lambda_q1, lambda_k1, lambda_q2, lambda_k2, g_subln, w_out, g_ffn,
           w_gate, w_up, w_down, rel_bias, g_final):
    batch, seq, d = x_prompt.shape
    dec_batch, t, _ = x_sample.shape
    depth = w_in.shape[0]
    assert depth == 1 and d == D_MODEL and meta.shape == (N_META, D_MODEL)
    assert seq % PROJ_ROWS == 0 and seq % Q_TILE == 0 and PROJ_ROWS % t == 0
    assert (dec_batch * t) % PROJ_ROWS == 0 and t >= POOL_STATE
    n_cache = cache_k.shape[2]
    assert (n_cache - N_META) % CHUNK == 0 and n_cache > SAMPLE_TAIL

    w_in_b = w_in[0].astype(BF16)
    w_pool_b = w_pool[0].astype(BF16)
    w_out_b = w_out[0].astype(BF16)
    w_gate_b = w_gate[0].astype(BF16)
    w_up_b = w_up[0].astype(BF16)
    w_down_b = w_down[0].astype(BF16)
    g_mix2 = g_mix[0].reshape(1, D_MODEL)
    g_ffn2 = g_ffn[0].reshape(1, D_MODEL)
    g_fin2 = g_final.reshape(1, D_MODEL)
    g_sub2 = g_subln[0].reshape(1, HEAD_DIM)
    ps2 = pool_scale[0].reshape(1, POOL_WIDTH)

    diag_b, meta_b, tail_b, lam = _bias_tables(
        rel_bias, lambda_q1, lambda_k1, lambda_q2, lambda_k2, t)
    cfar = rel_bias[len(_BUCKET_THRESHOLDS)]

    xp = x_prompt.reshape(batch * seq, D_MODEL)
    xs = x_sample.reshape(dec_batch * t, D_MODEL)
    u_m, _, k_all, v_all = _project_meta(meta, g_mix2, w_in_b, batch, N_META + seq)
    u_p, q_p, k_all, v_all = _project_prompt(xp, g_mix2, w_in_b, k_all, v_all, PROJ_ROWS, batch, seq)
    u_s, q_s, k_s, v_s = _project_rows(xs, g_mix2, w_in_b, PROJ_ROWS)

    b_p = _attn_prompt(q_p, k_all, v_all, diag_b, meta_b, cfar, lam, g_sub2, batch, seq)
    b_s = _attn_sample(q_s, k_s, v_s,
                       cache_k[0].reshape(dec_batch, n_cache * HEADS, HEAD_DIM),
                       cache_v[0].reshape(dec_batch, n_cache * HEADS, HEAD_DIM),
                       tail_b, cfar, lam, g_sub2, dec_batch, t)

    h_p = _out_prompt(u_p, u_m, b_p, xp, w_pool_b, ps2, w_out_b, batch, seq)
    state16 = jnp.pad(state_pool[0], ((0, 0), (N_META - POOL_STATE, 0), (0, 0)))
    h_s = _out_sample(u_s, state16, b_s, xs, w_pool_b, ps2, w_out_b, t)

    y_p = _ffn(h_p, g_ffn2, w_gate_b, w_up_b, w_down_b, g_fin2)
    y_s = _ffn(h_s, g_ffn2, w_gate_b, w_up_b, w_down_b, g_fin2)

    hd = (HEADS, HEAD_DIM)
    return (
        y_p.reshape(batch, seq, D_MODEL),
        y_s.reshape(dec_batch, t, D_MODEL),
        k_all.reshape(1, batch, N_META + seq, *hd),
        v_all.reshape(1, batch, N_META + seq, *hd),
        u_p.reshape(batch, seq, POOL_WIDTH)[None, :, seq - POOL_STATE:],
        k_s.reshape(1, dec_batch, t, *hd),
        v_s.reshape(1, dec_batch, t, *hd),
        u_s.reshape(dec_batch, t, POOL_WIDTH)[None, :, t - POOL_STATE:],
    )
```

```python
import functools
import math

import jax
import jax.numpy as jnp
from jax import lax
from jax.experimental import pallas as pl
from jax.experimental.pallas import tpu as pltpu

F32 = jnp.float32
BF16 = jnp.bfloat16

D_MODEL = 2048
CHUNK = 64
N_META = 16
POOL_WINDOWS = (2, 4, 8, 16)
POOL_GROUP_DIM = 256
POOL_WIDTH = 1024
POOL_STATE = 15
HEADS = 8
HEAD_DIM = 128
HALF_DIM = 64
ATTN_WIDTH = 1024
ATT_SCALE = HALF_DIM ** -0.5
D_FF = 5632
EPS = 1e-6
SUBLN_EPS = 1e-5
NEG_INF = -1e30
LAMBDA_INIT = 0.8 - 0.6 * math.exp(-0.3 * 0)

_BUCKET_THRESHOLDS = (1, 2, 3, 4, 5, 6, 7, 8, 12, 16, 23, 32, 46, 64, 91)

Q_TILE = 256
PROJ_ROWS = 512
FFN_ROWS = 1024
FFN_COLS = 512
SAMPLE_TAIL = 272
VMEM_LIMIT = 56 * 1024 * 1024


def _rms(x, g, eps):
    ms = jnp.mean(x * x, axis=-1, keepdims=True)
    return x * lax.rsqrt(ms + eps) * g


def _dot(a, b):
    return jnp.dot(a, b, preferred_element_type=F32)


def _dot_nt(a, b):
    return lax.dot_general(a, b, (((1,), (1,)), ((), ())), preferred_element_type=F32)


def _resident(shape):
    nd = len(shape)
    return pl.BlockSpec(shape, lambda *_: (0,) * nd, pipeline_mode=pl.Buffered(1))


def _smem():
    return pl.BlockSpec(memory_space=pltpu.SMEM)


def _bias_of_rel(rel, table_ref, h):
    n = jnp.abs(rel)
    far = table_ref[len(_BUCKET_THRESHOLDS), h]
    neg = jnp.full(rel.shape, table_ref[0, h] - far, F32)
    pos = jnp.full(rel.shape, table_ref[16, h] - far, F32)
    for b, thr in enumerate(_BUCKET_THRESHOLDS, start=1):
        ge = n >= thr
        neg = jnp.where(ge, table_ref[b, h] - far, neg)
        pos = jnp.where(ge, table_ref[16 + b, h] - far, pos)
    return jnp.where(rel > 0, pos, neg)


def _bias_kernel(table_ref, lq1_ref, lk1_ref, lq2_ref, lk2_ref,
                 diag_ref, meta_ref, tail_ref, lam_ref):
    h = pl.program_id(0)
    t = Q_TILE
    qi = lax.broadcasted_iota(jnp.int32, (t, t), 0)
    kj = lax.broadcasted_iota(jnp.int32, (t, t), 1)
    visible = (kj // CHUNK) <= (qi // CHUNK)
    diag_ref[0, 0] = jnp.where(visible, _bias_of_rel(kj - qi, table_ref, h), NEG_INF)
    diag_ref[0, 1] = _bias_of_rel(kj - t - qi, table_ref, h)
    qm = lax.broadcasted_iota(jnp.int32, (t, N_META), 0)
    km = lax.broadcasted_iota(jnp.int32, (t, N_META), 1)
    meta_ref[0] = _bias_of_rel(km - N_META - qm, table_ref, h)
    qs = lax.broadcasted_iota(jnp.int32, tail_ref.shape[1:], 0)
    ks = lax.broadcasted_iota(jnp.int32, tail_ref.shape[1:], 1)
    tail_ref[0] = _bias_of_rel(ks - SAMPLE_TAIL - qs, table_ref, h)
    s1 = jnp.sum(lq1_ref[...] * lk1_ref[...], axis=-1, keepdims=True)
    s2 = jnp.sum(lq2_ref[...] * lk2_ref[...], axis=-1, keepdims=True)
    lam_ref[...] = jnp.exp(s1) - jnp.exp(s2) + LAMBDA_INIT


def _bias_tables(rel_bias, lq1, lk1, lq2, lk2, n_new):
    tail_w = SAMPLE_TAIL + n_new
    vec = pl.BlockSpec((1, HALF_DIM), lambda h: (0, 0))
    return pl.pallas_call(
        _bias_kernel,
        grid=(HEADS,),
        in_specs=[_smem(), vec, vec, vec, vec],
        out_specs=[
            pl.BlockSpec((1, 2, Q_TILE, Q_TILE), lambda h: (h, 0, 0, 0)),
            pl.BlockSpec((1, Q_TILE, N_META), lambda h: (h, 0, 0)),
            pl.BlockSpec((1, n_new, tail_w), lambda h: (h, 0, 0)),
            pl.BlockSpec((1, 1), lambda h: (0, 0)),
        ],
        out_shape=[
            jax.ShapeDtypeStruct((HEADS, 2, Q_TILE, Q_TILE), F32),
            jax.ShapeDtypeStruct((HEADS, Q_TILE, N_META), F32),
            jax.ShapeDtypeStruct((HEADS, n_new, tail_w), F32),
            jax.ShapeDtypeStruct((1, 1), F32),
        ],
        name="bias_tables",
    )(rel_bias, lq1, lk1, lq2, lk2)


def _store_heads(o_ref, x):
    rows = x.shape[0]
    for h in range(HEADS):
        o_ref[pl.ds(h, rows, stride=HEADS), :] = x[:, h * HEAD_DIM:(h + 1) * HEAD_DIM]


def _proj_kernel(x_ref, g_ref, w_ref, *rest):
    u_ref, q_ref, k_ref, v_ref = rest[-4:]
    xn = _rms(x_ref[...], g_ref[...], EPS).astype(BF16)
    w = POOL_WIDTH
    u_ref[...] = _dot(xn, w_ref[:, 0 * w:1 * w])
    q_ref[...] = (_dot(xn, w_ref[:, 1 * w:2 * w]) * ATT_SCALE).astype(BF16)
    _store_heads(k_ref, _dot(xn, w_ref[:, 2 * w:3 * w]))
    _store_heads(v_ref, _dot(xn, w_ref[:, 3 * w:4 * w]))


def _proj_call(grid, x_spec, uq_spec, kv_spec, n, kv_shape, w_bf16, extra_specs=(), aliases=None):
    kv = jax.ShapeDtypeStruct(kv_shape, F32)
    return pl.pallas_call(
        _proj_kernel,
        grid=grid,
        in_specs=[x_spec, _resident((1, D_MODEL)), _resident(w_bf16.shape), *extra_specs],
        out_specs=[uq_spec, uq_spec, kv_spec, kv_spec],
        out_shape=[jax.ShapeDtypeStruct((n, POOL_WIDTH), F32),
                   jax.ShapeDtypeStruct((n, POOL_WIDTH), BF16), kv, kv],
        input_output_aliases=aliases or {},
        compiler_params=pltpu.CompilerParams(
            dimension_semantics=("arbitrary",) * len(grid), vmem_limit_bytes=VMEM_LIMIT),
        name="in_proj",
    )


def _project_rows(x, g, w_bf16, rows):
    n = x.shape[0]
    return _proj_call(
        (n // rows,),
        pl.BlockSpec((rows, D_MODEL), lambda i: (i, 0)),
        pl.BlockSpec((rows, POOL_WIDTH), lambda i: (i, 0)),
        pl.BlockSpec((rows * HEADS, HEAD_DIM), lambda i: (i, 0)),
        n, (n * HEADS, HEAD_DIM), w_bf16)(x, g, w_bf16)


def _project_meta(meta, g, w_bf16, batch, seq_rows):
    same = lambda b: (0, 0)
    return _proj_call(
        (batch,),
        pl.BlockSpec((N_META, D_MODEL), same),
        pl.BlockSpec((N_META, POOL_WIDTH), same),
        pl.BlockSpec((None, N_META * HEADS, HEAD_DIM), lambda b: (b, 0, 0)),
        N_META, (batch, seq_rows * HEADS, HEAD_DIM), w_bf16)(meta, g, w_bf16)


def _project_prompt(x, g, w_bf16, k_buf, v_buf, rows, batch, seq):
    nt = seq // rows
    row_idx = lambda b, i: (b * nt + i, 0)
    any_spec = pl.BlockSpec(memory_space=pl.ANY)
    kv_spec = pl.BlockSpec(
        (pl.Squeezed(), pl.Element(rows * HEADS), pl.Element(HEAD_DIM)),
        lambda b, i: (b, (N_META + i * rows) * HEADS, 0))
    return _proj_call(
        (batch, nt),
        pl.BlockSpec((rows, D_MODEL), row_idx),
        pl.BlockSpec((rows, POOL_WIDTH), row_idx),
        kv_spec, batch * seq, k_buf.shape, w_bf16,
        extra_specs=(any_spec, any_spec), aliases={3: 2, 4: 3})(x, g, w_bf16, k_buf, v_buf)


def _stack_maps(q):
    lane = lax.broadcasted_iota(jnp.int32, q.shape, 1)
    zero = jnp.zeros_like(q)
    return jnp.concatenate(
        [jnp.where(lane < HALF_DIM, q, zero), jnp.where(lane >= HALF_DIM, q, zero)], axis=0)


def _finish_head(o, gs):
    return (_rms(o, gs, SUBLN_EPS) * (1.0 - LAMBDA_INIT)).astype(BF16)


def _attn_prompt_kernel(lam_ref, q_ref, k_ref, v_ref, diag_ref, meta_ref, gs_ref,
                        o_ref, kb_ref, vb_ref, s_ref):
    tq = Q_TILE
    lam = lam_ref[0, 0]
    head_rows = pl.ds(pl.program_id(1), kb_ref.shape[0], stride=HEADS)
    kb_ref[...] = k_ref[0, head_rows, :].astype(BF16)
    vb_ref[...] = v_ref[0, head_rows, :].astype(BF16)
    for c in range(q_ref.shape[0] // tq):
        slot = c % 2
        n = (c + 1) * tq
        q2 = _stack_maps(q_ref[c * tq:(c + 1) * tq, :])
        for j in range(c + 1):
            d = _dot_nt(q2, kb_ref[N_META + j * tq:N_META + (j + 1) * tq, :])
            cols = slice(j * tq, (j + 1) * tq)
            if j >= c - 1:
                bias = diag_ref[0, c - j]
                s_ref[slot, :tq, cols] = d[:tq] + bias
                s_ref[slot, tq:, cols] = d[tq:] + bias
            else:
                s_ref[slot, :, cols] = d
        sm = _dot_nt(q2, kb_ref[0:N_META, :])
        if c == 0:
            sm = sm + jnp.concatenate([meta_ref[0], meta_ref[0]], axis=0)
        m = jnp.maximum(jnp.max(s_ref[slot, :, :n], axis=-1, keepdims=True),
                        jnp.max(sm, axis=-1, keepdims=True))
        p = jnp.exp(s_ref[slot, :, :n] - m)
        pm = jnp.exp(sm - m)
        l = jnp.sum(p, axis=-1, keepdims=True) + jnp.sum(pm, axis=-1, keepdims=True)
        s_ref[slot, :, :n] = p
        r = 1.0 / l
        r1 = r[:tq]
        r2 = lam * r[tq:]
        w = (s_ref[slot, :tq, :n] * r1 - s_ref[slot, tq:, :n] * r2).astype(BF16)
        wm = (pm[:tq] * r1 - pm[tq:] * r2).astype(BF16)
        o = _dot(w, vb_ref[N_META:N_META + n, :]) + _dot(wm, vb_ref[0:N_META, :])
        o_ref[c * tq:(c + 1) * tq, :] = _finish_head(o, gs_ref[...])


def _attn_prompt(q, k_all, v_all, diag, meta_b, lam, gs, batch, seq):
    tq = Q_TILE
    kv = pl.BlockSpec((1, (N_META + seq) * HEADS, HEAD_DIM), lambda b, h: (b, 0, 0))
    qo = pl.BlockSpec((seq, HEAD_DIM), lambda b, h: (b, h))
    return pl.pallas_call(
        _attn_prompt_kernel,
        grid=(batch, HEADS),
        in_specs=[
            _smem(), qo, kv, kv,
            pl.BlockSpec((1, 2, tq, tq), lambda b, h: (h, 0, 0, 0)),
            pl.BlockSpec((1, tq, N_META), lambda b, h: (h, 0, 0)),
            pl.BlockSpec((1, HEAD_DIM), lambda b, h: (0, 0)),
        ],
        out_specs=qo,
        out_shape=jax.ShapeDtypeStruct((batch * seq, ATTN_WIDTH), BF16),
        scratch_shapes=[
            pltpu.VMEM((N_META + seq, HEAD_DIM), BF16),
            pltpu.VMEM((N_META + seq, HEAD_DIM), BF16),
            pltpu.VMEM((2, 2 * tq, seq), F32),
        ],
        compiler_params=pltpu.CompilerParams(
            dimension_semantics=("parallel", "parallel"), vmem_limit_bytes=VMEM_LIMIT),
        name="attn_prompt",
    )(lam, q, k_all, v_all, diag, meta_b, gs)


def _attn_sample_kernel(lam_ref, q_ref, kn_ref, vn_ref, ck_ref, cv_ref, tail_ref, gs_ref,
                        o_ref, s_ref):
    t = q_ref.shape[0]
    n_cache = ck_ref.shape[1] // HEADS
    split = n_cache - SAMPLE_TAIL
    width = n_cache + t
    lam = lam_ref[0, 0]
    for h in range(HEADS):
        slot = h % 2
        cs = slice(h * HEAD_DIM, (h + 1) * HEAD_DIM)
        q2 = _stack_maps(q_ref[:, cs])
        far = pl.ds(h, split, stride=HEADS)
        near = pl.ds(split * HEADS + h, SAMPLE_TAIL, stride=HEADS)
        new = pl.ds(h, t, stride=HEADS)
        kt = jnp.concatenate([ck_ref[0, near, :].astype(BF16), kn_ref[new, :].astype(BF16)], axis=0)
        vt = jnp.concatenate([cv_ref[0, near, :].astype(BF16), vn_ref[new, :].astype(BF16)], axis=0)
        bt = tail_ref[h]
        s_ref[slot, :, 0:split] = _dot_nt(q2, ck_ref[0, far, :].astype(BF16))
        s_ref[slot, :, split:width] = _dot_nt(q2, kt) + jnp.concatenate([bt, bt], axis=0)
        m = jnp.max(s_ref[slot, :, 0:width], axis=-1, keepdims=True)
        p = jnp.exp(s_ref[slot, :, 0:width] - m)
        r = 1.0 / jnp.sum(p, axis=-1, keepdims=True)
        s_ref[slot, :, 0:width] = p
        r1 = r[:t]
        r2 = lam * r[t:]
        w = (s_ref[slot, :t, 0:width] * r1 - s_ref[slot, t:, 0:width] * r2).astype(BF16)
        o = _dot(w[:, 0:split], cv_ref[0, far, :].astype(BF16)) + _dot(w[:, split:width], vt)
        o_ref[:, cs] = _finish_head(o, gs_ref[...])


def _attn_sample(q, k_new, v_new, cache_k, cache_v, tail_b, lam, gs, batch, t):
    n_cache = cache_k.shape[1] // HEADS
    row = pl.BlockSpec((t, ATTN_WIDTH), lambda b: (b, 0))
    new = pl.BlockSpec((t * HEADS, HEAD_DIM), lambda b: (b, 0))
    cache = pl.BlockSpec((1,) + cache_k.shape[1:], lambda b: (b, 0, 0))
    width = -(-(n_cache + t) // HEAD_DIM) * HEAD_DIM
    return pl.pallas_call(
        _attn_sample_kernel,
        grid=(batch,),
        in_specs=[_smem(), row, new, new, cache, cache,
                  _resident(tail_b.shape), _resident((1, HEAD_DIM))],
        out_specs=row,
        out_shape=jax.ShapeDtypeStruct((batch * t, ATTN_WIDTH), BF16),
        scratch_shapes=[pltpu.VMEM((2, 2 * t, width), F32)],
        compiler_params=pltpu.CompilerParams(
            dimension_semantics=("parallel",), vmem_limit_bytes=VMEM_LIMIT),
        name="attn_sample",
    )(lam, q, k_new, v_new, cache_k, cache_v, tail_b, gs)


def _pool_mix(ext, wp_ref, scale):
    outs = []
    for g, w in enumerate(POOL_WINDOWS):
        xg = ext[:, g * POOL_GROUP_DIM:(g + 1) * POOL_GROUP_DIM]
        s = xg
        span = 1
        while span < w:
            s = s + pltpu.roll(s, span, axis=0)
            span *= 2
        p = (s[N_META:] * (1.0 / w) - xg[N_META:]).astype(BF16)
        outs.append(_dot(p, wp_ref[g]))
    return jnp.concatenate(outs, axis=-1) * scale


def _mix_out(a, b_ref, x_ref, wo_ref, o_ref):
    mix = jnp.concatenate([a.astype(BF16), b_ref[...]], axis=-1)
    o_ref[...] = x_ref[...] + _dot(mix, wo_ref[...])


def _out_prompt_kernel(u_ref, halo_ref, umeta_ref, b_ref, x_ref, wp_ref, ps_ref, wo_ref, o_ref):
    first = pl.program_id(1) == 0
    halo = jnp.where(first, umeta_ref[...], halo_ref[...])
    ext = jnp.concatenate([halo, u_ref[...]], axis=0)
    _mix_out(_pool_mix(ext, wp_ref, ps_ref[...]), b_ref, x_ref, wo_ref, o_ref)


def _out_sample_kernel(u_ref, state_ref, b_ref, x_ref, wp_ref, ps_ref, wo_ref, o_ref):
    n_seq, t = state_ref.shape[0], u_ref.shape[0] // state_ref.shape[0]
    parts = []
    for s in range(n_seq):
        ext = jnp.concatenate([state_ref[s], u_ref[s * t:(s + 1) * t, :]], axis=0)
        parts.append(_pool_mix(ext, wp_ref, ps_ref[...]))
    _mix_out(jnp.concatenate(parts, axis=0), b_ref, x_ref, wo_ref, o_ref)


def _out_common_specs(rows, idx):
    return (pl.BlockSpec((rows, ATTN_WIDTH), idx), pl.BlockSpec((rows, D_MODEL), idx))


def _out_prompt(u, u_meta, b, x, wp, ps, wo, batch, seq):
    rows = PROJ_ROWS
    nt = seq // rows
    idx = lambda bi, i: (bi * nt + i, 0)
    halo_blocks = rows // N_META
    halo_idx = lambda bi, i: (jnp.maximum((bi * nt + i) * halo_blocks - 1, 0), 0)
    b_spec, x_spec = _out_common_specs(rows, idx)
    return pl.pallas_call(
        _out_prompt_kernel,
        grid=(batch, nt),
        in_specs=[
            pl.BlockSpec((rows, POOL_WIDTH), idx),
            pl.BlockSpec((N_META, POOL_WIDTH), halo_idx),
            _resident(u_meta.shape), b_spec, x_spec,
            _resident(wp.shape), _resident(ps.shape), _resident(wo.shape),
        ],
        out_specs=x_spec,
        out_shape=jax.ShapeDtypeStruct(x.shape, F32),
        compiler_params=pltpu.CompilerParams(
            dimension_semantics=("parallel", "parallel"), vmem_limit_bytes=VMEM_LIMIT),
        name="out_proj_prompt",
    )(u, u, u_meta, b, x, wp, ps, wo)


def _out_sample(u, state16, b, x, wp, ps, wo, t):
    rows = PROJ_ROWS
    n_seq = rows // t
    idx = lambda i: (i, 0)
    b_spec, x_spec = _out_common_specs(rows, idx)
    return pl.pallas_call(
        _out_sample_kernel,
        grid=(x.shape[0] // rows,),
        in_specs=[
            pl.BlockSpec((rows, POOL_WIDTH), idx),
            pl.BlockSpec((n_seq, N_META, POOL_WIDTH), lambda i: (i, 0, 0)),
            b_spec, x_spec,
            _resident(wp.shape), _resident(ps.shape), _resident(wo.shape),
        ],
        out_specs=x_spec,
        out_shape=jax.ShapeDtypeStruct(x.shape, F32),
        compiler_params=pltpu.CompilerParams(
            dimension_semantics=("parallel",), vmem_limit_bytes=VMEM_LIMIT),
        name="out_proj_sample",
    )(u, state16, b, x, wp, ps, wo)


def _ffn_kernel(h_ref, gf_ref, wg_ref, wu_ref, wd_ref, gfin_ref, y_ref, n_ref):
    f = pl.program_id(1)

    @pl.when(f == 0)
    def _():
        h = h_ref[...]
        n_ref[...] = _rms(h, gf_ref[...], EPS).astype(BF16)
        y_ref[...] = h

    n = n_ref[...]
    g = _dot(n, wg_ref[...])
    u = _dot(n, wu_ref[...])
    act = (g * (1.0 / (1.0 + jnp.exp(-g))) * u).astype(BF16)
    y_ref[...] += _dot(act, wd_ref[...])

    @pl.when(f == pl.num_programs(1) - 1)
    def _():
        y_ref[...] = _rms(y_ref[...], gfin_ref[...], EPS)


def _ffn(h, gf, wg, wu, wd, gfin):
    rows, cols = FFN_ROWS, FFN_COLS
    row = pl.BlockSpec((rows, D_MODEL), lambda i, f: (i, 0))
    vec = pl.BlockSpec((1, D_MODEL), lambda i, f: (0, 0))
    return pl.pallas_call(
        _ffn_kernel,
        grid=(h.shape[0] // rows, D_FF // cols),
        in_specs=[
            row, vec,
            pl.BlockSpec((D_MODEL, cols), lambda i, f: (0, f)),
            pl.BlockSpec((D_MODEL, cols), lambda i, f: (0, f)),
            pl.BlockSpec((cols, D_MODEL), lambda i, f: (f, 0)),
            vec,
        ],
        out_specs=row,
        out_shape=jax.ShapeDtypeStruct(h.shape, F32),
        scratch_shapes=[pltpu.VMEM((rows, D_MODEL), BF16)],
        compiler_params=pltpu.CompilerParams(
            dimension_semantics=("parallel", "arbitrary"), vmem_limit_bytes=VMEM_LIMIT),
        name="swiglu",
    )(h, gf, wg, wu, wd, gfin)


def kernel(x_prompt, x_sample, cache_k, cache_v, state_pool, meta, g_mix, w_in, w_pool,
           pool_scale, lambda_q1, lambda_k1, lambda_q2, lambda_k2, g_subln, w_out, g_ffn,
           w_gate, w_up, w_down, rel_bias, g_final):
    batch, seq, d = x_prompt.shape
    dec_batch, t, _ = x_sample.shape
    depth = w_in.shape[0]
    assert depth == 1 and d == D_MODEL and meta.shape == (N_META, D_MODEL)
    assert seq % PROJ_ROWS == 0 and seq % Q_TILE == 0 and PROJ_ROWS % t == 0
    assert (dec_batch * t) % PROJ_ROWS == 0 and t >= POOL_STATE
    n_cache = cache_k.shape[2]
    assert (n_cache - N_META) % CHUNK == 0 and n_cache > SAMPLE_TAIL

    w_in_b = w_in[0].astype(BF16)
    w_pool_b = w_pool[0].astype(BF16)
    w_out_b = w_out[0].astype(BF16)
    w_gate_b = w_gate[0].astype(BF16)
    w_up_b = w_up[0].astype(BF16)
    w_down_b = w_down[0].astype(BF16)
    g_mix2 = g_mix[0].reshape(1, D_MODEL)
    g_ffn2 = g_ffn[0].reshape(1, D_MODEL)
    g_fin2 = g_final.reshape(1, D_MODEL)
    g_sub2 = g_subln[0].reshape(1, HEAD_DIM)
    ps2 = pool_scale[0].reshape(1, POOL_WIDTH)

    diag_b, meta_b, tail_b, lam = _bias_tables(
        rel_bias, lambda_q1, lambda_k1, lambda_q2, lambda_k2, t)

    xp = x_prompt.reshape(batch * seq, D_MODEL)
    xs = x_sample.reshape(dec_batch * t, D_MODEL)
    u_m, _, k_all, v_all = _project_meta(meta, g_mix2, w_in_b, batch, N_META + seq)
    u_p, q_p, k_all, v_all = _project_prompt(xp, g_mix2, w_in_b, k_all, v_all, PROJ_ROWS, batch, seq)
    u_s, q_s, k_s, v_s = _project_rows(xs, g_mix2, w_in_b, PROJ_ROWS)

    b_p = _attn_prompt(q_p, k_all, v_all, diag_b, meta_b, lam, g_sub2, batch, seq)
    b_s = _attn_sample(q_s, k_s, v_s,
                       cache_k[0].reshape(dec_batch, n_cache * HEADS, HEAD_DIM),
                       cache_v[0].reshape(dec_batch, n_cache * HEADS, HEAD_DIM),
                       tail_b, lam, g_sub2, dec_batch, t)

    h_p = _out_prompt(u_p, u_m, b_p, xp, w_pool_b, ps2, w_out_b, batch, seq)
    state16 = jnp.pad(state_pool[0], ((0, 0), (N_META - POOL_STATE, 0), (0, 0)))
    h_s = _out_sample(u_s, state16, b_s, xs, w_pool_b, ps2, w_out_b, t)

    y_p = _ffn(h_p, g_ffn2, w_gate_b, w_up_b, w_down_b, g_fin2)
    y_s = _ffn(h_s, g_ffn2, w_gate_b, w_up_b, w_down_b, g_fin2)

    hd = (HEADS, HEAD_DIM)
    return (
        y_p.reshape(batch, seq, D_MODEL),
        y_s.reshape(dec_batch, t, D_MODEL),
        k_all.reshape(1, batch, N_META + seq, *hd),
        v_all.reshape(1, batch, N_META + seq, *hd),
        u_p.reshape(batch, seq, POOL_WIDTH)[None, :, seq - POOL_STATE:],
        k_s.reshape(1, dec_batch, t, *hd),
        v_s.reshape(1, dec_batch, t, *hd),
        u_s.reshape(dec_batch, t, POOL_WIDTH)[None, :, t - POOL_STATE:],
    )
```

```python
import functools
import math

import jax
import jax.numpy as jnp
from jax import lax
from jax.experimental import pallas as pl
from jax.experimental.pallas import tpu as pltpu

F32 = jnp.float32
BF16 = jnp.bfloat16

D_MODEL = 2048
CHUNK = 64
N_META = 16
POOL_WINDOWS = (2, 4, 8, 16)
POOL_GROUP_DIM = 256
POOL_WIDTH = 1024
POOL_STATE = 15
HEADS = 8
HEAD_DIM = 128
HALF_DIM = 64
ATTN_WIDTH = 1024
ATT_SCALE = HALF_DIM ** -0.5
LOG2E = math.log2(math.e)
D_FF = 5632
EPS = 1e-6
SUBLN_EPS = 1e-5
NEG_INF = -1e30
LAMBDA_INIT = 0.8 - 0.6 * math.exp(-0.3 * 0)

_BUCKET_THRESHOLDS = (1, 2, 3, 4, 5, 6, 7, 8, 12, 16, 23, 32, 46, 64, 91)

Q_TILE = 256
SOFTMAX_ROWS = 32
PROJ_ROWS = 512
FFN_ROWS = 1024
FFN_COLS = 256
SAMPLE_TAIL = 272
VMEM_LIMIT = 56 * 1024 * 1024


def _rms(x, g, eps):
    ms = jnp.mean(x * x, axis=-1, keepdims=True)
    return x * lax.rsqrt(ms + eps) * g


def _dot(a, b):
    return jnp.dot(a, b, preferred_element_type=F32)


def _dot_nt(a, b):
    return lax.dot_general(a, b, (((1,), (1,)), ((), ())), preferred_element_type=F32)


def _resident(shape):
    nd = len(shape)
    return pl.BlockSpec(shape, lambda *_: (0,) * nd, pipeline_mode=pl.Buffered(1))


def _smem():
    return pl.BlockSpec(memory_space=pltpu.SMEM)


def _bias_of_rel(rel, table_ref, h):
    n = jnp.abs(rel)
    far = table_ref[len(_BUCKET_THRESHOLDS), h]

    def entry(b):
        return (table_ref[b, h] - far) * LOG2E

    neg = jnp.full(rel.shape, entry(0), F32)
    pos = jnp.full(rel.shape, entry(16), F32)
    for b, thr in enumerate(_BUCKET_THRESHOLDS, start=1):
        ge = n >= thr
        neg = jnp.where(ge, entry(b), neg)
        pos = jnp.where(ge, entry(16 + b), pos)
    return jnp.where(rel > 0, pos, neg)


def _bias_kernel(table_ref, lq1_ref, lk1_ref, lq2_ref, lk2_ref,
                 diag_ref, meta_ref, tail_ref, lam_ref):
    h = pl.program_id(0)
    t = Q_TILE
    qi = lax.broadcasted_iota(jnp.int32, (t, t), 0)
    kj = lax.broadcasted_iota(jnp.int32, (t, t), 1)
    visible = (kj // CHUNK) <= (qi // CHUNK)
    diag_ref[0, 0] = jnp.where(visible, _bias_of_rel(kj - qi, table_ref, h), NEG_INF)
    diag_ref[0, 1] = _bias_of_rel(kj - t - qi, table_ref, h)
    qm = lax.broadcasted_iota(jnp.int32, (t, N_META), 0)
    km = lax.broadcasted_iota(jnp.int32, (t, N_META), 1)
    meta_ref[0] = _bias_of_rel(km - N_META - qm, table_ref, h)
    qs = lax.broadcasted_iota(jnp.int32, tail_ref.shape[1:], 0)
    ks = lax.broadcasted_iota(jnp.int32, tail_ref.shape[1:], 1)
    tail_ref[0] = _bias_of_rel(ks - SAMPLE_TAIL - qs, table_ref, h)
    s1 = jnp.sum(lq1_ref[...] * lk1_ref[...], axis=-1, keepdims=True)
    s2 = jnp.sum(lq2_ref[...] * lk2_ref[...], axis=-1, keepdims=True)
    lam_ref[...] = jnp.exp(s1) - jnp.exp(s2) + LAMBDA_INIT


def _bias_tables(rel_bias, lq1, lk1, lq2, lk2, n_new):
    tail_w = SAMPLE_TAIL + n_new
    vec = pl.BlockSpec((1, HALF_DIM), lambda h: (0, 0))
    return pl.pallas_call(
        _bias_kernel,
        grid=(HEADS,),
        in_specs=[_smem(), vec, vec, vec, vec],
        out_specs=[
            pl.BlockSpec((1, 2, Q_TILE, Q_TILE), lambda h: (h, 0, 0, 0)),
            pl.BlockSpec((1, Q_TILE, N_META), lambda h: (h, 0, 0)),
            pl.BlockSpec((1, n_new, tail_w), lambda h: (h, 0, 0)),
            pl.BlockSpec((1, 1), lambda h: (0, 0)),
        ],
        out_shape=[
            jax.ShapeDtypeStruct((HEADS, 2, Q_TILE, Q_TILE), F32),
            jax.ShapeDtypeStruct((HEADS, Q_TILE, N_META), F32),
            jax.ShapeDtypeStruct((HEADS, n_new, tail_w), F32),
            jax.ShapeDtypeStruct((1, 1), F32),
        ],
        name="bias_tables",
    )(rel_bias, lq1, lk1, lq2, lk2)


def _store_heads(o_ref, x):
    rows = x.shape[0]
    for h in range(HEADS):
        o_ref[pl.ds(h, rows, stride=HEADS), :] = x[:, h * HEAD_DIM:(h + 1) * HEAD_DIM]


def _proj_kernel(x_ref, g_ref, w_ref, *rest):
    u_ref, q_ref, k_ref, v_ref = rest[-4:]
    xn = _rms(x_ref[...], g_ref[...], EPS).astype(BF16)
    w = POOL_WIDTH
    u_ref[...] = _dot(xn, w_ref[:, 0 * w:1 * w])
    q_ref[...] = (_dot(xn, w_ref[:, 1 * w:2 * w]) * (ATT_SCALE * LOG2E)).astype(BF16)
    _store_heads(k_ref, _dot(xn, w_ref[:, 2 * w:3 * w]))
    _store_heads(v_ref, _dot(xn, w_ref[:, 3 * w:4 * w]))


def _proj_call(grid, x_spec, uq_spec, kv_spec, n, kv_shape, w_bf16, extra_specs=(), aliases=None):
    kv = jax.ShapeDtypeStruct(kv_shape, F32)
    return pl.pallas_call(
        _proj_kernel,
        grid=grid,
        in_specs=[x_spec, _resident((1, D_MODEL)), _resident(w_bf16.shape), *extra_specs],
        out_specs=[uq_spec, uq_spec, kv_spec, kv_spec],
        out_shape=[jax.ShapeDtypeStruct((n, POOL_WIDTH), F32),
                   jax.ShapeDtypeStruct((n, POOL_WIDTH), BF16), kv, kv],
        input_output_aliases=aliases or {},
        compiler_params=pltpu.CompilerParams(
            dimension_semantics=("arbitrary",) * len(grid), vmem_limit_bytes=VMEM_LIMIT),
        name="in_proj",
    )


def _project_rows(x, g, w_bf16, rows):
    n = x.shape[0]
    return _proj_call(
        (n // rows,),
        pl.BlockSpec((rows, D_MODEL), lambda i: (i, 0)),
        pl.BlockSpec((rows, POOL_WIDTH), lambda i: (i, 0)),
        pl.BlockSpec((rows * HEADS, HEAD_DIM), lambda i: (i, 0)),
        n, (n * HEADS, HEAD_DIM), w_bf16)(x, g, w_bf16)


def _project_meta(meta, g, w_bf16, batch, seq_rows):
    same = lambda b: (0, 0)
    return _proj_call(
        (batch,),
        pl.BlockSpec((N_META, D_MODEL), same),
        pl.BlockSpec((N_META, POOL_WIDTH), same),
        pl.BlockSpec((None, N_META * HEADS, HEAD_DIM), lambda b: (b, 0, 0)),
        N_META, (batch, seq_rows * HEADS, HEAD_DIM), w_bf16)(meta, g, w_bf16)


def _project_prompt(x, g, w_bf16, k_buf, v_buf, rows, batch, seq):
    nt = seq // rows
    row_idx = lambda b, i: (b * nt + i, 0)
    any_spec = pl.BlockSpec(memory_space=pl.ANY)
    kv_spec = pl.BlockSpec(
        (pl.Squeezed(), pl.Element(rows * HEADS), pl.Element(HEAD_DIM)),
        lambda b, i: (b, (N_META + i * rows) * HEADS, 0))
    return _proj_call(
        (batch, nt),
        pl.BlockSpec((rows, D_MODEL), row_idx),
        pl.BlockSpec((rows, POOL_WIDTH), row_idx),
        kv_spec, batch * seq, k_buf.shape, w_bf16,
        extra_specs=(any_spec, any_spec), aliases={3: 2, 4: 3})(x, g, w_bf16, k_buf, v_buf)


def _stack_maps(q):
    lane = lax.broadcasted_iota(jnp.int32, q.shape, 1)
    zero = jnp.zeros_like(q)
    return jnp.concatenate(
        [jnp.where(lane < HALF_DIM, q, zero), jnp.where(lane >= HALF_DIM, q, zero)], axis=0)


def _finish_head(o, gs):
    return (_rms(o, gs, SUBLN_EPS) * (1.0 - LAMBDA_INIT)).astype(BF16)


def _attn_prompt_kernel(lam_ref, q_ref, k_ref, v_ref, diag_ref, meta_ref, gs_ref,
                        o_ref, kb_ref, vb_ref, q2_ref, s_ref, sm_ref, w_ref, wm_ref):
    tq = Q_TILE
    nt = q_ref.shape[0] // tq
    rb = SOFTMAX_ROWS
    lam = lam_ref[0, 0]
    head_rows = pl.ds(pl.program_id(1), kb_ref.shape[0], stride=HEADS)
    kb_ref[...] = k_ref[0, head_rows, :].astype(BF16)
    vb_ref[...] = v_ref[0, head_rows, :].astype(BF16)

    def score_steps(c):
        slot = c % 2

        def stack():
            q2_ref[slot] = _stack_maps(q_ref[c * tq:(c + 1) * tq, :])

        def frames(j):
            d = _dot_nt(q2_ref[slot], kb_ref[N_META + j * tq:N_META + (j + 1) * tq, :])
            cols = slice(j * tq, (j + 1) * tq)
            if j >= c - 1:
                bias = diag_ref[0, c - j]
                s_ref[slot, :tq, cols] = d[:tq] + bias
                s_ref[slot, tq:, cols] = d[tq:] + bias
            else:
                s_ref[slot, :, cols] = d

        def metas():
            sm = _dot_nt(q2_ref[slot], kb_ref[0:N_META, :])
            if c == 0:
                sm = sm + jnp.concatenate([meta_ref[0], meta_ref[0]], axis=0)
            sm_ref[slot] = sm

        return [stack, metas] + [functools.partial(frames, j) for j in range(c + 1)]

    def softmax_step(c, i):
        slot = c % 2
        n = (c + 1) * tq
        rows = (slice(i * rb, (i + 1) * rb), slice(tq + i * rb, tq + (i + 1) * rb))
        p, pm, r = [], [], []
        for rmap in rows:
            s = s_ref[slot, rmap, :n]
            sm = sm_ref[slot, rmap, :]
            m = jnp.maximum(jnp.max(s, axis=-1, keepdims=True), jnp.max(sm, axis=-1, keepdims=True))
            p.append(jnp.exp2(s - m))
            pm.append(jnp.exp2(sm - m))
            r.append(1.0 / (jnp.sum(p[-1], axis=-1, keepdims=True) + jnp.sum(pm[-1], axis=-1, keepdims=True)))
        r1, r2 = r[0], lam * r[1]
        w_ref[slot, rows[0], :n] = (p[0] * r1 - p[1] * r2).astype(BF16)
        wm_ref[slot, rows[0], :] = (pm[0] * r1 - pm[1] * r2).astype(BF16)

    def value_step(c):
        slot = c % 2
        n = (c + 1) * tq
        o = _dot(w_ref[slot, :, :n], vb_ref[N_META:N_META + n, :]) + _dot(wm_ref[slot], vb_ref[0:N_META, :])
        o_ref[c * tq:(c + 1) * tq, :] = _finish_head(o, gs_ref[...])

    for step in score_steps(0):
        step()
    for c in range(nt):
        ahead = score_steps(c + 1) if c + 1 < nt else []
        blocks = tq // rb
        for i in range(blocks):
            softmax_step(c, i)
            for step in ahead[i * len(ahead) // blocks:(i + 1) * len(ahead) // blocks]:
                step()
        value_step(c)


def _attn_prompt(q, k_all, v_all, diag, meta_b, lam, gs, batch, seq):
    tq = Q_TILE
    kv = pl.BlockSpec((1, (N_META + seq) * HEADS, HEAD_DIM), lambda b, h: (b, 0, 0))
    qo = pl.BlockSpec((seq, HEAD_DIM), lambda b, h: (b, h))
    return pl.pallas_call(
        _attn_prompt_kernel,
        grid=(batch, HEADS),
        in_specs=[
            _smem(), qo, kv, kv,
            pl.BlockSpec((1, 2, tq, tq), lambda b, h: (h, 0, 0, 0)),
            pl.BlockSpec((1, tq, N_META), lambda b, h: (h, 0, 0)),
            pl.BlockSpec((1, HEAD_DIM), lambda b, h: (0, 0)),
        ],
        out_specs=qo,
        out_shape=jax.ShapeDtypeStruct((batch * seq, ATTN_WIDTH), BF16),
        scratch_shapes=[
            pltpu.VMEM((N_META + seq, HEAD_DIM), BF16),
            pltpu.VMEM((N_META + seq, HEAD_DIM), BF16),
            pltpu.VMEM((2, 2 * tq, HEAD_DIM), BF16),
            pltpu.VMEM((2, 2 * tq, seq), F32),
            pltpu.VMEM((2, 2 * tq, N_META), F32),
            pltpu.VMEM((2, tq, seq), BF16),
            pltpu.VMEM((2, tq, N_META), BF16),
        ],
        compiler_params=pltpu.CompilerParams(
            dimension_semantics=("parallel", "parallel"), vmem_limit_bytes=VMEM_LIMIT),
        name="attn_prompt",
    )(lam, q, k_all, v_all, diag, meta_b, gs)


def _attn_sample_kernel(lam_ref, q_ref, kn_ref, vn_ref, ck_ref, cv_ref, tail_ref, gs_ref,
                        o_ref, s_ref):
    t = q_ref.shape[0]
    n_cache = ck_ref.shape[1] // HEADS
    split = n_cache - SAMPLE_TAIL
    width = n_cache + t
    lam = lam_ref[0, 0]
    for h in range(HEADS):
        slot = h % 2
        cs = slice(h * HEAD_DIM, (h + 1) * HEAD_DIM)
        q2 = _stack_maps(q_ref[:, cs])
        far = pl.ds(h, split, stride=HEADS)
        near = pl.ds(split * HEADS + h, SAMPLE_TAIL, stride=HEADS)
        new = pl.ds(h, t, stride=HEADS)
        kt = jnp.concatenate([ck_ref[0, near, :].astype(BF16), kn_ref[new, :].astype(BF16)], axis=0)
        vt = jnp.concatenate([cv_ref[0, near, :].astype(BF16), vn_ref[new, :].astype(BF16)], axis=0)
        bt = tail_ref[h]
        s_ref[slot, :, 0:split] = _dot_nt(q2, ck_ref[0, far, :].astype(BF16))
        s_ref[slot, :, split:width] = _dot_nt(q2, kt) + jnp.concatenate([bt, bt], axis=0)
        m = jnp.max(s_ref[slot, :, 0:width], axis=-1, keepdims=True)
        p = jnp.exp2(s_ref[slot, :, 0:width] - m)
        r = 1.0 / jnp.sum(p, axis=-1, keepdims=True)
        s_ref[slot, :, 0:width] = p
        r1 = r[:t]
        r2 = lam * r[t:]
        w = (s_ref[slot, :t, 0:width] * r1 - s_ref[slot, t:, 0:width] * r2).astype(BF16)
        o = _dot(w[:, 0:split], cv_ref[0, far, :].astype(BF16)) + _dot(w[:, split:width], vt)
        o_ref[:, cs] = _finish_head(o, gs_ref[...])


def _attn_sample(q, k_new, v_new, cache_k, cache_v, tail_b, lam, gs, batch, t):
    n_cache = cache_k.shape[1] // HEADS
    row = pl.BlockSpec((t, ATTN_WIDTH), lambda b: (b, 0))
    new = pl.BlockSpec((t * HEADS, HEAD_DIM), lambda b: (b, 0))
    cache = pl.BlockSpec((1,) + cache_k.shape[1:], lambda b: (b, 0, 0))
    width = -(-(n_cache + t) // HEAD_DIM) * HEAD_DIM
    return pl.pallas_call(
        _attn_sample_kernel,
        grid=(batch,),
        in_specs=[_smem(), row, new, new, cache, cache,
                  _resident(tail_b.shape), _resident((1, HEAD_DIM))],
        out_specs=row,
        out_shape=jax.ShapeDtypeStruct((batch * t, ATTN_WIDTH), BF16),
        scratch_shapes=[pltpu.VMEM((2, 2 * t, width), F32)],
        compiler_params=pltpu.CompilerParams(
            dimension_semantics=("parallel",), vmem_limit_bytes=VMEM_LIMIT),
        name="attn_sample",
    )(lam, q, k_new, v_new, cache_k, cache_v, tail_b, gs)


def _pool_mix(ext, wp_ref, scale):
    outs = []
    for g, w in enumerate(POOL_WINDOWS):
        xg = ext[:, g * POOL_GROUP_DIM:(g + 1) * POOL_GROUP_DIM]
        s = xg
        span = 1
        while span < w:
            s = s + pltpu.roll(s, span, axis=0)
            span *= 2
        p = (s[N_META:] * (1.0 / w) - xg[N_META:]).astype(BF16)
        outs.append(_dot(p, wp_ref[g]))
    return jnp.concatenate(outs, axis=-1) * scale


def _mix_out(a, b_ref, x_ref, wo_ref, o_ref):
    mix = jnp.concatenate([a.astype(BF16), b_ref[...]], axis=-1)
    o_ref[...] = x_ref[...] + _dot(mix, wo_ref[...])


def _out_prompt_kernel(u_ref, halo_ref, umeta_ref, b_ref, x_ref, wp_ref, ps_ref, wo_ref, o_ref):
    first = pl.program_id(1) == 0
    halo = jnp.where(first, umeta_ref[...], halo_ref[...])
    ext = jnp.concatenate([halo, u_ref[...]], axis=0)
    _mix_out(_pool_mix(ext, wp_ref, ps_ref[...]), b_ref, x_ref, wo_ref, o_ref)


def _out_sample_kernel(u_ref, state_ref, b_ref, x_ref, wp_ref, ps_ref, wo_ref, o_ref):
    n_seq, t = state_ref.shape[0], u_ref.shape[0] // state_ref.shape[0]
    parts = []
    for s in range(n_seq):
        ext = jnp.concatenate([state_ref[s], u_ref[s * t:(s + 1) * t, :]], axis=0)
        parts.append(_pool_mix(ext, wp_ref, ps_ref[...]))
    _mix_out(jnp.concatenate(parts, axis=0), b_ref, x_ref, wo_ref, o_ref)


def _out_common_specs(rows, idx):
    return (pl.BlockSpec((rows, ATTN_WIDTH), idx), pl.BlockSpec((rows, D_MODEL), idx))


def _out_prompt(u, u_meta, b, x, wp, ps, wo, batch, seq):
    rows = PROJ_ROWS
    nt = seq // rows
    idx = lambda bi, i: (bi * nt + i, 0)
    halo_blocks = rows // N_META
    halo_idx = lambda bi, i: (jnp.maximum((bi * nt + i) * halo_blocks - 1, 0), 0)
    b_spec, x_spec = _out_common_specs(rows, idx)
    return pl.pallas_call(
        _out_prompt_kernel,
        grid=(batch, nt),
        in_specs=[
            pl.BlockSpec((rows, POOL_WIDTH), idx),
            pl.BlockSpec((N_META, POOL_WIDTH), halo_idx),
            _resident(u_meta.shape), b_spec, x_spec,
            _resident(wp.shape), _resident(ps.shape), _resident(wo.shape),
        ],
        out_specs=x_spec,
        out_shape=jax.ShapeDtypeStruct(x.shape, F32),
        compiler_params=pltpu.CompilerParams(
            dimension_semantics=("parallel", "parallel"), vmem_limit_bytes=VMEM_LIMIT),
        name="out_proj_prompt",
    )(u, u, u_meta, b, x, wp, ps, wo)


def _out_sample(u, state16, b, x, wp, ps, wo, t):
    rows = PROJ_ROWS
    n_seq = rows // t
    idx = lambda i: (i, 0)
    b_spec, x_spec = _out_common_specs(rows, idx)
    return pl.pallas_call(
        _out_sample_kernel,
        grid=(x.shape[0] // rows,),
        in_specs=[
            pl.BlockSpec((rows, POOL_WIDTH), idx),
            pl.BlockSpec((n_seq, N_META, POOL_WIDTH), lambda i: (i, 0, 0)),
            b_spec, x_spec,
            _resident(wp.shape), _resident(ps.shape), _resident(wo.shape),
        ],
        out_specs=x_spec,
        out_shape=jax.ShapeDtypeStruct(x.shape, F32),
        compiler_params=pltpu.CompilerParams(
            dimension_semantics=("parallel",), vmem_limit_bytes=VMEM_LIMIT),
        name="out_proj_sample",
    )(u, state16, b, x, wp, ps, wo)


def _ffn_kernel(h_ref, gf_ref, wg_ref, wu_ref, wd_ref, gfin_ref, y_ref, n_ref):
    f = pl.program_id(1)

    @pl.when(f == 0)
    def _():
        h = h_ref[...]
        n_ref[...] = _rms(h, gf_ref[...], EPS).astype(BF16)
        y_ref[...] = h

    n = n_ref[...]
    g = _dot(n, wg_ref[...].astype(BF16))
    u = _dot(n, wu_ref[...].astype(BF16))
    act = (g * (1.0 / (1.0 + jnp.exp(-g))) * u).astype(BF16)
    y_ref[...] += _dot(act, wd_ref[...].astype(BF16))

    @pl.when(f == pl.num_programs(1) - 1)
    def _():
        y_ref[...] = _rms(y_ref[...], gfin_ref[...], EPS)


def _ffn(h, gf, wg, wu, wd, gfin):
    rows, cols = FFN_ROWS, FFN_COLS
    row = pl.BlockSpec((rows, D_MODEL), lambda i, f: (i, 0))
    vec = pl.BlockSpec((1, D_MODEL), lambda i, f: (0, 0))
    return pl.pallas_call(
        _ffn_kernel,
        grid=(h.shape[0] // rows, D_FF // cols),
        in_specs=[
            row, vec,
            pl.BlockSpec((D_MODEL, cols), lambda i, f: (0, f)),
            pl.BlockSpec((D_MODEL, cols), lambda i, f: (0, f)),
            pl.BlockSpec((cols, D_MODEL), lambda i, f: (f, 0)),
            vec,
        ],
        out_specs=row,
        out_shape=jax.ShapeDtypeStruct(h.shape, F32),
        scratch_shapes=[pltpu.VMEM((rows, D_MODEL), BF16)],
        compiler_params=pltpu.CompilerParams(
            dimension_semantics=("parallel", "arbitrary"), vmem_limit_bytes=VMEM_LIMIT),
        name="swiglu",
    )(h, gf, wg, wu, wd, gfin)


def kernel(x_prompt, x_sample, cache_k, cache_v, state_pool, meta, g_mix, w_in, w_pool,
           pool_scale, lambda_q1, lambda_k1, lambda_q2, lambda_k2, g_subln, w_out, g_ffn,
           w_gate, w_up, w_down, rel_bias, g_final):
    batch, seq, d = x_prompt.shape
    dec_batch, t, _ = x_sample.shape
    depth = w_in.shape[0]
    assert depth == 1 and d == D_MODEL and meta.shape == (N_META, D_MODEL)
    assert seq % PROJ_ROWS == 0 and seq % Q_TILE == 0 and PROJ_ROWS % t == 0
    assert (dec_batch * t) % PROJ_ROWS == 0 and t >= POOL_STATE
    n_cache = cache_k.shape[2]
    assert (n_cache - N_META) % CHUNK == 0 and n_cache > SAMPLE_TAIL

    w_in_b = w_in[0].astype(BF16)
    w_pool_b = w_pool[0].astype(BF16)
    w_out_b = w_out[0].astype(BF16)
    g_mix2 = g_mix[0].reshape(1, D_MODEL)
    g_ffn2 = g_ffn[0].reshape(1, D_MODEL)
    g_fin2 = g_final.reshape(1, D_MODEL)
    g_sub2 = g_subln[0].reshape(1, HEAD_DIM)
    ps2 = pool_scale[0].reshape(1, POOL_WIDTH)

    diag_b, meta_b, tail_b, lam = _bias_tables(
        rel_bias, lambda_q1, lambda_k1, lambda_q2, lambda_k2, t)

    xp = x_prompt.reshape(batch * seq, D_MODEL)
    xs = x_sample.reshape(dec_batch * t, D_MODEL)
    u_m, _, k_all, v_all = _project_meta(meta, g_mix2, w_in_b, batch, N_META + seq)
    u_p, q_p, k_all, v_all = _project_prompt(xp, g_mix2, w_in_b, k_all, v_all, PROJ_ROWS, batch, seq)
    u_s, q_s, k_s, v_s = _project_rows(xs, g_mix2, w_in_b, PROJ_ROWS)

    b_p = _attn_prompt(q_p, k_all, v_all, diag_b, meta_b, lam, g_sub2, batch, seq)
    b_s = _attn_sample(q_s, k_s, v_s,
                       cache_k[0].reshape(dec_batch, n_cache * HEADS, HEAD_DIM),
                       cache_v[0].reshape(dec_batch, n_cache * HEADS, HEAD_DIM),
                       tail_b, lam, g_sub2, dec_batch, t)

    h_p = _out_prompt(u_p, u_m, b_p, xp, w_pool_b, ps2, w_out_b, batch, seq)
    state16 = jnp.pad(state_pool[0], ((0, 0), (N_META - POOL_STATE, 0), (0, 0)))
    h_s = _out_sample(u_s, state16, b_s, xs, w_pool_b, ps2, w_out_b, t)

    y_p = _ffn(h_p, g_ffn2, w_gate[0], w_up[0], w_down[0], g_fin2)
    y_s = _ffn(h_s, g_ffn2, w_gate[0], w_up[0], w_down[0], g_fin2)

    hd = (HEADS, HEAD_DIM)
    return (
        y_p.reshape(batch, seq, D_MODEL),
        y_s.reshape(dec_batch, t, D_MODEL),
        k_all.reshape(1, batch, N_META + seq, *hd),
        v_all.reshape(1, batch, N_META + seq, *hd),
        u_p.reshape(batch, seq, POOL_WIDTH)[None, :, seq - POOL_STATE:],
        k_s.reshape(1, dec_batch, t, *hd),
        v_s.reshape(1, dec_batch, t, *hd),
        u_s.reshape(dec_batch, t, POOL_WIDTH)[None, :, t - POOL_STATE:],
    )
```

```python
import functools
import math

import jax
import jax.numpy as jnp
from jax import lax
from jax.experimental import pallas as pl
from jax.experimental.pallas import tpu as pltpu

F32 = jnp.float32
BF16 = jnp.bfloat16

D_MODEL = 2048
CHUNK = 64
N_META = 16
POOL_WINDOWS = (2, 4, 8, 16)
POOL_GROUP_DIM = 256
POOL_WIDTH = 1024
POOL_STATE = 15
HEADS = 8
HEAD_DIM = 128
HALF_DIM = 64
ATTN_WIDTH = 1024
ATT_SCALE = HALF_DIM ** -0.5
LOG2E = math.log2(math.e)
D_FF = 5632
EPS = 1e-6
SUBLN_EPS = 1e-5
NEG_INF = -1e30
LAMBDA_INIT = 0.8 - 0.6 * math.exp(-0.3 * 0)

_BUCKET_THRESHOLDS = (1, 2, 3, 4, 5, 6, 7, 8, 12, 16, 23, 32, 46, 64, 91)

Q_TILE = 256
SOFTMAX_ROWS = 32
PROJ_ROWS = 512
PROMPT_TILES = 3
FFN_ROWS = 1024
FFN_COLS = 256
SAMPLE_TAIL = 272
VMEM_LIMIT = 56 * 1024 * 1024


def _rms(x, g, eps):
    ms = jnp.mean(x * x, axis=-1, keepdims=True)
    return x * lax.rsqrt(ms + eps) * g


def _dot(a, b):
    return jnp.dot(a, b, preferred_element_type=F32)


def _dot_nt(a, b):
    return lax.dot_general(a, b, (((1,), (1,)), ((), ())), preferred_element_type=F32)


def _resident(shape):
    nd = len(shape)
    return pl.BlockSpec(shape, lambda *_: (0,) * nd, pipeline_mode=pl.Buffered(1))


def _smem():
    return pl.BlockSpec(memory_space=pltpu.SMEM)


def _bias_of_rel(rel, table_ref, h):
    n = jnp.abs(rel)
    far = table_ref[len(_BUCKET_THRESHOLDS), h]

    def entry(b):
        return (table_ref[b, h] - far) * LOG2E

    neg = jnp.full(rel.shape, entry(0), F32)
    pos = jnp.full(rel.shape, entry(16), F32)
    for b, thr in enumerate(_BUCKET_THRESHOLDS, start=1):
        ge = n >= thr
        neg = jnp.where(ge, entry(b), neg)
        pos = jnp.where(ge, entry(16 + b), pos)
    return jnp.where(rel > 0, pos, neg)


def _bias_kernel(table_ref, lq1_ref, lk1_ref, lq2_ref, lk2_ref,
                 diag_ref, meta_ref, tail_ref, lam_ref):
    h = pl.program_id(0)
    t = Q_TILE
    qi = lax.broadcasted_iota(jnp.int32, (t, t), 0)
    kj = lax.broadcasted_iota(jnp.int32, (t, t), 1)
    visible = (kj // CHUNK) <= (qi // CHUNK)
    diag_ref[0, 0] = jnp.where(visible, _bias_of_rel(kj - qi, table_ref, h), NEG_INF)
    diag_ref[0, 1] = _bias_of_rel(kj - t - qi, table_ref, h)
    qm = lax.broadcasted_iota(jnp.int32, (t, N_META), 0)
    km = lax.broadcasted_iota(jnp.int32, (t, N_META), 1)
    meta_ref[0] = _bias_of_rel(km - N_META - qm, table_ref, h)
    qs = lax.broadcasted_iota(jnp.int32, tail_ref.shape[1:], 0)
    ks = lax.broadcasted_iota(jnp.int32, tail_ref.shape[1:], 1)
    tail_ref[0] = _bias_of_rel(ks - SAMPLE_TAIL - qs, table_ref, h)
    s1 = jnp.sum(lq1_ref[...] * lk1_ref[...], axis=-1, keepdims=True)
    s2 = jnp.sum(lq2_ref[...] * lk2_ref[...], axis=-1, keepdims=True)
    lam_ref[...] = jnp.exp(s1) - jnp.exp(s2) + LAMBDA_INIT


def _bias_tables(rel_bias, lq1, lk1, lq2, lk2, n_new):
    tail_w = SAMPLE_TAIL + n_new
    vec = pl.BlockSpec((1, HALF_DIM), lambda h: (0, 0))
    return pl.pallas_call(
        _bias_kernel,
        grid=(HEADS,),
        in_specs=[_smem(), vec, vec, vec, vec],
        out_specs=[
            pl.BlockSpec((1, 2, Q_TILE, Q_TILE), lambda h: (h, 0, 0, 0)),
            pl.BlockSpec((1, Q_TILE, N_META), lambda h: (h, 0, 0)),
            pl.BlockSpec((1, n_new, tail_w), lambda h: (h, 0, 0)),
            pl.BlockSpec((1, 1), lambda h: (0, 0)),
        ],
        out_shape=[
            jax.ShapeDtypeStruct((HEADS, 2, Q_TILE, Q_TILE), F32),
            jax.ShapeDtypeStruct((HEADS, Q_TILE, N_META), F32),
            jax.ShapeDtypeStruct((HEADS, n_new, tail_w), F32),
            jax.ShapeDtypeStruct((1, 1), F32),
        ],
        name="bias_tables",
    )(rel_bias, lq1, lk1, lq2, lk2)


def _store_heads(o_ref, x):
    rows = x.shape[0]
    for h in range(HEADS):
        o_ref[pl.ds(h, rows, stride=HEADS), :] = x[:, h * HEAD_DIM:(h + 1) * HEAD_DIM]


def _project_store(xn, w_ref, u_ref, q_ref, k_ref, v_ref):
    w = POOL_WIDTH
    u_ref[...] = _dot(xn, w_ref[:, 0 * w:1 * w])
    q_ref[...] = (_dot(xn, w_ref[:, 1 * w:2 * w]) * (ATT_SCALE * LOG2E)).astype(BF16)
    _store_heads(k_ref, _dot(xn, w_ref[:, 2 * w:3 * w]))
    _store_heads(v_ref, _dot(xn, w_ref[:, 3 * w:4 * w]))


def _proj_rows_kernel(x_ref, g_ref, w_ref, u_ref, q_ref, k_ref, v_ref):
    xn = _rms(x_ref[...], g_ref[...], EPS).astype(BF16)
    _project_store(xn, w_ref, u_ref, q_ref, k_ref, v_ref)


def _proj_prompt_kernel(x_ref, meta_ref, g_ref, w_ref, u_ref, q_ref, k_ref, v_ref, xn_ref, carry_ref):
    i = pl.program_id(1)
    g = g_ref[...]
    rows = xn_ref.shape[0]

    @pl.when(i == 0)
    def _():
        xn_ref[0:N_META, :] = _rms(meta_ref[...], g, EPS).astype(BF16)

    @pl.when(i > 0)
    def _():
        xn_ref[0:N_META, :] = carry_ref[...]

    xn_ref[N_META:rows, :] = _rms(x_ref[0:rows - N_META, :], g, EPS).astype(BF16)

    @pl.when(i < pl.num_programs(1) - 1)
    def _():
        carry_ref[...] = _rms(x_ref[rows - N_META:rows, :], g, EPS).astype(BF16)

    _project_store(xn_ref[...], w_ref, u_ref, q_ref, k_ref, v_ref)


def _proj_out_shapes(lead, n):
    return [jax.ShapeDtypeStruct((*lead, n, POOL_WIDTH), F32),
            jax.ShapeDtypeStruct((*lead, n, POOL_WIDTH), BF16),
            jax.ShapeDtypeStruct((*lead, n * HEADS, HEAD_DIM), F32),
            jax.ShapeDtypeStruct((*lead, n * HEADS, HEAD_DIM), F32)]


def _project_rows(x, g, w_bf16, rows):
    n = x.shape[0]
    uq = pl.BlockSpec((rows, POOL_WIDTH), lambda i: (i, 0))
    kv = pl.BlockSpec((rows * HEADS, HEAD_DIM), lambda i: (i, 0))
    return pl.pallas_call(
        _proj_rows_kernel,
        grid=(n // rows,),
        in_specs=[pl.BlockSpec((rows, D_MODEL), lambda i: (i, 0)),
                  _resident((1, D_MODEL)), _resident(w_bf16.shape)],
        out_specs=[uq, uq, kv, kv],
        out_shape=_proj_out_shapes((), n),
        compiler_params=pltpu.CompilerParams(
            dimension_semantics=("parallel",), vmem_limit_bytes=VMEM_LIMIT),
        name="in_proj_rows",
    )(x, g, w_bf16)


def _project_prompt(x, meta, g, w_bf16, rows):
    batch, seq, _ = x.shape
    total = N_META + seq
    x_spec = pl.BlockSpec((None, rows, D_MODEL), lambda b, i: (b, i, 0))
    uq = pl.BlockSpec((None, rows, POOL_WIDTH), lambda b, i: (b, i, 0))
    kv = pl.BlockSpec((None, rows * HEADS, HEAD_DIM), lambda b, i: (b, i, 0))
    return pl.pallas_call(
        _proj_prompt_kernel,
        grid=(batch, total // rows),
        in_specs=[x_spec, _resident(meta.shape), _resident((1, D_MODEL)), _resident(w_bf16.shape)],
        out_specs=[uq, uq, kv, kv],
        out_shape=_proj_out_shapes((batch,), total),
        scratch_shapes=[pltpu.VMEM((rows, D_MODEL), BF16), pltpu.VMEM((N_META, D_MODEL), BF16)],
        compiler_params=pltpu.CompilerParams(
            dimension_semantics=("parallel", "arbitrary"), vmem_limit_bytes=VMEM_LIMIT),
        name="in_proj_prompt",
    )(x, meta, g, w_bf16)


def _stack_maps(q):
    lane = lax.broadcasted_iota(jnp.int32, q.shape, 1)
    zero = jnp.zeros_like(q)
    return jnp.concatenate(
        [jnp.where(lane < HALF_DIM, q, zero), jnp.where(lane >= HALF_DIM, q, zero)], axis=0)


def _finish_head(o, gs):
    return (_rms(o, gs, SUBLN_EPS) * (1.0 - LAMBDA_INIT)).astype(BF16)


def _attn_prompt_kernel(lam_ref, q_ref, k_ref, v_ref, diag_ref, meta_ref, gs_ref,
                        o_ref, kb_ref, vb_ref, q2_ref, s_ref, sm_ref, w_ref, wm_ref):
    tq = Q_TILE
    nt = o_ref.shape[0] // tq
    rb = SOFTMAX_ROWS
    lam = lam_ref[0, 0]
    head_rows = pl.ds(pl.program_id(1), kb_ref.shape[0], stride=HEADS)
    kb_ref[...] = k_ref[0, head_rows, :].astype(BF16)
    vb_ref[...] = v_ref[0, head_rows, :].astype(BF16)

    def score_steps(c):
        slot = c % 2

        def stack():
            q2_ref[slot] = _stack_maps(q_ref[N_META + c * tq:N_META + (c + 1) * tq, :])

        def frames(j):
            d = _dot_nt(q2_ref[slot], kb_ref[N_META + j * tq:N_META + (j + 1) * tq, :])
            cols = slice(j * tq, (j + 1) * tq)
            if j >= c - 1:
                bias = diag_ref[0, c - j]
                s_ref[slot, :tq, cols] = d[:tq] + bias
                s_ref[slot, tq:, cols] = d[tq:] + bias
            else:
                s_ref[slot, :, cols] = d

        def metas():
            sm = _dot_nt(q2_ref[slot], kb_ref[0:N_META, :])
            if c == 0:
                sm = sm + jnp.concatenate([meta_ref[0], meta_ref[0]], axis=0)
            sm_ref[slot] = sm

        return [stack, metas] + [functools.partial(frames, j) for j in range(c + 1)]

    def softmax_step(c, i):
        slot = c % 2
        n = (c + 1) * tq
        rows = (slice(i * rb, (i + 1) * rb), slice(tq + i * rb, tq + (i + 1) * rb))
        p, pm, r = [], [], []
        for rmap in rows:
            s = s_ref[slot, rmap, :n]
            sm = sm_ref[slot, rmap, :]
            m = jnp.maximum(jnp.max(s, axis=-1, keepdims=True), jnp.max(sm, axis=-1, keepdims=True))
            p.append(jnp.exp2(s - m))
            pm.append(jnp.exp2(sm - m))
            r.append(1.0 / (jnp.sum(p[-1], axis=-1, keepdims=True) + jnp.sum(pm[-1], axis=-1, keepdims=True)))
        r1, r2 = r[0], lam * r[1]
        w_ref[slot, rows[0], :n] = (p[0] * r1 - p[1] * r2).astype(BF16)
        wm_ref[slot, rows[0], :] = (pm[0] * r1 - pm[1] * r2).astype(BF16)

    def value_step(c):
        slot = c % 2
        n = (c + 1) * tq
        o = _dot(w_ref[slot, :, :n], vb_ref[N_META:N_META + n, :]) + _dot(wm_ref[slot], vb_ref[0:N_META, :])
        o_ref[c * tq:(c + 1) * tq, :] = _finish_head(o, gs_ref[...])

    for step in score_steps(0):
        step()
    for c in range(nt):
        ahead = score_steps(c + 1) if c + 1 < nt else []
        blocks = tq // rb
        for i in range(blocks):
            softmax_step(c, i)
            for step in ahead[i * len(ahead) // blocks:(i + 1) * len(ahead) // blocks]:
                step()
        value_step(c)


def _attn_prompt(q, k_all, v_all, diag, meta_b, lam, gs, batch, seq):
    tq = Q_TILE
    kv = pl.BlockSpec((1, (N_META + seq) * HEADS, HEAD_DIM), lambda b, h: (b, 0, 0))
    qi = pl.BlockSpec((None, N_META + seq, HEAD_DIM), lambda b, h: (b, 0, h))
    qo = pl.BlockSpec((seq, HEAD_DIM), lambda b, h: (b, h))
    return pl.pallas_call(
        _attn_prompt_kernel,
        grid=(batch, HEADS),
        in_specs=[
            _smem(), qi, kv, kv,
            pl.BlockSpec((1, 2, tq, tq), lambda b, h: (h, 0, 0, 0)),
            pl.BlockSpec((1, tq, N_META), lambda b, h: (h, 0, 0)),
            pl.BlockSpec((1, HEAD_DIM), lambda b, h: (0, 0)),
        ],
        out_specs=qo,
        out_shape=jax.ShapeDtypeStruct((batch * seq, ATTN_WIDTH), BF16),
        scratch_shapes=[
            pltpu.VMEM((N_META + seq, HEAD_DIM), BF16),
            pltpu.VMEM((N_META + seq, HEAD_DIM), BF16),
            pltpu.VMEM((2, 2 * tq, HEAD_DIM), BF16),
            pltpu.VMEM((2, 2 * tq, seq), F32),
            pltpu.VMEM((2, 2 * tq, N_META), F32),
            pltpu.VMEM((2, tq, seq), BF16),
            pltpu.VMEM((2, tq, N_META), BF16),
        ],
        compiler_params=pltpu.CompilerParams(
            dimension_semantics=("parallel", "parallel"), vmem_limit_bytes=VMEM_LIMIT),
        name="attn_prompt",
    )(lam, q, k_all, v_all, diag, meta_b, gs)


def _attn_sample_kernel(lam_ref, q_ref, kn_ref, vn_ref, ck_ref, cv_ref, tail_ref, gs_ref,
                        o_ref, s_ref):
    t = q_ref.shape[0]
    n_cache = ck_ref.shape[1] // HEADS
    split = n_cache - SAMPLE_TAIL
    width = n_cache + t
    lam = lam_ref[0, 0]
    for h in range(HEADS):
        slot = h % 2
        cs = slice(h * HEAD_DIM, (h + 1) * HEAD_DIM)
        q2 = _stack_maps(q_ref[:, cs])
        far = pl.ds(h, split, stride=HEADS)
        near = pl.ds(split * HEADS + h, SAMPLE_TAIL, stride=HEADS)
        new = pl.ds(h, t, stride=HEADS)
        kt = jnp.concatenate([ck_ref[0, near, :].astype(BF16), kn_ref[new, :].astype(BF16)], axis=0)
        vt = jnp.concatenate([cv_ref[0, near, :].astype(BF16), vn_ref[new, :].astype(BF16)], axis=0)
        bt = tail_ref[h]
        s_ref[slot, :, 0:split] = _dot_nt(q2, ck_ref[0, far, :].astype(BF16))
        s_ref[slot, :, split:width] = _dot_nt(q2, kt) + jnp.concatenate([bt, bt], axis=0)
        m = jnp.max(s_ref[slot, :, 0:width], axis=-1, keepdims=True)
        p = jnp.exp2(s_ref[slot, :, 0:width] - m)
        r = 1.0 / jnp.sum(p, axis=-1, keepdims=True)
        s_ref[slot, :, 0:width] = p
        r1 = r[:t]
        r2 = lam * r[t:]
        w = (s_ref[slot, :t, 0:width] * r1 - s_ref[slot, t:, 0:width] * r2).astype(BF16)
        o = _dot(w[:, 0:split], cv_ref[0, far, :].astype(BF16)) + _dot(w[:, split:width], vt)
        o_ref[:, cs] = _finish_head(o, gs_ref[...])


def _attn_sample(q, k_new, v_new, cache_k, cache_v, tail_b, lam, gs, batch, t):
    n_cache = cache_k.shape[1] // HEADS
    row = pl.BlockSpec((t, ATTN_WIDTH), lambda b: (b, 0))
    new = pl.BlockSpec((t * HEADS, HEAD_DIM), lambda b: (b, 0))
    cache = pl.BlockSpec((1,) + cache_k.shape[1:], lambda b: (b, 0, 0))
    width = -(-(n_cache + t) // HEAD_DIM) * HEAD_DIM
    return pl.pallas_call(
        _attn_sample_kernel,
        grid=(batch,),
        in_specs=[_smem(), row, new, new, cache, cache,
                  _resident(tail_b.shape), _resident((1, HEAD_DIM))],
        out_specs=row,
        out_shape=jax.ShapeDtypeStruct((batch * t, ATTN_WIDTH), BF16),
        scratch_shapes=[pltpu.VMEM((2, 2 * t, width), F32)],
        compiler_params=pltpu.CompilerParams(
            dimension_semantics=("parallel",), vmem_limit_bytes=VMEM_LIMIT),
        name="attn_sample",
    )(lam, q, k_new, v_new, cache_k, cache_v, tail_b, gs)


def _pool_mix(ext, wp_ref, scale):
    outs = []
    for g, w in enumerate(POOL_WINDOWS):
        xg = ext[:, g * POOL_GROUP_DIM:(g + 1) * POOL_GROUP_DIM]
        s = xg
        span = 1
        while span < w:
            s = s + pltpu.roll(s, span, axis=0)
            span *= 2
        p = (s[N_META:] * (1.0 / w) - xg[N_META:]).astype(BF16)
        outs.append(_dot(p, wp_ref[g]))
    return jnp.concatenate(outs, axis=-1) * scale


def _mix_out(a, b_ref, x_ref, wo_ref, o_ref):
    mix = jnp.concatenate([a.astype(BF16), b_ref[...]], axis=-1)
    o_ref[...] = x_ref[...] + _dot(mix, wo_ref[...])


def _out_prompt_kernel(u_ref, b_ref, x_ref, wp_ref, ps_ref, wo_ref, o_ref):
    _mix_out(_pool_mix(u_ref[...], wp_ref, ps_ref[...]), b_ref, x_ref, wo_ref, o_ref)


def _out_sample_kernel(u_ref, state_ref, b_ref, x_ref, wp_ref, ps_ref, wo_ref, o_ref):
    n_seq, t = state_ref.shape[0], u_ref.shape[0] // state_ref.shape[0]
    parts = []
    for s in range(n_seq):
        ext = jnp.concatenate([state_ref[s], u_ref[s * t:(s + 1) * t, :]], axis=0)
        parts.append(_pool_mix(ext, wp_ref, ps_ref[...]))
    _mix_out(jnp.concatenate(parts, axis=0), b_ref, x_ref, wo_ref, o_ref)


def _out_common_specs(rows, idx):
    return (pl.BlockSpec((rows, ATTN_WIDTH), idx), pl.BlockSpec((rows, D_MODEL), idx))


def _out_prompt(u, b, x, wp, ps, wo, batch, seq):
    rows = PROJ_ROWS
    nt = seq // rows
    idx = lambda bi, i: (bi * nt + i, 0)
    u_spec = pl.BlockSpec(
        (pl.Squeezed(), pl.Element(N_META + rows), pl.Element(POOL_WIDTH)),
        lambda bi, i: (bi, i * rows, 0))
    b_spec, x_spec = _out_common_specs(rows, idx)
    return pl.pallas_call(
        _out_prompt_kernel,
        grid=(batch, nt),
        in_specs=[u_spec, b_spec, x_spec,
                  _resident(wp.shape), _resident(ps.shape), _resident(wo.shape)],
        out_specs=x_spec,
        out_shape=jax.ShapeDtypeStruct(x.shape, F32),
        compiler_params=pltpu.CompilerParams(
            dimension_semantics=("parallel", "parallel"), vmem_limit_bytes=VMEM_LIMIT),
        name="out_proj_prompt",
    )(u, b, x, wp, ps, wo)


def _out_sample(u, state16, b, x, wp, ps, wo, t):
    rows = PROJ_ROWS
    n_seq = rows // t
    idx = lambda i: (i, 0)
    b_spec, x_spec = _out_common_specs(rows, idx)
    return pl.pallas_call(
        _out_sample_kernel,
        grid=(x.shape[0] // rows,),
        in_specs=[
            pl.BlockSpec((rows, POOL_WIDTH), idx),
            pl.BlockSpec((n_seq, N_META, POOL_WIDTH), lambda i: (i, 0, 0)),
            b_spec, x_spec,
            _resident(wp.shape), _resident(ps.shape), _resident(wo.shape),
        ],
        out_specs=x_spec,
        out_shape=jax.ShapeDtypeStruct(x.shape, F32),
        compiler_params=pltpu.CompilerParams(
            dimension_semantics=("parallel",), vmem_limit_bytes=VMEM_LIMIT),
        name="out_proj_sample",
    )(u, state16, b, x, wp, ps, wo)


def _ffn_kernel(h_ref, gf_ref, wg_ref, wu_ref, wd_ref, gfin_ref, y_ref, n_ref):
    f = pl.program_id(1)

    @pl.when(f == 0)
    def _():
        h = h_ref[...]
        n_ref[...] = _rms(h, gf_ref[...], EPS).astype(BF16)
        y_ref[...] = h

    n = n_ref[...]
    g = _dot(n, wg_ref[...].astype(BF16))
    u = _dot(n, wu_ref[...].astype(BF16))
    act = (g * (1.0 / (1.0 + jnp.exp(-g))) * u).astype(BF16)
    y_ref[...] += _dot(act, wd_ref[...].astype(BF16))

    @pl.when(f == pl.num_programs(1) - 1)
    def _():
        y_ref[...] = _rms(y_ref[...], gfin_ref[...], EPS)


def _ffn(h, gf, wg, wu, wd, gfin):
    rows, cols = FFN_ROWS, FFN_COLS
    row = pl.BlockSpec((rows, D_MODEL), lambda i, f: (i, 0))
    vec = pl.BlockSpec((1, D_MODEL), lambda i, f: (0, 0))
    return pl.pallas_call(
        _ffn_kernel,
        grid=(h.shape[0] // rows, D_FF // cols),
        in_specs=[
            row, vec,
            pl.BlockSpec((D_MODEL, cols), lambda i, f: (0, f)),
            pl.BlockSpec((D_MODEL, cols), lambda i, f: (0, f)),
            pl.BlockSpec((cols, D_MODEL), lambda i, f: (f, 0)),
            vec,
        ],
        out_specs=row,
        out_shape=jax.ShapeDtypeStruct(h.shape, F32),
        scratch_shapes=[pltpu.VMEM((rows, D_MODEL), BF16)],
        compiler_params=pltpu.CompilerParams(
            dimension_semantics=("parallel", "arbitrary"), vmem_limit_bytes=VMEM_LIMIT),
        name="swiglu",
    )(h, gf, wg, wu, wd, gfin)


def kernel(x_prompt, x_sample, cache_k, cache_v, state_pool, meta, g_mix, w_in, w_pool,
           pool_scale, lambda_q1, lambda_k1, lambda_q2, lambda_k2, g_subln, w_out, g_ffn,
           w_gate, w_up, w_down, rel_bias, g_final):
    batch, seq, d = x_prompt.shape
    dec_batch, t, _ = x_sample.shape
    depth = w_in.shape[0]
    assert depth == 1 and d == D_MODEL and meta.shape == (N_META, D_MODEL)
    assert seq % PROJ_ROWS == 0 and seq % Q_TILE == 0 and PROJ_ROWS % t == 0
    assert (dec_batch * t) % PROJ_ROWS == 0 and t >= POOL_STATE
    assert (N_META + seq) % (PROMPT_TILES * 16) == 0
    prompt_rows = (N_META + seq) // PROMPT_TILES
    n_cache = cache_k.shape[2]
    assert (n_cache - N_META) % CHUNK == 0 and n_cache > SAMPLE_TAIL

    w_in_b = w_in[0].astype(BF16)
    w_pool_b = w_pool[0].astype(BF16)
    w_out_b = w_out[0].astype(BF16)
    g_mix2 = g_mix[0].reshape(1, D_MODEL)
    g_ffn2 = g_ffn[0].reshape(1, D_MODEL)
    g_fin2 = g_final.reshape(1, D_MODEL)
    g_sub2 = g_subln[0].reshape(1, HEAD_DIM)
    ps2 = pool_scale[0].reshape(1, POOL_WIDTH)

    diag_b, meta_b, tail_b, lam = _bias_tables(
        rel_bias, lambda_q1, lambda_k1, lambda_q2, lambda_k2, t)

    xp = x_prompt.reshape(batch * seq, D_MODEL)
    xs = x_sample.reshape(dec_batch * t, D_MODEL)
    u_p, q_p, k_all, v_all = _project_prompt(x_prompt, meta, g_mix2, w_in_b, prompt_rows)
    u_s, q_s, k_s, v_s = _project_rows(xs, g_mix2, w_in_b, PROJ_ROWS)

    b_p = _attn_prompt(q_p, k_all, v_all, diag_b, meta_b, lam, g_sub2, batch, seq)
    b_s = _attn_sample(q_s, k_s, v_s,
                       cache_k[0].reshape(dec_batch, n_cache * HEADS, HEAD_DIM),
                       cache_v[0].reshape(dec_batch, n_cache * HEADS, HEAD_DIM),
                       tail_b, lam, g_sub2, dec_batch, t)

    h_p = _out_prompt(u_p, b_p, xp, w_pool_b, ps2, w_out_b, batch, seq)
    state16 = jnp.pad(state_pool[0], ((0, 0), (N_META - POOL_STATE, 0), (0, 0)))
    h_s = _out_sample(u_s, state16, b_s, xs, w_pool_b, ps2, w_out_b, t)

    y_p = _ffn(h_p, g_ffn2, w_gate[0], w_up[0], w_down[0], g_fin2)
    y_s = _ffn(h_s, g_ffn2, w_gate[0], w_up[0], w_down[0], g_fin2)

    hd = (HEADS, HEAD_DIM)
    return (
        y_p.reshape(batch, seq, D_MODEL),
        y_s.reshape(dec_batch, t, D_MODEL),
        k_all.reshape(1, batch, N_META + seq, *hd),
        v_all.reshape(1, batch, N_META + seq, *hd),
        u_p[None, :, N_META + seq - POOL_STATE:],
        k_s.reshape(1, dec_batch, t, *hd),
        v_s.reshape(1, dec_batch, t, *hd),
        u_s.reshape(dec_batch, t, POOL_WIDTH)[None, :, t - POOL_STATE:],
    )
```

```python
import functools
import math

import jax
import jax.numpy as jnp
from jax import lax
from jax.experimental import pallas as pl
from jax.experimental.pallas import tpu as pltpu

F32 = jnp.float32
BF16 = jnp.bfloat16

D_MODEL = 2048
CHUNK = 64
N_META = 16
POOL_WINDOWS = (2, 4, 8, 16)
POOL_GROUP_DIM = 256
POOL_WIDTH = 1024
POOL_STATE = 15
HEADS = 8
HEAD_DIM = 128
HALF_DIM = 64
ATTN_WIDTH = 1024
ATT_SCALE = HALF_DIM ** -0.5
LOG2E = math.log2(math.e)
D_FF = 5632
EPS = 1e-6
SUBLN_EPS = 1e-5
NEG_INF = -1e30
LAMBDA_INIT = 0.8 - 0.6 * math.exp(-0.3 * 0)

_BUCKET_THRESHOLDS = (1, 2, 3, 4, 5, 6, 7, 8, 12, 16, 23, 32, 46, 64, 91)

Q_TILE = 256
SOFTMAX_ROWS = 32
PROJ_ROWS = 512
PROMPT_TILES = 3
FFN_ROWS = 1024
FFN_COLS = 256
SAMPLE_TAIL = 272
VMEM_LIMIT = 56 * 1024 * 1024


def _rms(x, g, eps):
    ms = jnp.mean(x * x, axis=-1, keepdims=True)
    return x * lax.rsqrt(ms + eps) * g


def _dot(a, b):
    return jnp.dot(a, b, preferred_element_type=F32)


def _dot_nt(a, b):
    return lax.dot_general(a, b, (((1,), (1,)), ((), ())), preferred_element_type=F32)


def _resident(shape):
    nd = len(shape)
    return pl.BlockSpec(shape, lambda *_: (0,) * nd, pipeline_mode=pl.Buffered(1))


def _smem():
    return pl.BlockSpec(memory_space=pltpu.SMEM)


def _bias_of_rel(rel, table_ref, h):
    n = jnp.abs(rel)
    far = table_ref[len(_BUCKET_THRESHOLDS), h]

    def entry(b):
        return (table_ref[b, h] - far) * LOG2E

    neg = jnp.full(rel.shape, entry(0), F32)
    pos = jnp.full(rel.shape, entry(16), F32)
    for b, thr in enumerate(_BUCKET_THRESHOLDS, start=1):
        ge = n >= thr
        neg = jnp.where(ge, entry(b), neg)
        pos = jnp.where(ge, entry(16 + b), pos)
    return jnp.where(rel > 0, pos, neg)


def _bias_kernel(table_ref, lq1_ref, lk1_ref, lq2_ref, lk2_ref,
                 diag_ref, meta_ref, tail_ref, lam_ref):
    h = pl.program_id(0)
    t = Q_TILE
    qi = lax.broadcasted_iota(jnp.int32, (t, t), 0)
    kj = lax.broadcasted_iota(jnp.int32, (t, t), 1)
    visible = (kj // CHUNK) <= (qi // CHUNK)
    diag_ref[0, 0] = jnp.where(visible, _bias_of_rel(kj - qi, table_ref, h), NEG_INF)
    diag_ref[0, 1] = _bias_of_rel(kj - t - qi, table_ref, h)
    qm = lax.broadcasted_iota(jnp.int32, (t, N_META), 0)
    km = lax.broadcasted_iota(jnp.int32, (t, N_META), 1)
    meta_ref[0] = _bias_of_rel(km - N_META - qm, table_ref, h)
    qs = lax.broadcasted_iota(jnp.int32, tail_ref.shape[1:], 0)
    ks = lax.broadcasted_iota(jnp.int32, tail_ref.shape[1:], 1)
    tail_ref[0] = _bias_of_rel(ks - SAMPLE_TAIL - qs, table_ref, h)
    s1 = jnp.sum(lq1_ref[...] * lk1_ref[...], axis=-1, keepdims=True)
    s2 = jnp.sum(lq2_ref[...] * lk2_ref[...], axis=-1, keepdims=True)
    lam_ref[...] = jnp.exp(s1) - jnp.exp(s2) + LAMBDA_INIT


def _bias_tables(rel_bias, lq1, lk1, lq2, lk2, n_new):
    tail_w = SAMPLE_TAIL + n_new
    vec = pl.BlockSpec((1, HALF_DIM), lambda h: (0, 0))
    return pl.pallas_call(
        _bias_kernel,
        grid=(HEADS,),
        in_specs=[_smem(), vec, vec, vec, vec],
        out_specs=[
            pl.BlockSpec((1, 2, Q_TILE, Q_TILE), lambda h: (h, 0, 0, 0)),
            pl.BlockSpec((1, Q_TILE, N_META), lambda h: (h, 0, 0)),
            pl.BlockSpec((1, n_new, tail_w), lambda h: (h, 0, 0)),
            pl.BlockSpec((1, 1), lambda h: (0, 0)),
        ],
        out_shape=[
            jax.ShapeDtypeStruct((HEADS, 2, Q_TILE, Q_TILE), F32),
            jax.ShapeDtypeStruct((HEADS, Q_TILE, N_META), F32),
            jax.ShapeDtypeStruct((HEADS, n_new, tail_w), F32),
            jax.ShapeDtypeStruct((1, 1), F32),
        ],
        name="bias_tables",
    )(rel_bias, lq1, lk1, lq2, lk2)


def _store_heads(o_ref, x):
    rows = x.shape[0]
    for h in range(HEADS):
        o_ref[pl.ds(h, rows, stride=HEADS), :] = x[:, h * HEAD_DIM:(h + 1) * HEAD_DIM]


def _project_store(xn, w_ref, u_ref, q_ref, k_ref, v_ref):
    w = POOL_WIDTH
    u_ref[...] = _dot(xn, w_ref[:, 0 * w:1 * w])
    q_ref[...] = (_dot(xn, w_ref[:, 1 * w:2 * w]) * (ATT_SCALE * LOG2E)).astype(BF16)
    _store_heads(k_ref, _dot(xn, w_ref[:, 2 * w:3 * w]))
    _store_heads(v_ref, _dot(xn, w_ref[:, 3 * w:4 * w]))


def _proj_rows_kernel(x_ref, g_ref, w_ref, u_ref, q_ref, k_ref, v_ref):
    xn = _rms(x_ref[...], g_ref[...], EPS).astype(BF16)
    _project_store(xn, w_ref, u_ref, q_ref, k_ref, v_ref)


def _proj_prompt_kernel(x_ref, meta_ref, g_ref, w_ref, u_ref, q_ref, k_ref, v_ref, xn_ref, carry_ref):
    i = pl.program_id(1)
    g = g_ref[...]
    rows = xn_ref.shape[0]

    @pl.when(i == 0)
    def _():
        xn_ref[0:N_META, :] = _rms(meta_ref[...], g, EPS).astype(BF16)

    @pl.when(i > 0)
    def _():
        xn_ref[0:N_META, :] = carry_ref[...]

    xn_ref[N_META:rows, :] = _rms(x_ref[0:rows - N_META, :], g, EPS).astype(BF16)

    @pl.when(i < pl.num_programs(1) - 1)
    def _():
        carry_ref[...] = _rms(x_ref[rows - N_META:rows, :], g, EPS).astype(BF16)

    _project_store(xn_ref[...], w_ref, u_ref, q_ref, k_ref, v_ref)


def _proj_out_shapes(lead, n):
    return [jax.ShapeDtypeStruct((*lead, n, POOL_WIDTH), F32),
            jax.ShapeDtypeStruct((*lead, n, POOL_WIDTH), BF16),
            jax.ShapeDtypeStruct((*lead, n * HEADS, HEAD_DIM), F32),
            jax.ShapeDtypeStruct((*lead, n * HEADS, HEAD_DIM), F32)]


def _project_rows(x, g, w_bf16, rows):
    n = x.shape[0]
    uq = pl.BlockSpec((rows, POOL_WIDTH), lambda i: (i, 0))
    kv = pl.BlockSpec((rows * HEADS, HEAD_DIM), lambda i: (i, 0))
    return pl.pallas_call(
        _proj_rows_kernel,
        grid=(n // rows,),
        in_specs=[pl.BlockSpec((rows, D_MODEL), lambda i: (i, 0)),
                  _resident((1, D_MODEL)), _resident(w_bf16.shape)],
        out_specs=[uq, uq, kv, kv],
        out_shape=_proj_out_shapes((), n),
        compiler_params=pltpu.CompilerParams(
            dimension_semantics=("parallel",), vmem_limit_bytes=VMEM_LIMIT),
        name="in_proj_rows",
    )(x, g, w_bf16)


def _project_prompt(x, meta, g, w_bf16, rows):
    batch, seq, _ = x.shape
    total = N_META + seq
    x_spec = pl.BlockSpec((None, rows, D_MODEL), lambda b, i: (b, i, 0))
    uq = pl.BlockSpec((None, rows, POOL_WIDTH), lambda b, i: (b, i, 0))
    kv = pl.BlockSpec((None, rows * HEADS, HEAD_DIM), lambda b, i: (b, i, 0))
    return pl.pallas_call(
        _proj_prompt_kernel,
        grid=(batch, total // rows),
        in_specs=[x_spec, _resident(meta.shape), _resident((1, D_MODEL)), _resident(w_bf16.shape)],
        out_specs=[uq, uq, kv, kv],
        out_shape=_proj_out_shapes((batch,), total),
        scratch_shapes=[pltpu.VMEM((rows, D_MODEL), BF16), pltpu.VMEM((N_META, D_MODEL), BF16)],
        compiler_params=pltpu.CompilerParams(
            dimension_semantics=("parallel", "arbitrary"), vmem_limit_bytes=VMEM_LIMIT),
        name="in_proj_prompt",
    )(x, meta, g, w_bf16)


def _stack_maps(q):
    lane = lax.broadcasted_iota(jnp.int32, q.shape, 1)
    zero = jnp.zeros_like(q)
    return jnp.concatenate(
        [jnp.where(lane < HALF_DIM, q, zero), jnp.where(lane >= HALF_DIM, q, zero)], axis=0)


def _finish_head(o, gs):
    return (_rms(o, gs, SUBLN_EPS) * (1.0 - LAMBDA_INIT)).astype(BF16)


def _attn_prompt_kernel(lam_ref, q_ref, k_ref, v_ref, diag_ref, meta_ref, gs_ref,
                        o_ref, kb_ref, vb_ref, q2_ref, s_ref, sm_ref, w_ref, wm_ref):
    tq = Q_TILE
    nt = o_ref.shape[0] // tq
    rb = SOFTMAX_ROWS
    lam = lam_ref[0, 0]
    head_rows = pl.ds(pl.program_id(1), kb_ref.shape[0], stride=HEADS)
    kb_ref[...] = k_ref[0, head_rows, :].astype(BF16)
    vb_ref[...] = v_ref[0, head_rows, :].astype(BF16)

    def score_steps(c):
        slot = c % 2

        def stack():
            q2_ref[slot] = _stack_maps(q_ref[N_META + c * tq:N_META + (c + 1) * tq, :])

        def frames(j):
            d = _dot_nt(q2_ref[slot], kb_ref[N_META + j * tq:N_META + (j + 1) * tq, :])
            cols = slice(j * tq, (j + 1) * tq)
            if j >= c - 1:
                bias = diag_ref[0, c - j]
                s_ref[slot, :tq, cols] = d[:tq] + bias
                s_ref[slot, tq:, cols] = d[tq:] + bias
            else:
                s_ref[slot, :, cols] = d

        def metas():
            sm = _dot_nt(q2_ref[slot], kb_ref[0:N_META, :])
            if c == 0:
                sm = sm + jnp.concatenate([meta_ref[0], meta_ref[0]], axis=0)
            sm_ref[slot] = sm

        return [stack, metas] + [functools.partial(frames, j) for j in range(c + 1)]

    def softmax_step(c, i):
        slot = c % 2
        n = (c + 1) * tq
        rows = (slice(i * rb, (i + 1) * rb), slice(tq + i * rb, tq + (i + 1) * rb))
        p, pm, r = [], [], []
        for rmap in rows:
            sm = sm_ref[slot, rmap, :]
            m = jnp.maximum(jnp.max(s_ref[slot, rmap, :n], axis=-1, keepdims=True),
                            jnp.max(sm, axis=-1, keepdims=True))
            p.append(jnp.exp2(s_ref[slot, rmap, :n] - m))
            pm.append(jnp.exp2(sm - m))
            r.append(1.0 / (jnp.sum(p[-1], axis=-1, keepdims=True) + jnp.sum(pm[-1], axis=-1, keepdims=True)))
        r1, r2 = r[0], lam * r[1]
        w_ref[slot, rows[0], :n] = (p[0] * r1 - p[1] * r2).astype(BF16)
        wm_ref[slot, rows[0], :] = (pm[0] * r1 - pm[1] * r2).astype(BF16)

    def value_step(c):
        slot = c % 2
        n = (c + 1) * tq
        o = _dot(w_ref[slot, :, :n], vb_ref[N_META:N_META + n, :]) + _dot(wm_ref[slot], vb_ref[0:N_META, :])
        o_ref[c * tq:(c + 1) * tq, :] = _finish_head(o, gs_ref[...])

    for step in score_steps(0):
        step()
    for c in range(nt):
        ahead = score_steps(c + 1) if c + 1 < nt else []
        blocks = tq // rb
        for i in range(blocks):
            softmax_step(c, i)
            for step in ahead[i * len(ahead) // blocks:(i + 1) * len(ahead) // blocks]:
                step()
        value_step(c)


def _attn_prompt(q, k_all, v_all, diag, meta_b, lam, gs, batch, seq):
    tq = Q_TILE
    kv = pl.BlockSpec((1, (N_META + seq) * HEADS, HEAD_DIM), lambda b, h: (b, 0, 0))
    qi = pl.BlockSpec((None, N_META + seq, HEAD_DIM), lambda b, h: (b, 0, h))
    qo = pl.BlockSpec((seq, HEAD_DIM), lambda b, h: (b, h))
    return pl.pallas_call(
        _attn_prompt_kernel,
        grid=(batch, HEADS),
        in_specs=[
            _smem(), qi, kv, kv,
            pl.BlockSpec((1, 2, tq, tq), lambda b, h: (h, 0, 0, 0)),
            pl.BlockSpec((1, tq, N_META), lambda b, h: (h, 0, 0)),
            pl.BlockSpec((1, HEAD_DIM), lambda b, h: (0, 0)),
        ],
        out_specs=qo,
        out_shape=jax.ShapeDtypeStruct((batch * seq, ATTN_WIDTH), BF16),
        scratch_shapes=[
            pltpu.VMEM((N_META + seq, HEAD_DIM), BF16),
            pltpu.VMEM((N_META + seq, HEAD_DIM), BF16),
            pltpu.VMEM((2, 2 * tq, HEAD_DIM), BF16),
            pltpu.VMEM((2, 2 * tq, seq), F32),
            pltpu.VMEM((2, 2 * tq, N_META), F32),
            pltpu.VMEM((2, tq, seq), BF16),
            pltpu.VMEM((2, tq, N_META), BF16),
        ],
        compiler_params=pltpu.CompilerParams(
            dimension_semantics=("parallel", "parallel"), vmem_limit_bytes=VMEM_LIMIT),
        name="attn_prompt",
    )(lam, q, k_all, v_all, diag, meta_b, gs)


def _attn_sample_kernel(lam_ref, q_ref, kn_ref, vn_ref, ck_ref, cv_ref, tail_ref, gs_ref,
                        o_ref, qd_ref, s_ref, w_ref):
    t = q_ref.shape[0]
    n_cache = ck_ref.shape[1] // HEADS
    split = n_cache - SAMPLE_TAIL
    width = n_cache + t
    lam = lam_ref[0, 0]
    hd = HEAD_DIM

    def gather(c_ref, n_ref, h):
        far = pl.ds(h, split, stride=HEADS)
        near = pl.ds(split * HEADS + h, SAMPLE_TAIL, stride=HEADS)
        new = pl.ds(h, t, stride=HEADS)
        return (c_ref[0, far, :].astype(BF16),
                jnp.concatenate([c_ref[0, near, :].astype(BF16), n_ref[new, :].astype(BF16)], axis=0))

    def pair_rows(c_ref, n_ref, hp):
        fa, ta = gather(c_ref, n_ref, 2 * hp)
        fb, tb = gather(c_ref, n_ref, 2 * hp + 1)
        return jnp.concatenate([fa, fb], axis=1), jnp.concatenate([ta, tb], axis=1)

    def score_steps(hp):
        slot = hp % 2

        def queries():
            zero = jnp.zeros((2 * t, hd), BF16)
            qa = _stack_maps(q_ref[:, 2 * hp * hd:(2 * hp + 1) * hd])
            qb = _stack_maps(q_ref[:, (2 * hp + 1) * hd:(2 * hp + 2) * hd])
            qd_ref[slot] = jnp.concatenate(
                [jnp.concatenate([qa, zero], axis=1), jnp.concatenate([zero, qb], axis=1)], axis=0)

        def scores():
            k_far, k_tail = pair_rows(ck_ref, kn_ref, hp)
            s_ref[slot, :, 0:split] = _dot_nt(qd_ref[slot], k_far)
            ba, bb = tail_ref[2 * hp], tail_ref[2 * hp + 1]
            s_ref[slot, :, split:width] = (
                _dot_nt(qd_ref[slot], k_tail) + jnp.concatenate([ba, ba, bb, bb], axis=0))

        return [queries, scores]

    def softmax_step(hp, j):
        slot = hp % 2
        r, p = [], []
        for rows in (slice(2 * t * j, 2 * t * j + t), slice(2 * t * j + t, 2 * t * (j + 1))):
            m = jnp.max(s_ref[slot, rows, 0:width], axis=-1, keepdims=True)
            p.append(jnp.exp2(s_ref[slot, rows, 0:width] - m))
            r.append(1.0 / jnp.sum(p[-1], axis=-1, keepdims=True))
        w_ref[slot, t * j:t * (j + 1), 0:width] = (p[0] * r[0] - p[1] * (lam * r[1])).astype(BF16)

    def value_step(hp):
        slot = hp % 2
        v_far, v_tail = pair_rows(cv_ref, vn_ref, hp)
        o = _dot(w_ref[slot, :, 0:split], v_far) + _dot(w_ref[slot, :, split:width], v_tail)
        gs = gs_ref[...]
        o_ref[:, 2 * hp * hd:(2 * hp + 1) * hd] = _finish_head(o[0:t, 0:hd], gs)
        o_ref[:, (2 * hp + 1) * hd:(2 * hp + 2) * hd] = _finish_head(o[t:2 * t, hd:2 * hd], gs)

    pairs = HEADS // 2
    for step in score_steps(0):
        step()
    for hp in range(pairs):
        ahead = score_steps(hp + 1) if hp + 1 < pairs else [lambda: None, lambda: None]
        softmax_step(hp, 0)
        ahead[0]()
        ahead[1]()
        softmax_step(hp, 1)
        value_step(hp)


def _attn_sample(q, k_new, v_new, cache_k, cache_v, tail_b, lam, gs, batch, t):
    n_cache = cache_k.shape[1] // HEADS
    row = pl.BlockSpec((t, ATTN_WIDTH), lambda b: (b, 0))
    new = pl.BlockSpec((t * HEADS, HEAD_DIM), lambda b: (b, 0))
    cache = pl.BlockSpec((1,) + cache_k.shape[1:], lambda b: (b, 0, 0))
    width = -(-(n_cache + t) // HEAD_DIM) * HEAD_DIM
    return pl.pallas_call(
        _attn_sample_kernel,
        grid=(batch,),
        in_specs=[_smem(), row, new, new, cache, cache,
                  _resident(tail_b.shape), _resident((1, HEAD_DIM))],
        out_specs=row,
        out_shape=jax.ShapeDtypeStruct((batch * t, ATTN_WIDTH), BF16),
        scratch_shapes=[
            pltpu.VMEM((2, 4 * t, 2 * HEAD_DIM), BF16),
            pltpu.VMEM((2, 4 * t, width), F32),
            pltpu.VMEM((2, 2 * t, width), BF16),
        ],
        compiler_params=pltpu.CompilerParams(
            dimension_semantics=("parallel",), vmem_limit_bytes=VMEM_LIMIT),
        name="attn_sample",
    )(lam, q, k_new, v_new, cache_k, cache_v, tail_b, gs)


def _pool_mix(ext, wp_ref, scale):
    outs = []
    for g, w in enumerate(POOL_WINDOWS):
        xg = ext[:, g * POOL_GROUP_DIM:(g + 1) * POOL_GROUP_DIM]
        s = xg
        span = 1
        while span < w:
            s = s + pltpu.roll(s, span, axis=0)
            span *= 2
        p = (s[N_META:] * (1.0 / w) - xg[N_META:]).astype(BF16)
        outs.append(_dot(p, wp_ref[g]))
    return jnp.concatenate(outs, axis=-1) * scale


def _mix_out(a, b_ref, x_ref, wo_ref, o_ref):
    mix = jnp.concatenate([a.astype(BF16), b_ref[...]], axis=-1)
    o_ref[...] = x_ref[...] + _dot(mix, wo_ref[...])


def _out_prompt_kernel(u_ref, b_ref, x_ref, wp_ref, ps_ref, wo_ref, o_ref):
    _mix_out(_pool_mix(u_ref[...], wp_ref, ps_ref[...]), b_ref, x_ref, wo_ref, o_ref)


def _out_sample_kernel(u_ref, state_ref, b_ref, x_ref, wp_ref, ps_ref, wo_ref, o_ref):
    n_seq, t = state_ref.shape[0], u_ref.shape[0] // state_ref.shape[0]
    parts = []
    for s in range(n_seq):
        ext = jnp.concatenate([state_ref[s], u_ref[s * t:(s + 1) * t, :]], axis=0)
        parts.append(_pool_mix(ext, wp_ref, ps_ref[...]))
    _mix_out(jnp.concatenate(parts, axis=0), b_ref, x_ref, wo_ref, o_ref)


def _out_common_specs(rows, idx):
    return (pl.BlockSpec((rows, ATTN_WIDTH), idx), pl.BlockSpec((rows, D_MODEL), idx))


def _out_prompt(u, b, x, wp, ps, wo, batch, seq):
    rows = PROJ_ROWS
    nt = seq // rows
    idx = lambda bi, i: (bi * nt + i, 0)
    u_spec = pl.BlockSpec(
        (pl.Squeezed(), pl.Element(N_META + rows), pl.Element(POOL_WIDTH)),
        lambda bi, i: (bi, i * rows, 0))
    b_spec, x_spec = _out_common_specs(rows, idx)
    return pl.pallas_call(
        _out_prompt_kernel,
        grid=(batch, nt),
        in_specs=[u_spec, b_spec, x_spec,
                  _resident(wp.shape), _resident(ps.shape), _resident(wo.shape)],
        out_specs=x_spec,
        out_shape=jax.ShapeDtypeStruct(x.shape, F32),
        compiler_params=pltpu.CompilerParams(
            dimension_semantics=("parallel", "parallel"), vmem_limit_bytes=VMEM_LIMIT),
        name="out_proj_prompt",
    )(u, b, x, wp, ps, wo)


def _out_sample(u, state16, b, x, wp, ps, wo, t):
    rows = PROJ_ROWS
    n_seq = rows // t
    idx = lambda i: (i, 0)
    b_spec, x_spec = _out_common_specs(rows, idx)
    return pl.pallas_call(
        _out_sample_kernel,
        grid=(x.shape[0] // rows,),
        in_specs=[
            pl.BlockSpec((rows, POOL_WIDTH), idx),
            pl.BlockSpec((n_seq, N_META, POOL_WIDTH), lambda i: (i, 0, 0)),
            b_spec, x_spec,
            _resident(wp.shape), _resident(ps.shape), _resident(wo.shape),
        ],
        out_specs=x_spec,
        out_shape=jax.ShapeDtypeStruct(x.shape, F32),
        compiler_params=pltpu.CompilerParams(
            dimension_semantics=("parallel",), vmem_limit_bytes=VMEM_LIMIT),
        name="out_proj_sample",
    )(u, state16, b, x, wp, ps, wo)


def _ffn_kernel(h_ref, gf_ref, wg_ref, wu_ref, wd_ref, gfin_ref, y_ref, n_ref):
    f = pl.program_id(1)

    @pl.when(f == 0)
    def _():
        h = h_ref[...]
        n_ref[...] = _rms(h, gf_ref[...], EPS).astype(BF16)
        y_ref[...] = h

    n = n_ref[...]
    g = _dot(n, wg_ref[...].astype(BF16))
    u = _dot(n, wu_ref[...].astype(BF16))
    act = (g * (1.0 / (1.0 + jnp.exp(-g))) * u).astype(BF16)
    y_ref[...] += _dot(act, wd_ref[...].astype(BF16))

    @pl.when(f == pl.num_programs(1) - 1)
    def _():
        y_ref[...] = _rms(y_ref[...], gfin_ref[...], EPS)


def _ffn(h, gf, wg, wu, wd, gfin):
    rows, cols = FFN_ROWS, FFN_COLS
    row = pl.BlockSpec((rows, D_MODEL), lambda i, f: (i, 0))
    vec = pl.BlockSpec((1, D_MODEL), lambda i, f: (0, 0))
    return pl.pallas_call(
        _ffn_kernel,
        grid=(h.shape[0] // rows, D_FF // cols),
        in_specs=[
            row, vec,
            pl.BlockSpec((D_MODEL, cols), lambda i, f: (0, f)),
            pl.BlockSpec((D_MODEL, cols), lambda i, f: (0, f)),
            pl.BlockSpec((cols, D_MODEL), lambda i, f: (f, 0)),
            vec,
        ],
        out_specs=row,
        out_shape=jax.ShapeDtypeStruct(h.shape, F32),
        scratch_shapes=[pltpu.VMEM((rows, D_MODEL), BF16)],
        compiler_params=pltpu.CompilerParams(
            dimension_semantics=("parallel", "arbitrary"), vmem_limit_bytes=VMEM_LIMIT),
        name="swiglu",
    )(h, gf, wg, wu, wd, gfin)


def kernel(x_prompt, x_sample, cache_k, cache_v, state_pool, meta, g_mix, w_in, w_pool,
           pool_scale, lambda_q1, lambda_k1, lambda_q2, lambda_k2, g_subln, w_out, g_ffn,
           w_gate, w_up, w_down, rel_bias, g_final):
    batch, seq, d = x_prompt.shape
    dec_batch, t, _ = x_sample.shape
    depth = w_in.shape[0]
    assert depth == 1 and d == D_MODEL and meta.shape == (N_META, D_MODEL)
    assert seq % PROJ_ROWS == 0 and seq % Q_TILE == 0 and PROJ_ROWS % t == 0
    assert (dec_batch * t) % PROJ_ROWS == 0 and t >= POOL_STATE
    assert (N_META + seq) % (PROMPT_TILES * 16) == 0
    prompt_rows = (N_META + seq) // PROMPT_TILES
    n_cache = cache_k.shape[2]
    assert (n_cache - N_META) % CHUNK == 0 and n_cache > SAMPLE_TAIL

    w_in_b = w_in[0].astype(BF16)
    w_pool_b = w_pool[0].astype(BF16)
    w_out_b = w_out[0].astype(BF16)
    g_mix2 = g_mix[0].reshape(1, D_MODEL)
    g_ffn2 = g_ffn[0].reshape(1, D_MODEL)
    g_fin2 = g_final.reshape(1, D_MODEL)
    g_sub2 = g_subln[0].reshape(1, HEAD_DIM)
    ps2 = pool_scale[0].reshape(1, POOL_WIDTH)

    diag_b, meta_b, tail_b, lam = _bias_tables(
        rel_bias, lambda_q1, lambda_k1, lambda_q2, lambda_k2, t)

    xp = x_prompt.reshape(batch * seq, D_MODEL)
    xs = x_sample.reshape(dec_batch * t, D_MODEL)
    u_p, q_p, k_all, v_all = _project_prompt(x_prompt, meta, g_mix2, w_in_b, prompt_rows)
    u_s, q_s, k_s, v_s = _project_rows(xs, g_mix2, w_in_b, PROJ_ROWS)

    b_p = _attn_prompt(q_p, k_all, v_all, diag_b, meta_b, lam, g_sub2, batch, seq)
    b_s = _attn_sample(q_s, k_s, v_s,
                       cache_k[0].reshape(dec_batch, n_cache * HEADS, HEAD_DIM),
                       cache_v[0].reshape(dec_batch, n_cache * HEADS, HEAD_DIM),
                       tail_b, lam, g_sub2, dec_batch, t)

    h_p = _out_prompt(u_p, b_p, xp, w_pool_b, ps2, w_out_b, batch, seq)
    state16 = jnp.pad(state_pool[0], ((0, 0), (N_META - POOL_STATE, 0), (0, 0)))
    h_s = _out_sample(u_s, state16, b_s, xs, w_pool_b, ps2, w_out_b, t)

    y_p = _ffn(h_p, g_ffn2, w_gate[0], w_up[0], w_down[0], g_fin2)
    y_s = _ffn(h_s, g_ffn2, w_gate[0], w_up[0], w_down[0], g_fin2)

    hd = (HEADS, HEAD_DIM)
    return (
        y_p.reshape(batch, seq, D_MODEL),
        y_s.reshape(dec_batch, t, D_MODEL),
        k_all.reshape(1, batch, N_META + seq, *hd),
        v_all.reshape(1, batch, N_META + seq, *hd),
        u_p[None, :, N_META + seq - POOL_STATE:],
        k_s.reshape(1, dec_batch, t, *hd),
        v_s.reshape(1, dec_batch, t, *hd),
        u_s.reshape(dec_batch, t, POOL_WIDTH)[None, :, t - POOL_STATE:],
    )
```

```python
import functools
import math

import jax
import jax.numpy as jnp
from jax import lax
from jax.experimental import pallas as pl
from jax.experimental.pallas import tpu as pltpu

F32 = jnp.float32
BF16 = jnp.bfloat16

D_MODEL = 2048
CHUNK = 64
N_META = 16
POOL_WINDOWS = (2, 4, 8, 16)
POOL_GROUP_DIM = 256
POOL_WIDTH = 1024
POOL_STATE = 15
HEADS = 8
HEAD_DIM = 128
HALF_DIM = 64
ATTN_WIDTH = 1024
ATT_SCALE = HALF_DIM ** -0.5
LOG2E = math.log2(math.e)
D_FF = 5632
EPS = 1e-6
SUBLN_EPS = 1e-5
NEG_INF = -1e30
LAMBDA_INIT = 0.8 - 0.6 * math.exp(-0.3 * 0)

_BUCKET_THRESHOLDS = (1, 2, 3, 4, 5, 6, 7, 8, 12, 16, 23, 32, 46, 64, 91)

Q_TILE = 256
SOFTMAX_ROWS = 32
PROJ_ROWS = 512
PROMPT_TILES = 3
FFN_ROWS = 1024
FFN_COLS = 512
FFN_COLS_F32 = 256
SAMPLE_TAIL = 272
VMEM_LIMIT = 56 * 1024 * 1024


def _rms(x, g, eps):
    ms = jnp.mean(x * x, axis=-1, keepdims=True)
    return x * lax.rsqrt(ms + eps) * g


def _dot(a, b):
    return jnp.dot(a, b, preferred_element_type=F32)


def _dot_nt(a, b):
    return lax.dot_general(a, b, (((1,), (1,)), ((), ())), preferred_element_type=F32)


def _resident(shape):
    nd = len(shape)
    return pl.BlockSpec(shape, lambda *_: (0,) * nd, pipeline_mode=pl.Buffered(1))


def _smem():
    return pl.BlockSpec(memory_space=pltpu.SMEM)


def _bias_of_rel(rel, table_ref, h):
    n = jnp.abs(rel)
    far = table_ref[len(_BUCKET_THRESHOLDS), h]

    def entry(b):
        return (table_ref[b, h] - far) * LOG2E

    neg = jnp.full(rel.shape, entry(0), F32)
    pos = jnp.full(rel.shape, entry(16), F32)
    for b, thr in enumerate(_BUCKET_THRESHOLDS, start=1):
        ge = n >= thr
        neg = jnp.where(ge, entry(b), neg)
        pos = jnp.where(ge, entry(16 + b), pos)
    return jnp.where(rel > 0, pos, neg)


def _bias_kernel(table_ref, lq1_ref, lk1_ref, lq2_ref, lk2_ref,
                 diag_ref, meta_ref, tail_ref, lam_ref):
    h = pl.program_id(0)
    t = Q_TILE
    qi = lax.broadcasted_iota(jnp.int32, (t, t), 0)
    kj = lax.broadcasted_iota(jnp.int32, (t, t), 1)
    visible = (kj // CHUNK) <= (qi // CHUNK)
    diag_ref[0, 0] = jnp.where(visible, _bias_of_rel(kj - qi, table_ref, h), NEG_INF)
    diag_ref[0, 1] = _bias_of_rel(kj - t - qi, table_ref, h)
    qm = lax.broadcasted_iota(jnp.int32, (t, N_META), 0)
    km = lax.broadcasted_iota(jnp.int32, (t, N_META), 1)
    meta_ref[0] = _bias_of_rel(km - N_META - qm, table_ref, h)
    qs = lax.broadcasted_iota(jnp.int32, tail_ref.shape[1:], 0)
    ks = lax.broadcasted_iota(jnp.int32, tail_ref.shape[1:], 1)
    tail_ref[0] = _bias_of_rel(ks - SAMPLE_TAIL - qs, table_ref, h)
    s1 = jnp.sum(lq1_ref[...] * lk1_ref[...], axis=-1, keepdims=True)
    s2 = jnp.sum(lq2_ref[...] * lk2_ref[...], axis=-1, keepdims=True)
    lam_ref[...] = jnp.exp(s1) - jnp.exp(s2) + LAMBDA_INIT


def _bias_tables(rel_bias, lq1, lk1, lq2, lk2, n_new):
    tail_w = SAMPLE_TAIL + n_new
    vec = pl.BlockSpec((1, HALF_DIM), lambda h: (0, 0))
    return pl.pallas_call(
        _bias_kernel,
        grid=(HEADS,),
        in_specs=[_smem(), vec, vec, vec, vec],
        out_specs=[
            pl.BlockSpec((1, 2, Q_TILE, Q_TILE), lambda h: (h, 0, 0, 0)),
            pl.BlockSpec((1, Q_TILE, N_META), lambda h: (h, 0, 0)),
            pl.BlockSpec((1, n_new, tail_w), lambda h: (h, 0, 0)),
            pl.BlockSpec((1, 1), lambda h: (0, 0)),
        ],
        out_shape=[
            jax.ShapeDtypeStruct((HEADS, 2, Q_TILE, Q_TILE), F32),
            jax.ShapeDtypeStruct((HEADS, Q_TILE, N_META), F32),
            jax.ShapeDtypeStruct((HEADS, n_new, tail_w), F32),
            jax.ShapeDtypeStruct((1, 1), F32),
        ],
        name="bias_tables",
    )(rel_bias, lq1, lk1, lq2, lk2)


def _store_heads(o_ref, x):
    rows = x.shape[0]
    for h in range(HEADS):
        o_ref[pl.ds(h, rows, stride=HEADS), :] = x[:, h * HEAD_DIM:(h + 1) * HEAD_DIM]


def _project_store(xn, w_ref, u_ref, q_ref, k_ref, v_ref):
    w = POOL_WIDTH
    u_ref[...] = _dot(xn, w_ref[:, 0 * w:1 * w])
    q_ref[...] = (_dot(xn, w_ref[:, 1 * w:2 * w]) * (ATT_SCALE * LOG2E)).astype(BF16)
    _store_heads(k_ref, _dot(xn, w_ref[:, 2 * w:3 * w]))
    _store_heads(v_ref, _dot(xn, w_ref[:, 3 * w:4 * w]))


def _proj_rows_kernel(x_ref, g_ref, w_ref, u_ref, q_ref, k_ref, v_ref):
    xn = _rms(x_ref[...], g_ref[...], EPS).astype(BF16)
    _project_store(xn, w_ref, u_ref, q_ref, k_ref, v_ref)


def _proj_prompt_kernel(x_ref, meta_ref, g_ref, w_ref, u_ref, q_ref, k_ref, v_ref, xn_ref, carry_ref):
    i = pl.program_id(1)
    g = g_ref[...]
    rows = xn_ref.shape[0]

    @pl.when(i == 0)
    def _():
        xn_ref[0:N_META, :] = _rms(meta_ref[...], g, EPS).astype(BF16)

    @pl.when(i > 0)
    def _():
        xn_ref[0:N_META, :] = carry_ref[...]

    xn_ref[N_META:rows, :] = _rms(x_ref[0:rows - N_META, :], g, EPS).astype(BF16)

    @pl.when(i < pl.num_programs(1) - 1)
    def _():
        carry_ref[...] = _rms(x_ref[rows - N_META:rows, :], g, EPS).astype(BF16)

    _project_store(xn_ref[...], w_ref, u_ref, q_ref, k_ref, v_ref)


def _proj_out_shapes(lead, n):
    return [jax.ShapeDtypeStruct((*lead, n, POOL_WIDTH), F32),
            jax.ShapeDtypeStruct((*lead, n, POOL_WIDTH), BF16),
            jax.ShapeDtypeStruct((*lead, n * HEADS, HEAD_DIM), F32),
            jax.ShapeDtypeStruct((*lead, n * HEADS, HEAD_DIM), F32)]


def _project_rows(x, g, w_bf16, rows):
    n = x.shape[0]
    uq = pl.BlockSpec((rows, POOL_WIDTH), lambda i: (i, 0))
    kv = pl.BlockSpec((rows * HEADS, HEAD_DIM), lambda i: (i, 0))
    return pl.pallas_call(
        _proj_rows_kernel,
        grid=(n // rows,),
        in_specs=[pl.BlockSpec((rows, D_MODEL), lambda i: (i, 0)),
                  _resident((1, D_MODEL)), _resident(w_bf16.shape)],
        out_specs=[uq, uq, kv, kv],
        out_shape=_proj_out_shapes((), n),
        compiler_params=pltpu.CompilerParams(
            dimension_semantics=("parallel",), vmem_limit_bytes=VMEM_LIMIT),
        name="in_proj_rows",
    )(x, g, w_bf16)


def _project_prompt(x, meta, g, w_bf16, rows):
    batch, seq, _ = x.shape
    total = N_META + seq
    x_spec = pl.BlockSpec((None, rows, D_MODEL), lambda b, i: (b, i, 0))
    uq = pl.BlockSpec((None, rows, POOL_WIDTH), lambda b, i: (b, i, 0))
    kv = pl.BlockSpec((None, rows * HEADS, HEAD_DIM), lambda b, i: (b, i, 0))
    return pl.pallas_call(
        _proj_prompt_kernel,
        grid=(batch, total // rows),
        in_specs=[x_spec, _resident(meta.shape), _resident((1, D_MODEL)), _resident(w_bf16.shape)],
        out_specs=[uq, uq, kv, kv],
        out_shape=_proj_out_shapes((batch,), total),
        scratch_shapes=[pltpu.VMEM((rows, D_MODEL), BF16), pltpu.VMEM((N_META, D_MODEL), BF16)],
        compiler_params=pltpu.CompilerParams(
            dimension_semantics=("parallel", "arbitrary"), vmem_limit_bytes=VMEM_LIMIT),
        name="in_proj_prompt",
    )(x, meta, g, w_bf16)


def _stack_maps(q):
    lane = lax.broadcasted_iota(jnp.int32, q.shape, 1)
    zero = jnp.zeros_like(q)
    return jnp.concatenate(
        [jnp.where(lane < HALF_DIM, q, zero), jnp.where(lane >= HALF_DIM, q, zero)], axis=0)


def _finish_head(o, gs):
    return (_rms(o, gs, SUBLN_EPS) * (1.0 - LAMBDA_INIT)).astype(BF16)


def _attn_prompt_kernel(lam_ref, q_ref, k_ref, v_ref, diag_ref, meta_ref, gs_ref,
                        o_ref, kb_ref, vb_ref, q2_ref, s_ref, sm_ref, w_ref, wm_ref):
    tq = Q_TILE
    nt = o_ref.shape[0] // tq
    rb = SOFTMAX_ROWS
    lam = lam_ref[0, 0]
    head_rows = pl.ds(pl.program_id(1), kb_ref.shape[0], stride=HEADS)
    kb_ref[...] = k_ref[0, head_rows, :].astype(BF16)
    vb_ref[...] = v_ref[0, head_rows, :].astype(BF16)

    def score_steps(c):
        slot = c % 2

        def stack():
            q2_ref[slot] = _stack_maps(q_ref[N_META + c * tq:N_META + (c + 1) * tq, :])

        def frames(j):
            d = _dot_nt(q2_ref[slot], kb_ref[N_META + j * tq:N_META + (j + 1) * tq, :])
            cols = slice(j * tq, (j + 1) * tq)
            if j >= c - 1:
                bias = diag_ref[0, c - j]
                s_ref[slot, :tq, cols] = d[:tq] + bias
                s_ref[slot, tq:, cols] = d[tq:] + bias
            else:
                s_ref[slot, :, cols] = d

        def metas():
            sm = _dot_nt(q2_ref[slot], kb_ref[0:N_META, :])
            if c == 0:
                sm = sm + jnp.concatenate([meta_ref[0], meta_ref[0]], axis=0)
            sm_ref[slot] = sm

        return [stack, metas] + [functools.partial(frames, j) for j in range(c + 1)]

    def softmax_step(c, i):
        slot = c % 2
        n = (c + 1) * tq
        rows = (slice(i * rb, (i + 1) * rb), slice(tq + i * rb, tq + (i + 1) * rb))
        p, pm, r = [], [], []
        for rmap in rows:
            sm = sm_ref[slot, rmap, :]
            m = jnp.maximum(jnp.max(s_ref[slot, rmap, :n], axis=-1, keepdims=True),
                            jnp.max(sm, axis=-1, keepdims=True))
            p.append(jnp.exp2(s_ref[slot, rmap, :n] - m))
            pm.append(jnp.exp2(sm - m))
            r.append(1.0 / (jnp.sum(p[-1], axis=-1, keepdims=True) + jnp.sum(pm[-1], axis=-1, keepdims=True)))
        r1, r2 = r[0], lam * r[1]
        w_ref[slot, rows[0], :n] = (p[0] * r1 - p[1] * r2).astype(BF16)
        wm_ref[slot, rows[0], :] = (pm[0] * r1 - pm[1] * r2).astype(BF16)

    def value_step(c):
        slot = c % 2
        n = (c + 1) * tq
        o = _dot(w_ref[slot, :, :n], vb_ref[N_META:N_META + n, :]) + _dot(wm_ref[slot], vb_ref[0:N_META, :])
        o_ref[c * tq:(c + 1) * tq, :] = _finish_head(o, gs_ref[...])

    for step in score_steps(0):
        step()
    for c in range(nt):
        ahead = score_steps(c + 1) if c + 1 < nt else []
        blocks = tq // rb
        for i in range(blocks):
            softmax_step(c, i)
            for step in ahead[i * len(ahead) // blocks:(i + 1) * len(ahead) // blocks]:
                step()
        value_step(c)


def _attn_prompt(q, k_all, v_all, diag, meta_b, lam, gs, batch, seq):
    tq = Q_TILE
    kv = pl.BlockSpec((1, (N_META + seq) * HEADS, HEAD_DIM), lambda b, h: (b, 0, 0))
    qi = pl.BlockSpec((None, N_META + seq, HEAD_DIM), lambda b, h: (b, 0, h))
    qo = pl.BlockSpec((seq, HEAD_DIM), lambda b, h: (b, h))
    return pl.pallas_call(
        _attn_prompt_kernel,
        grid=(batch, HEADS),
        in_specs=[
            _smem(), qi, kv, kv,
            pl.BlockSpec((1, 2, tq, tq), lambda b, h: (h, 0, 0, 0)),
            pl.BlockSpec((1, tq, N_META), lambda b, h: (h, 0, 0)),
            pl.BlockSpec((1, HEAD_DIM), lambda b, h: (0, 0)),
        ],
        out_specs=qo,
        out_shape=jax.ShapeDtypeStruct((batch * seq, ATTN_WIDTH), BF16),
        scratch_shapes=[
            pltpu.VMEM((N_META + seq, HEAD_DIM), BF16),
            pltpu.VMEM((N_META + seq, HEAD_DIM), BF16),
            pltpu.VMEM((2, 2 * tq, HEAD_DIM), BF16),
            pltpu.VMEM((2, 2 * tq, seq), F32),
            pltpu.VMEM((2, 2 * tq, N_META), F32),
            pltpu.VMEM((2, tq, seq), BF16),
            pltpu.VMEM((2, tq, N_META), BF16),
        ],
        compiler_params=pltpu.CompilerParams(
            dimension_semantics=("parallel", "parallel"), vmem_limit_bytes=VMEM_LIMIT),
        name="attn_prompt",
    )(lam, q, k_all, v_all, diag, meta_b, gs)


def _attn_sample_kernel(lam_ref, q_ref, kn_ref, vn_ref, ck_ref, cv_ref, tail_ref, gs_ref,
                        o_ref, qd_ref, s_ref, w_ref):
    t = q_ref.shape[0]
    n_cache = ck_ref.shape[1] // HEADS
    split = n_cache - SAMPLE_TAIL
    width = n_cache + t
    lam = lam_ref[0, 0]
    hd = HEAD_DIM

    def gather(c_ref, n_ref, h):
        far = pl.ds(h, split, stride=HEADS)
        near = pl.ds(split * HEADS + h, SAMPLE_TAIL, stride=HEADS)
        new = pl.ds(h, t, stride=HEADS)
        return (c_ref[0, far, :].astype(BF16),
                jnp.concatenate([c_ref[0, near, :].astype(BF16), n_ref[new, :].astype(BF16)], axis=0))

    def pair_rows(c_ref, n_ref, hp):
        fa, ta = gather(c_ref, n_ref, 2 * hp)
        fb, tb = gather(c_ref, n_ref, 2 * hp + 1)
        return jnp.concatenate([fa, fb], axis=1), jnp.concatenate([ta, tb], axis=1)

    def score_steps(hp):
        slot = hp % 2

        def queries():
            zero = jnp.zeros((2 * t, hd), BF16)
            qa = _stack_maps(q_ref[:, 2 * hp * hd:(2 * hp + 1) * hd])
            qb = _stack_maps(q_ref[:, (2 * hp + 1) * hd:(2 * hp + 2) * hd])
            qd_ref[slot] = jnp.concatenate(
                [jnp.concatenate([qa, zero], axis=1), jnp.concatenate([zero, qb], axis=1)], axis=0)

        def scores():
            k_far, k_tail = pair_rows(ck_ref, kn_ref, hp)
            s_ref[slot, :, 0:split] = _dot_nt(qd_ref[slot], k_far)
            ba, bb = tail_ref[2 * hp], tail_ref[2 * hp + 1]
            s_ref[slot, :, split:width] = (
                _dot_nt(qd_ref[slot], k_tail) + jnp.concatenate([ba, ba, bb, bb], axis=0))

        return [queries, scores]

    def softmax_step(hp, j):
        slot = hp % 2
        r, p = [], []
        for rows in (slice(2 * t * j, 2 * t * j + t), slice(2 * t * j + t, 2 * t * (j + 1))):
            m = jnp.max(s_ref[slot, rows, 0:width], axis=-1, keepdims=True)
            p.append(jnp.exp2(s_ref[slot, rows, 0:width] - m))
            r.append(1.0 / jnp.sum(p[-1], axis=-1, keepdims=True))
        w_ref[slot, t * j:t * (j + 1), 0:width] = (p[0] * r[0] - p[1] * (lam * r[1])).astype(BF16)

    def value_step(hp):
        slot = hp % 2
        v_far, v_tail = pair_rows(cv_ref, vn_ref, hp)
        o = _dot(w_ref[slot, :, 0:split], v_far) + _dot(w_ref[slot, :, split:width], v_tail)
        gs = gs_ref[...]
        o_ref[:, 2 * hp * hd:(2 * hp + 1) * hd] = _finish_head(o[0:t, 0:hd], gs)
        o_ref[:, (2 * hp + 1) * hd:(2 * hp + 2) * hd] = _finish_head(o[t:2 * t, hd:2 * hd], gs)

    pairs = HEADS // 2
    for step in score_steps(0):
        step()
    for hp in range(pairs):
        ahead = score_steps(hp + 1) if hp + 1 < pairs else [lambda: None, lambda: None]
        softmax_step(hp, 0)
        ahead[0]()
        ahead[1]()
        softmax_step(hp, 1)
        value_step(hp)


def _attn_sample(q, k_new, v_new, cache_k, cache_v, tail_b, lam, gs, batch, t):
    n_cache = cache_k.shape[1] // HEADS
    row = pl.BlockSpec((t, ATTN_WIDTH), lambda b: (b, 0))
    new = pl.BlockSpec((t * HEADS, HEAD_DIM), lambda b: (b, 0))
    cache = pl.BlockSpec((1,) + cache_k.shape[1:], lambda b: (b, 0, 0))
    width = -(-(n_cache + t) // HEAD_DIM) * HEAD_DIM
    return pl.pallas_call(
        _attn_sample_kernel,
        grid=(batch,),
        in_specs=[_smem(), row, new, new, cache, cache,
                  _resident(tail_b.shape), _resident((1, HEAD_DIM))],
        out_specs=row,
        out_shape=jax.ShapeDtypeStruct((batch * t, ATTN_WIDTH), BF16),
        scratch_shapes=[
            pltpu.VMEM((2, 4 * t, 2 * HEAD_DIM), BF16),
            pltpu.VMEM((2, 4 * t, width), F32),
            pltpu.VMEM((2, 2 * t, width), BF16),
        ],
        compiler_params=pltpu.CompilerParams(
            dimension_semantics=("parallel",), vmem_limit_bytes=VMEM_LIMIT),
        name="attn_sample",
    )(lam, q, k_new, v_new, cache_k, cache_v, tail_b, gs)


def _pool_mix(ext, wp_ref, scale):
    outs = []
    for g, w in enumerate(POOL_WINDOWS):
        xg = ext[:, g * POOL_GROUP_DIM:(g + 1) * POOL_GROUP_DIM]
        s = xg
        span = 1
        while span < w:
            s = s + pltpu.roll(s, span, axis=0)
            span *= 2
        p = (s[N_META:] * (1.0 / w) - xg[N_META:]).astype(BF16)
        outs.append(_dot(p, wp_ref[g]))
    return jnp.concatenate(outs, axis=-1) * scale


def _mix_out(a, b_ref, x_ref, wo_ref, o_ref):
    mix = jnp.concatenate([a.astype(BF16), b_ref[...]], axis=-1)
    o_ref[...] = x_ref[...] + _dot(mix, wo_ref[...])


def _out_prompt_kernel(u_ref, b_ref, x_ref, wp_ref, ps_ref, wo_ref, o_ref):
    _mix_out(_pool_mix(u_ref[...], wp_ref, ps_ref[...]), b_ref, x_ref, wo_ref, o_ref)


def _out_sample_kernel(u_ref, state_ref, b_ref, x_ref, wp_ref, ps_ref, wo_ref, o_ref):
    n_seq, t = state_ref.shape[0], u_ref.shape[0] // state_ref.shape[0]
    parts = []
    for s in range(n_seq):
        ext = jnp.concatenate([state_ref[s], u_ref[s * t:(s + 1) * t, :]], axis=0)
        parts.append(_pool_mix(ext, wp_ref, ps_ref[...]))
    _mix_out(jnp.concatenate(parts, axis=0), b_ref, x_ref, wo_ref, o_ref)


def _out_common_specs(rows, idx):
    return (pl.BlockSpec((rows, ATTN_WIDTH), idx), pl.BlockSpec((rows, D_MODEL), idx))


def _out_prompt(u, b, x, wp, ps, wo, batch, seq):
    rows = PROJ_ROWS
    nt = seq // rows
    idx = lambda bi, i: (bi * nt + i, 0)
    u_spec = pl.BlockSpec(
        (pl.Squeezed(), pl.Element(N_META + rows), pl.Element(POOL_WIDTH)),
        lambda bi, i: (bi, i * rows, 0))
    b_spec, x_spec = _out_common_specs(rows, idx)
    return pl.pallas_call(
        _out_prompt_kernel,
        grid=(batch, nt),
        in_specs=[u_spec, b_spec, x_spec,
                  _resident(wp.shape), _resident(ps.shape), _resident(wo.shape)],
        out_specs=x_spec,
        out_shape=jax.ShapeDtypeStruct(x.shape, F32),
        compiler_params=pltpu.CompilerParams(
            dimension_semantics=("parallel", "parallel"), vmem_limit_bytes=VMEM_LIMIT),
        name="out_proj_prompt",
    )(u, b, x, wp, ps, wo)


def _out_sample(u, state16, b, x, wp, ps, wo, t):
    rows = PROJ_ROWS
    n_seq = rows // t
    idx = lambda i: (i, 0)
    b_spec, x_spec = _out_common_specs(rows, idx)
    return pl.pallas_call(
        _out_sample_kernel,
        grid=(x.shape[0] // rows,),
        in_specs=[
            pl.BlockSpec((rows, POOL_WIDTH), idx),
            pl.BlockSpec((n_seq, N_META, POOL_WIDTH), lambda i: (i, 0, 0)),
            b_spec, x_spec,
            _resident(wp.shape), _resident(ps.shape), _resident(wo.shape),
        ],
        out_specs=x_spec,
        out_shape=jax.ShapeDtypeStruct(x.shape, F32),
        compiler_params=pltpu.CompilerParams(
            dimension_semantics=("parallel",), vmem_limit_bytes=VMEM_LIMIT),
        name="out_proj_sample",
    )(u, state16, b, x, wp, ps, wo)


def _ffn_kernel(h_ref, gf_ref, wg_ref, wu_ref, wd_ref, gfin_ref, y_ref, *rest):
    n_ref = rest[-1]
    f = pl.program_id(1)

    @pl.when(f == 0)
    def _():
        h = h_ref[...]
        n_ref[...] = _rms(h, gf_ref[...], EPS).astype(BF16)
        y_ref[...] = h

    weights = [wg_ref[...], wu_ref[...], wd_ref[...]]
    if len(rest) > 1:
        weights = [w.astype(BF16) for w in weights]
        for out_ref, w in zip(rest[:3], weights):
            out_ref[...] = w
    wg, wu, wd = weights
    n = n_ref[...]
    g = _dot(n, wg)
    u = _dot(n, wu)
    act = (g * (1.0 / (1.0 + jnp.exp(-g))) * u).astype(BF16)
    y_ref[...] += _dot(act, wd)

    @pl.when(f == pl.num_programs(1) - 1)
    def _():
        y_ref[...] = _rms(y_ref[...], gfin_ref[...], EPS)


def _ffn(h, gf, wg, wu, wd, gfin, cols):
    rows = FFN_ROWS
    emit = wg.dtype == F32
    row = pl.BlockSpec((rows, D_MODEL), lambda i, f: (i, 0))
    vec = pl.BlockSpec((1, D_MODEL), lambda i, f: (0, 0))
    w_in = pl.BlockSpec((D_MODEL, cols), lambda i, f: (0, f))
    w_out = pl.BlockSpec((cols, D_MODEL), lambda i, f: (f, 0))
    out_specs, out_shape = [row], [jax.ShapeDtypeStruct(h.shape, F32)]
    if emit:
        out_specs += [w_in, w_in, w_out]
        out_shape += [jax.ShapeDtypeStruct(w.shape, BF16) for w in (wg, wu, wd)]
    outs = pl.pallas_call(
        _ffn_kernel,
        grid=(h.shape[0] // rows, D_FF // cols),
        in_specs=[row, vec, w_in, w_in, w_out, vec],
        out_specs=out_specs,
        out_shape=out_shape,
        scratch_shapes=[pltpu.VMEM((rows, D_MODEL), BF16)],
        compiler_params=pltpu.CompilerParams(
            dimension_semantics=("arbitrary" if emit else "parallel", "arbitrary"),
            vmem_limit_bytes=VMEM_LIMIT),
        name="swiglu_f32w" if emit else "swiglu",
    )(h, gf, wg, wu, wd, gfin)
    return outs if emit else outs[0]


def kernel(x_prompt, x_sample, cache_k, cache_v, state_pool, meta, g_mix, w_in, w_pool,
           pool_scale, lambda_q1, lambda_k1, lambda_q2, lambda_k2, g_subln, w_out, g_ffn,
           w_gate, w_up, w_down, rel_bias, g_final):
    batch, seq, d = x_prompt.shape
    dec_batch, t, _ = x_sample.shape
    depth = w_in.shape[0]
    assert depth == 1 and d == D_MODEL and meta.shape == (N_META, D_MODEL)
    assert seq % PROJ_ROWS == 0 and seq % Q_TILE == 0 and PROJ_ROWS % t == 0
    assert (dec_batch * t) % PROJ_ROWS == 0 and t >= POOL_STATE
    assert (N_META + seq) % (PROMPT_TILES * 16) == 0
    prompt_rows = (N_META + seq) // PROMPT_TILES
    n_cache = cache_k.shape[2]
    assert (n_cache - N_META) % CHUNK == 0 and n_cache > SAMPLE_TAIL

    w_in_b = w_in[0].astype(BF16)
    w_pool_b = w_pool[0].astype(BF16)
    w_out_b = w_out[0].astype(BF16)
    g_mix2 = g_mix[0].reshape(1, D_MODEL)
    g_ffn2 = g_ffn[0].reshape(1, D_MODEL)
    g_fin2 = g_final.reshape(1, D_MODEL)
    g_sub2 = g_subln[0].reshape(1, HEAD_DIM)
    ps2 = pool_scale[0].reshape(1, POOL_WIDTH)

    diag_b, meta_b, tail_b, lam = _bias_tables(
        rel_bias, lambda_q1, lambda_k1, lambda_q2, lambda_k2, t)

    xp = x_prompt.reshape(batch * seq, D_MODEL)
    xs = x_sample.reshape(dec_batch * t, D_MODEL)
    u_p, q_p, k_all, v_all = _project_prompt(x_prompt, meta, g_mix2, w_in_b, prompt_rows)
    u_s, q_s, k_s, v_s = _project_rows(xs, g_mix2, w_in_b, PROJ_ROWS)

    b_p = _attn_prompt(q_p, k_all, v_all, diag_b, meta_b, lam, g_sub2, batch, seq)
    b_s = _attn_sample(q_s, k_s, v_s,
                       cache_k[0].reshape(dec_batch, n_cache * HEADS, HEAD_DIM),
                       cache_v[0].reshape(dec_batch, n_cache * HEADS, HEAD_DIM),
                       tail_b, lam, g_sub2, dec_batch, t)

    h_p = _out_prompt(u_p, b_p, xp, w_pool_b, ps2, w_out_b, batch, seq)
    state16 = jnp.pad(state_pool[0], ((0, 0), (N_META - POOL_STATE, 0), (0, 0)))
    h_s = _out_sample(u_s, state16, b_s, xs, w_pool_b, ps2, w_out_b, t)

    y_s, wg_b, wu_b, wd_b = _ffn(h_s, g_ffn2, w_gate[0], w_up[0], w_down[0], g_fin2, FFN_COLS_F32)
    y_p = _ffn(h_p, g_ffn2, wg_b, wu_b, wd_b, g_fin2, FFN_COLS)

    hd = (HEADS, HEAD_DIM)
    return (
        y_p.reshape(batch, seq, D_MODEL),
        y_s.reshape(dec_batch, t, D_MODEL),
        k_all.reshape(1, batch, N_META + seq, *hd),
        v_all.reshape(1, batch, N_META + seq, *hd),
        u_p[None, :, N_META + seq - POOL_STATE:],
        k_s.reshape(1, dec_batch, t, *hd),
        v_s.reshape(1, dec_batch, t, *hd),
        u_s.reshape(dec_batch, t, POOL_WIDTH)[None, :, t - POOL_STATE:],
    )
```

```python
import functools
import math

import jax
import jax.numpy as jnp
from jax import lax
from jax.experimental import pallas as pl
from jax.experimental.pallas import tpu as pltpu

F32 = jnp.float32
BF16 = jnp.bfloat16

D_MODEL = 2048
CHUNK = 64
N_META = 16
POOL_WINDOWS = (2, 4, 8, 16)
POOL_GROUP_DIM = 256
POOL_WIDTH = 1024
POOL_STATE = 15
HEADS = 8
HEAD_DIM = 128
HALF_DIM = 64
ATTN_WIDTH = 1024
ATT_SCALE = HALF_DIM ** -0.5
LOG2E = math.log2(math.e)
D_FF = 5632
EPS = 1e-6
SUBLN_EPS = 1e-5
NEG_INF = -1e30
LAMBDA_INIT = 0.8 - 0.6 * math.exp(-0.3 * 0)

_BUCKET_THRESHOLDS = (1, 2, 3, 4, 5, 6, 7, 8, 12, 16, 23, 32, 46, 64, 91)

Q_TILE = 256
SOFTMAX_ROWS = 32
PROJ_ROWS = 512
PROMPT_TILES = 3
FFN_ROWS = 1024
FFN_COLS = 512
FFN_COLS_F32 = 256
SAMPLE_TAIL = 272
VMEM_LIMIT = 56 * 1024 * 1024


def _rms(x, g, eps):
    ms = jnp.mean(x * x, axis=-1, keepdims=True)
    return x * lax.rsqrt(ms + eps) * g


def _dot(a, b):
    return jnp.dot(a, b, preferred_element_type=F32)


def _dot_nt(a, b):
    return lax.dot_general(a, b, (((1,), (1,)), ((), ())), preferred_element_type=F32)


def _resident(shape):
    nd = len(shape)
    return pl.BlockSpec(shape, lambda *_: (0,) * nd, pipeline_mode=pl.Buffered(1))


def _smem():
    return pl.BlockSpec(memory_space=pltpu.SMEM)


def _bias_of_rel(rel, table_ref, h):
    n = jnp.abs(rel)
    far = table_ref[len(_BUCKET_THRESHOLDS), h]

    def entry(b):
        return (table_ref[b, h] - far) * LOG2E

    neg = jnp.full(rel.shape, entry(0), F32)
    pos = jnp.full(rel.shape, entry(16), F32)
    for b, thr in enumerate(_BUCKET_THRESHOLDS, start=1):
        ge = n >= thr
        neg = jnp.where(ge, entry(b), neg)
        pos = jnp.where(ge, entry(16 + b), pos)
    return jnp.where(rel > 0, pos, neg)


def _bias_kernel(table_ref, lq1_ref, lk1_ref, lq2_ref, lk2_ref,
                 diag_ref, meta_ref, tail_ref, lam_ref):
    h = pl.program_id(0)
    t = Q_TILE
    qi = lax.broadcasted_iota(jnp.int32, (t, t), 0)
    kj = lax.broadcasted_iota(jnp.int32, (t, t), 1)
    visible = (kj // CHUNK) <= (qi // CHUNK)
    diag_ref[0, 0] = jnp.where(visible, _bias_of_rel(kj - qi, table_ref, h), NEG_INF)
    diag_ref[0, 1] = _bias_of_rel(kj - t - qi, table_ref, h)
    qm = lax.broadcasted_iota(jnp.int32, (t, N_META), 0)
    km = lax.broadcasted_iota(jnp.int32, (t, N_META), 1)
    meta_ref[0] = _bias_of_rel(km - N_META - qm, table_ref, h)
    qs = lax.broadcasted_iota(jnp.int32, tail_ref.shape[1:], 0)
    ks = lax.broadcasted_iota(jnp.int32, tail_ref.shape[1:], 1)
    tail_ref[0] = _bias_of_rel(ks - SAMPLE_TAIL - qs, table_ref, h)
    s1 = jnp.sum(lq1_ref[...] * lk1_ref[...], axis=-1, keepdims=True)
    s2 = jnp.sum(lq2_ref[...] * lk2_ref[...], axis=-1, keepdims=True)
    lam_ref[...] = jnp.exp(s1) - jnp.exp(s2) + LAMBDA_INIT


def _bias_tables(rel_bias, lq1, lk1, lq2, lk2, n_new):
    tail_w = SAMPLE_TAIL + n_new
    vec = pl.BlockSpec((1, HALF_DIM), lambda h: (0, 0))
    return pl.pallas_call(
        _bias_kernel,
        grid=(HEADS,),
        in_specs=[_smem(), vec, vec, vec, vec],
        out_specs=[
            pl.BlockSpec((1, 2, Q_TILE, Q_TILE), lambda h: (h, 0, 0, 0)),
            pl.BlockSpec((1, Q_TILE, N_META), lambda h: (h, 0, 0)),
            pl.BlockSpec((1, n_new, tail_w), lambda h: (h, 0, 0)),
            pl.BlockSpec((1, 1), lambda h: (0, 0)),
        ],
        out_shape=[
            jax.ShapeDtypeStruct((HEADS, 2, Q_TILE, Q_TILE), F32),
            jax.ShapeDtypeStruct((HEADS, Q_TILE, N_META), F32),
            jax.ShapeDtypeStruct((HEADS, n_new, tail_w), F32),
            jax.ShapeDtypeStruct((1, 1), F32),
        ],
        name="bias_tables",
    )(rel_bias, lq1, lk1, lq2, lk2)


def _store_heads(o_ref, x):
    rows = x.shape[0]
    for h in range(HEADS):
        o_ref[pl.ds(h, rows, stride=HEADS), :] = x[:, h * HEAD_DIM:(h + 1) * HEAD_DIM]


def _project_store(xn, w_ref, u_ref, q_ref, k_ref, v_ref):
    w = POOL_WIDTH
    u_ref[...] = _dot(xn, w_ref[:, 0 * w:1 * w])
    q_ref[...] = (_dot(xn, w_ref[:, 1 * w:2 * w]) * (ATT_SCALE * LOG2E)).astype(BF16)
    _store_heads(k_ref, _dot(xn, w_ref[:, 2 * w:3 * w]))
    _store_heads(v_ref, _dot(xn, w_ref[:, 3 * w:4 * w]))


def _proj_rows_kernel(x_ref, g_ref, w_ref, u_ref, q_ref, k_ref, v_ref):
    xn = _rms(x_ref[...], g_ref[...], EPS).astype(BF16)
    _project_store(xn, w_ref, u_ref, q_ref, k_ref, v_ref)


def _proj_prompt_kernel(x_ref, meta_ref, g_ref, w_ref, u_ref, q_ref, k_ref, v_ref, xn_ref, carry_ref):
    i = pl.program_id(1)
    g = g_ref[...]
    rows = xn_ref.shape[0]

    @pl.when(i == 0)
    def _():
        xn_ref[0:N_META, :] = _rms(meta_ref[...], g, EPS).astype(BF16)

    @pl.when(i > 0)
    def _():
        xn_ref[0:N_META, :] = carry_ref[...]

    xn_ref[N_META:rows, :] = _rms(x_ref[0:rows - N_META, :], g, EPS).astype(BF16)

    @pl.when(i < pl.num_programs(1) - 1)
    def _():
        carry_ref[...] = _rms(x_ref[rows - N_META:rows, :], g, EPS).astype(BF16)

    _project_store(xn_ref[...], w_ref, u_ref, q_ref, k_ref, v_ref)


def _proj_out_shapes(lead, n):
    return [jax.ShapeDtypeStruct((*lead, n, POOL_WIDTH), F32),
            jax.ShapeDtypeStruct((*lead, n, POOL_WIDTH), BF16),
            jax.ShapeDtypeStruct((*lead, n * HEADS, HEAD_DIM), F32),
            jax.ShapeDtypeStruct((*lead, n * HEADS, HEAD_DIM), F32)]


def _project_rows(x, g, w_bf16, rows):
    n = x.shape[0]
    uq = pl.BlockSpec((rows, POOL_WIDTH), lambda i: (i, 0))
    kv = pl.BlockSpec((rows * HEADS, HEAD_DIM), lambda i: (i, 0))
    return pl.pallas_call(
        _proj_rows_kernel,
        grid=(n // rows,),
        in_specs=[pl.BlockSpec((rows, D_MODEL), lambda i: (i, 0)),
                  _resident((1, D_MODEL)), _resident(w_bf16.shape)],
        out_specs=[uq, uq, kv, kv],
        out_shape=_proj_out_shapes((), n),
        compiler_params=pltpu.CompilerParams(
            dimension_semantics=("parallel",), vmem_limit_bytes=VMEM_LIMIT),
        name="in_proj_rows",
    )(x, g, w_bf16)


def _project_prompt(x, meta, g, w_bf16, rows):
    batch, seq, _ = x.shape
    total = N_META + seq
    x_spec = pl.BlockSpec((None, rows, D_MODEL), lambda b, i: (b, i, 0))
    uq = pl.BlockSpec((None, rows, POOL_WIDTH), lambda b, i: (b, i, 0))
    kv = pl.BlockSpec((None, rows * HEADS, HEAD_DIM), lambda b, i: (b, i, 0))
    return pl.pallas_call(
        _proj_prompt_kernel,
        grid=(batch, total // rows),
        in_specs=[x_spec, _resident(meta.shape), _resident((1, D_MODEL)), _resident(w_bf16.shape)],
        out_specs=[uq, uq, kv, kv],
        out_shape=_proj_out_shapes((batch,), total),
        scratch_shapes=[pltpu.VMEM((rows, D_MODEL), BF16), pltpu.VMEM((N_META, D_MODEL), BF16)],
        compiler_params=pltpu.CompilerParams(
            dimension_semantics=("parallel", "arbitrary"), vmem_limit_bytes=VMEM_LIMIT),
        name="in_proj_prompt",
    )(x, meta, g, w_bf16)


def _stack_maps(q):
    lane = lax.broadcasted_iota(jnp.int32, q.shape, 1)
    zero = jnp.zeros_like(q)
    return jnp.concatenate(
        [jnp.where(lane < HALF_DIM, q, zero), jnp.where(lane >= HALF_DIM, q, zero)], axis=0)


def _finish_head(o, gs):
    return (_rms(o, gs, SUBLN_EPS) * (1.0 - LAMBDA_INIT)).astype(BF16)


def _attn_prompt_kernel(lam_ref, q_ref, k_ref, v_ref, diag_ref, meta_ref, gs_ref,
                        o_ref, kb_ref, vb_ref, q2_ref, s_ref, sm_ref, w_ref, wm_ref, rl_ref):
    tq = Q_TILE
    nt = o_ref.shape[0] // tq
    rb = SOFTMAX_ROWS
    lam = lam_ref[0, 0]
    head_rows = pl.ds(pl.program_id(1), kb_ref.shape[0], stride=HEADS)
    kb_ref[...] = k_ref[0, head_rows, :].astype(BF16)
    vb_ref[...] = v_ref[0, head_rows, :].astype(BF16)

    def score_steps(c):
        slot = c % 2

        def stack():
            q2_ref[slot] = _stack_maps(q_ref[N_META + c * tq:N_META + (c + 1) * tq, :])

        def frames(j):
            d = _dot_nt(q2_ref[slot], kb_ref[N_META + j * tq:N_META + (j + 1) * tq, :])
            cols = slice(j * tq, (j + 1) * tq)
            if j >= c - 1:
                bias = diag_ref[0, c - j]
                s_ref[slot, :tq, cols] = d[:tq] + bias
                s_ref[slot, tq:, cols] = d[tq:] + bias
            else:
                s_ref[slot, :, cols] = d

        def metas():
            sm = _dot_nt(q2_ref[slot], kb_ref[0:N_META, :])
            if c == 0:
                sm = sm + jnp.concatenate([meta_ref[0], meta_ref[0]], axis=0)
            sm_ref[slot] = sm

        return [stack, metas] + [functools.partial(frames, j) for j in range(c + 1)]

    def softmax_step(c, i):
        slot = c % 2
        n = (c + 1) * tq
        last_chunk = ((i + 1) * rb - 1) // CHUNK
        seen = c * tq + -(-(last_chunk + 1) * CHUNK // HEAD_DIM) * HEAD_DIM
        rows = (slice(i * rb, (i + 1) * rb), slice(tq + i * rb, tq + (i + 1) * rb))
        p, pm, l = [], [], []
        for rmap in rows:
            sm = sm_ref[slot, rmap, :]
            m = jnp.maximum(jnp.max(s_ref[slot, rmap, :seen], axis=-1, keepdims=True),
                            jnp.max(sm, axis=-1, keepdims=True))
            p.append(jnp.exp2(s_ref[slot, rmap, :seen] - m))
            pm.append(jnp.exp2(sm - m))
            l.append(jnp.sum(p[-1], axis=-1, keepdims=True) + jnp.sum(pm[-1], axis=-1, keepdims=True))
        rho = lam * l[0] / l[1]
        w_ref[slot, rows[0], :seen] = (p[0] - p[1] * rho).astype(BF16)
        if seen < n:
            w_ref[slot, rows[0], seen:n] = jnp.zeros((rb, n - seen), BF16)
        wm_ref[slot, rows[0], :] = (pm[0] - pm[1] * rho).astype(BF16)
        rl_ref[slot, rows[0], :] = 1.0 / l[0]

    def value_step(c):
        slot = c % 2
        n = (c + 1) * tq
        o = _dot(w_ref[slot, :, :n], vb_ref[N_META:N_META + n, :]) + _dot(wm_ref[slot], vb_ref[0:N_META, :])
        o_ref[c * tq:(c + 1) * tq, :] = _finish_head(o * rl_ref[slot], gs_ref[...])

    for step in score_steps(0):
        step()
    for c in range(nt):
        ahead = score_steps(c + 1) if c + 1 < nt else []
        blocks = tq // rb
        for i in range(blocks):
            softmax_step(c, i)
            for step in ahead[i * len(ahead) // blocks:(i + 1) * len(ahead) // blocks]:
                step()
        value_step(c)


def _attn_prompt(q, k_all, v_all, diag, meta_b, lam, gs, batch, seq):
    tq = Q_TILE
    kv = pl.BlockSpec((1, (N_META + seq) * HEADS, HEAD_DIM), lambda b, h: (b, 0, 0))
    qi = pl.BlockSpec((None, N_META + seq, HEAD_DIM), lambda b, h: (b, 0, h))
    qo = pl.BlockSpec((seq, HEAD_DIM), lambda b, h: (b, h))
    return pl.pallas_call(
        _attn_prompt_kernel,
        grid=(batch, HEADS),
        in_specs=[
            _smem(), qi, kv, kv,
            pl.BlockSpec((1, 2, tq, tq), lambda b, h: (h, 0, 0, 0)),
            pl.BlockSpec((1, tq, N_META), lambda b, h: (h, 0, 0)),
            pl.BlockSpec((1, HEAD_DIM), lambda b, h: (0, 0)),
        ],
        out_specs=qo,
        out_shape=jax.ShapeDtypeStruct((batch * seq, ATTN_WIDTH), BF16),
        scratch_shapes=[
            pltpu.VMEM((N_META + seq, HEAD_DIM), BF16),
            pltpu.VMEM((N_META + seq, HEAD_DIM), BF16),
            pltpu.VMEM((2, 2 * tq, HEAD_DIM), BF16),
            pltpu.VMEM((2, 2 * tq, seq), F32),
            pltpu.VMEM((2, 2 * tq, N_META), F32),
            pltpu.VMEM((2, tq, seq), BF16),
            pltpu.VMEM((2, tq, N_META), BF16),
            pltpu.VMEM((2, tq, 1), F32),
        ],
        compiler_params=pltpu.CompilerParams(
            dimension_semantics=("parallel", "parallel"), vmem_limit_bytes=VMEM_LIMIT),
        name="attn_prompt",
    )(lam, q, k_all, v_all, diag, meta_b, gs)


def _attn_sample_kernel(lam_ref, q_ref, kn_ref, vn_ref, ck_ref, cv_ref, tail_ref, gs_ref,
                        o_ref, qd_ref, s_ref, w_ref):
    t = q_ref.shape[0]
    n_cache = ck_ref.shape[1] // HEADS
    split = n_cache - SAMPLE_TAIL
    width = n_cache + t
    lam = lam_ref[0, 0]
    hd = HEAD_DIM

    def gather(c_ref, n_ref, h):
        far = pl.ds(h, split, stride=HEADS)
        near = pl.ds(split * HEADS + h, SAMPLE_TAIL, stride=HEADS)
        new = pl.ds(h, t, stride=HEADS)
        return (c_ref[0, far, :].astype(BF16),
                jnp.concatenate([c_ref[0, near, :].astype(BF16), n_ref[new, :].astype(BF16)], axis=0))

    def pair_rows(c_ref, n_ref, hp):
        fa, ta = gather(c_ref, n_ref, 2 * hp)
        fb, tb = gather(c_ref, n_ref, 2 * hp + 1)
        return jnp.concatenate([fa, fb], axis=1), jnp.concatenate([ta, tb], axis=1)

    def score_steps(hp):
        slot = hp % 2

        def queries():
            zero = jnp.zeros((2 * t, hd), BF16)
            qa = _stack_maps(q_ref[:, 2 * hp * hd:(2 * hp + 1) * hd])
            qb = _stack_maps(q_ref[:, (2 * hp + 1) * hd:(2 * hp + 2) * hd])
            qd_ref[slot] = jnp.concatenate(
                [jnp.concatenate([qa, zero], axis=1), jnp.concatenate([zero, qb], axis=1)], axis=0)

        def scores():
            k_far, k_tail = pair_rows(ck_ref, kn_ref, hp)
            s_ref[slot, :, 0:split] = _dot_nt(qd_ref[slot], k_far)
            ba, bb = tail_ref[2 * hp], tail_ref[2 * hp + 1]
            s_ref[slot, :, split:width] = (
                _dot_nt(qd_ref[slot], k_tail) + jnp.concatenate([ba, ba, bb, bb], axis=0))

        return [queries, scores]

    def softmax_step(hp, j):
        slot = hp % 2
        r, p = [], []
        for rows in (slice(2 * t * j, 2 * t * j + t), slice(2 * t * j + t, 2 * t * (j + 1))):
            m = jnp.max(s_ref[slot, rows, 0:width], axis=-1, keepdims=True)
            p.append(jnp.exp2(s_ref[slot, rows, 0:width] - m))
            r.append(1.0 / jnp.sum(p[-1], axis=-1, keepdims=True))
        w_ref[slot, t * j:t * (j + 1), 0:width] = (p[0] * r[0] - p[1] * (lam * r[1])).astype(BF16)

    def value_step(hp):
        slot = hp % 2
        v_far, v_tail = pair_rows(cv_ref, vn_ref, hp)
        o = _dot(w_ref[slot, :, 0:split], v_far) + _dot(w_ref[slot, :, split:width], v_tail)
        gs = gs_ref[...]
        o_ref[:, 2 * hp * hd:(2 * hp + 1) * hd] = _finish_head(o[0:t, 0:hd], gs)
        o_ref[:, (2 * hp + 1) * hd:(2 * hp + 2) * hd] = _finish_head(o[t:2 * t, hd:2 * hd], gs)

    pairs = HEADS // 2
    for step in score_steps(0):
        step()
    for hp in range(pairs):
        ahead = score_steps(hp + 1) if hp + 1 < pairs else [lambda: None, lambda: None]
        softmax_step(hp, 0)
        ahead[0]()
        ahead[1]()
        softmax_step(hp, 1)
        value_step(hp)


def _attn_sample(q, k_new, v_new, cache_k, cache_v, tail_b, lam, gs, batch, t):
    n_cache = cache_k.shape[1] // HEADS
    row = pl.BlockSpec((t, ATTN_WIDTH), lambda b: (b, 0))
    new = pl.BlockSpec((t * HEADS, HEAD_DIM), lambda b: (b, 0))
    cache = pl.BlockSpec((1,) + cache_k.shape[1:], lambda b: (b, 0, 0))
    width = -(-(n_cache + t) // HEAD_DIM) * HEAD_DIM
    return pl.pallas_call(
        _attn_sample_kernel,
        grid=(batch,),
        in_specs=[_smem(), row, new, new, cache, cache,
                  _resident(tail_b.shape), _resident((1, HEAD_DIM))],
        out_specs=row,
        out_shape=jax.ShapeDtypeStruct((batch * t, ATTN_WIDTH), BF16),
        scratch_shapes=[
            pltpu.VMEM((2, 4 * t, 2 * HEAD_DIM), BF16),
            pltpu.VMEM((2, 4 * t, width), F32),
            pltpu.VMEM((2, 2 * t, width), BF16),
        ],
        compiler_params=pltpu.CompilerParams(
            dimension_semantics=("parallel",), vmem_limit_bytes=VMEM_LIMIT),
        name="attn_sample",
    )(lam, q, k_new, v_new, cache_k, cache_v, tail_b, gs)


def _pool_mix(ext, wp_ref, scale):
    outs = []
    for g, w in enumerate(POOL_WINDOWS):
        xg = ext[:, g * POOL_GROUP_DIM:(g + 1) * POOL_GROUP_DIM]
        s = xg
        span = 1
        while span < w:
            s = s + pltpu.roll(s, span, axis=0)
            span *= 2
        p = (s[N_META:] * (1.0 / w) - xg[N_META:]).astype(BF16)
        outs.append(_dot(p, wp_ref[g]))
    return jnp.concatenate(outs, axis=-1) * scale


def _mix_out(a, b_ref, x_ref, wo_ref, o_ref):
    mix = jnp.concatenate([a.astype(BF16), b_ref[...]], axis=-1)
    o_ref[...] = x_ref[...] + _dot(mix, wo_ref[...])


def _out_prompt_kernel(u_ref, b_ref, x_ref, wp_ref, ps_ref, wo_ref, o_ref):
    _mix_out(_pool_mix(u_ref[...], wp_ref, ps_ref[...]), b_ref, x_ref, wo_ref, o_ref)


def _out_sample_kernel(u_ref, state_ref, b_ref, x_ref, wp_ref, ps_ref, wo_ref, o_ref):
    n_seq, t = state_ref.shape[0], u_ref.shape[0] // state_ref.shape[0]
    parts = []
    for s in range(n_seq):
        ext = jnp.concatenate([state_ref[s], u_ref[s * t:(s + 1) * t, :]], axis=0)
        parts.append(_pool_mix(ext, wp_ref, ps_ref[...]))
    _mix_out(jnp.concatenate(parts, axis=0), b_ref, x_ref, wo_ref, o_ref)


def _out_common_specs(rows, idx):
    return (pl.BlockSpec((rows, ATTN_WIDTH), idx), pl.BlockSpec((rows, D_MODEL), idx))


def _out_prompt(u, b, x, wp, ps, wo, batch, seq):
    rows = PROJ_ROWS
    nt = seq // rows
    idx = lambda bi, i: (bi * nt + i, 0)
    u_spec = pl.BlockSpec(
        (pl.Squeezed(), pl.Element(N_META + rows), pl.Element(POOL_WIDTH)),
        lambda bi, i: (bi, i * rows, 0))
    b_spec, x_spec = _out_common_specs(rows, idx)
    return pl.pallas_call(
        _out_prompt_kernel,
        grid=(batch, nt),
        in_specs=[u_spec, b_spec, x_spec,
                  _resident(wp.shape), _resident(ps.shape), _resident(wo.shape)],
        out_specs=x_spec,
        out_shape=jax.ShapeDtypeStruct(x.shape, F32),
        compiler_params=pltpu.CompilerParams(
            dimension_semantics=("parallel", "parallel"), vmem_limit_bytes=VMEM_LIMIT),
        name="out_proj_prompt",
    )(u, b, x, wp, ps, wo)


def _out_sample(u, state16, b, x, wp, ps, wo, t):
    rows = PROJ_ROWS
    n_seq = rows // t
    idx = lambda i: (i, 0)
    b_spec, x_spec = _out_common_specs(rows, idx)
    return pl.pallas_call(
        _out_sample_kernel,
        grid=(x.shape[0] // rows,),
        in_specs=[
            pl.BlockSpec((rows, POOL_WIDTH), idx),
            pl.BlockSpec((n_seq, N_META, POOL_WIDTH), lambda i: (i, 0, 0)),
            b_spec, x_spec,
            _resident(wp.shape), _resident(ps.shape), _resident(wo.shape),
        ],
        out_specs=x_spec,
        out_shape=jax.ShapeDtypeStruct(x.shape, F32),
        compiler_params=pltpu.CompilerParams(
            dimension_semantics=("parallel",), vmem_limit_bytes=VMEM_LIMIT),
        name="out_proj_sample",
    )(u, state16, b, x, wp, ps, wo)


def _ffn_kernel(h_ref, gf_ref, wg_ref, wu_ref, wd_ref, gfin_ref, y_ref, *rest):
    n_ref = rest[-1]
    f = pl.program_id(1)

    @pl.when(f == 0)
    def _():
        h = h_ref[...]
        n_ref[...] = _rms(h, gf_ref[...], EPS).astype(BF16)
        y_ref[...] = h

    weights = [wg_ref[...], wu_ref[...], wd_ref[...]]
    if len(rest) > 1:
        weights = [w.astype(BF16) for w in weights]
        for out_ref, w in zip(rest[:3], weights):
            out_ref[...] = w
    wg, wu, wd = weights
    n = n_ref[...]
    g = _dot(n, wg)
    u = _dot(n, wu)
    act = (g * (1.0 / (1.0 + jnp.exp(-g))) * u).astype(BF16)
    y_ref[...] += _dot(act, wd)

    @pl.when(f == pl.num_programs(1) - 1)
    def _():
        y_ref[...] = _rms(y_ref[...], gfin_ref[...], EPS)


def _ffn(h, gf, wg, wu, wd, gfin, cols):
    rows = FFN_ROWS
    emit = wg.dtype == F32
    row = pl.BlockSpec((rows, D_MODEL), lambda i, f: (i, 0))
    vec = pl.BlockSpec((1, D_MODEL), lambda i, f: (0, 0))
    w_in = pl.BlockSpec((D_MODEL, cols), lambda i, f: (0, f))
    w_out = pl.BlockSpec((cols, D_MODEL), lambda i, f: (f, 0))
    out_specs, out_shape = [row], [jax.ShapeDtypeStruct(h.shape, F32)]
    if emit:
        out_specs += [w_in, w_in, w_out]
        out_shape += [jax.ShapeDtypeStruct(w.shape, BF16) for w in (wg, wu, wd)]
    outs = pl.pallas_call(
        _ffn_kernel,
        grid=(h.shape[0] // rows, D_FF // cols),
        in_specs=[row, vec, w_in, w_in, w_out, vec],
        out_specs=out_specs,
        out_shape=out_shape,
        scratch_shapes=[pltpu.VMEM((rows, D_MODEL), BF16)],
        compiler_params=pltpu.CompilerParams(
            dimension_semantics=("arbitrary" if emit else "parallel", "arbitrary"),
            vmem_limit_bytes=VMEM_LIMIT),
        name="swiglu_f32w" if emit else "swiglu",
    )(h, gf, wg, wu, wd, gfin)
    return outs if emit else outs[0]


def kernel(x_prompt, x_sample, cache_k, cache_v, state_pool, meta, g_mix, w_in, w_pool,
           pool_scale, lambda_q1, lambda_k1, lambda_q2, lambda_k2, g_subln, w_out, g_ffn,
           w_gate, w_up, w_down, rel_bias, g_final):
    batch, seq, d = x_prompt.shape
    dec_batch, t, _ = x_sample.shape
    depth = w_in.shape[0]
    assert depth == 1 and d == D_MODEL and meta.shape == (N_META, D_MODEL)
    assert seq % PROJ_ROWS == 0 and seq % Q_TILE == 0 and PROJ_ROWS % t == 0
    assert (dec_batch * t) % PROJ_ROWS == 0 and t >= POOL_STATE
    assert (N_META + seq) % (PROMPT_TILES * 16) == 0
    prompt_rows = (N_META + seq) // PROMPT_TILES
    n_cache = cache_k.shape[2]
    assert (n_cache - N_META) % CHUNK == 0 and n_cache > SAMPLE_TAIL

    w_in_b = w_in[0].astype(BF16)
    w_pool_b = w_pool[0].astype(BF16)
    w_out_b = w_out[0].astype(BF16)
    g_mix2 = g_mix[0].reshape(1, D_MODEL)
    g_ffn2 = g_ffn[0].reshape(1, D_MODEL)
    g_fin2 = g_final.reshape(1, D_MODEL)
    g_sub2 = g_subln[0].reshape(1, HEAD_DIM)
    ps2 = pool_scale[0].reshape(1, POOL_WIDTH)

    diag_b, meta_b, tail_b, lam = _bias_tables(
        rel_bias, lambda_q1, lambda_k1, lambda_q2, lambda_k2, t)

    xp = x_prompt.reshape(batch * seq, D_MODEL)
    xs = x_sample.reshape(dec_batch * t, D_MODEL)
    u_p, q_p, k_all, v_all = _project_prompt(x_prompt, meta, g_mix2, w_in_b, prompt_rows)
    u_s, q_s, k_s, v_s = _project_rows(xs, g_mix2, w_in_b, PROJ_ROWS)

    b_p = _attn_prompt(q_p, k_all, v_all, diag_b, meta_b, lam, g_sub2, batch, seq)
    b_s = _attn_sample(q_s, k_s, v_s,
                       cache_k[0].reshape(dec_batch, n_cache * HEADS, HEAD_DIM),
                       cache_v[0].reshape(dec_batch, n_cache * HEADS, HEAD_DIM),
                       tail_b, lam, g_sub2, dec_batch, t)

    h_p = _out_prompt(u_p, b_p, xp, w_pool_b, ps2, w_out_b, batch, seq)
    state16 = jnp.pad(state_pool[0], ((0, 0), (N_META - POOL_STATE, 0), (0, 0)))
    h_s = _out_sample(u_s, state16, b_s, xs, w_pool_b, ps2, w_out_b, t)

    y_s, wg_b, wu_b, wd_b = _ffn(h_s, g_ffn2, w_gate[0], w_up[0], w_down[0], g_fin2, FFN_COLS_F32)
    y_p = _ffn(h_p, g_ffn2, wg_b, wu_b, wd_b, g_fin2, FFN_COLS)

    hd = (HEADS, HEAD_DIM)
    return (
        y_p.reshape(batch, seq, D_MODEL),
        y_s.reshape(dec_batch, t, D_MODEL),
        k_all.reshape(1, batch, N_META + seq, *hd),
        v_all.reshape(1, batch, N_META + seq, *hd),
        u_p[None, :, N_META + seq - POOL_STATE:],
        k_s.reshape(1, dec_batch, t, *hd),
        v_s.reshape(1, dec_batch, t, *hd),
        u_s.reshape(dec_batch, t, POOL_WIDTH)[None, :, t - POOL_STATE:],
    )
```

```python
import functools
import math

import jax
import jax.numpy as jnp
from jax import lax
from jax.experimental import pallas as pl
from jax.experimental.pallas import tpu as pltpu

F32 = jnp.float32
BF16 = jnp.bfloat16

D_MODEL = 2048
CHUNK = 64
N_META = 16
POOL_WINDOWS = (2, 4, 8, 16)
POOL_GROUP_DIM = 256
POOL_WIDTH = 1024
POOL_STATE = 15
HEADS = 8
HEAD_DIM = 128
HALF_DIM = 64
ATTN_WIDTH = 1024
ATT_SCALE = HALF_DIM ** -0.5
LOG2E = math.log2(math.e)
D_FF = 5632
EPS = 1e-6
SUBLN_EPS = 1e-5
NEG_INF = -1e30
LAMBDA_INIT = 0.8 - 0.6 * math.exp(-0.3 * 0)

_BUCKET_THRESHOLDS = (1, 2, 3, 4, 5, 6, 7, 8, 12, 16, 23, 32, 46, 64, 91)

Q_TILE = 256
SOFTMAX_ROWS = 32
PROJ_ROWS = 512
PROMPT_TILES = 3
FFN_ROWS = 1024
FFN_COLS = 512
FFN_COLS_F32 = 256
SAMPLE_TAIL = 272
VMEM_LIMIT = 56 * 1024 * 1024


def _rms(x, g, eps):
    ms = jnp.mean(x * x, axis=-1, keepdims=True)
    return x * lax.rsqrt(ms + eps) * g


def _dot(a, b):
    return jnp.dot(a, b, preferred_element_type=F32)


def _dot_nt(a, b):
    return lax.dot_general(a, b, (((1,), (1,)), ((), ())), preferred_element_type=F32)


def _resident(shape):
    nd = len(shape)
    return pl.BlockSpec(shape, lambda *_: (0,) * nd, pipeline_mode=pl.Buffered(1))


def _smem():
    return pl.BlockSpec(memory_space=pltpu.SMEM)


def _bias_of_rel(rel, table_ref, h):
    n = jnp.abs(rel)
    far = table_ref[len(_BUCKET_THRESHOLDS), h]

    def entry(b):
        return (table_ref[b, h] - far) * LOG2E

    neg = jnp.full(rel.shape, entry(0), F32)
    pos = jnp.full(rel.shape, entry(16), F32)
    for b, thr in enumerate(_BUCKET_THRESHOLDS, start=1):
        ge = n >= thr
        neg = jnp.where(ge, entry(b), neg)
        pos = jnp.where(ge, entry(16 + b), pos)
    return jnp.where(rel > 0, pos, neg)


def _bias_kernel(table_ref, lq1_ref, lk1_ref, lq2_ref, lk2_ref,
                 diag_ref, meta_ref, tail_ref, lam_ref):
    h = pl.program_id(0)
    t = Q_TILE
    qi = lax.broadcasted_iota(jnp.int32, (t, t), 0)
    kj = lax.broadcasted_iota(jnp.int32, (t, t), 1)
    visible = (kj // CHUNK) <= (qi // CHUNK)
    diag_ref[0, 0] = jnp.where(visible, _bias_of_rel(kj - qi, table_ref, h), NEG_INF)
    diag_ref[0, 1] = _bias_of_rel(kj - t - qi, table_ref, h)
    qm = lax.broadcasted_iota(jnp.int32, (t, N_META), 0)
    km = lax.broadcasted_iota(jnp.int32, (t, N_META), 1)
    meta_ref[0] = _bias_of_rel(km - N_META - qm, table_ref, h)
    qs = lax.broadcasted_iota(jnp.int32, tail_ref.shape[1:], 0)
    ks = lax.broadcasted_iota(jnp.int32, tail_ref.shape[1:], 1)
    tail_ref[0] = _bias_of_rel(ks - SAMPLE_TAIL - qs, table_ref, h)
    s1 = jnp.sum(lq1_ref[...] * lk1_ref[...], axis=-1, keepdims=True)
    s2 = jnp.sum(lq2_ref[...] * lk2_ref[...], axis=-1, keepdims=True)
    lam_ref[...] = jnp.exp(s1) - jnp.exp(s2) + LAMBDA_INIT


def _bias_tables(rel_bias, lq1, lk1, lq2, lk2, n_new):
    tail_w = SAMPLE_TAIL + n_new
    vec = pl.BlockSpec((1, HALF_DIM), lambda h: (0, 0))
    return pl.pallas_call(
        _bias_kernel,
        grid=(HEADS,),
        in_specs=[_smem(), vec, vec, vec, vec],
        out_specs=[
            pl.BlockSpec((1, 2, Q_TILE, Q_TILE), lambda h: (h, 0, 0, 0)),
            pl.BlockSpec((1, Q_TILE, N_META), lambda h: (h, 0, 0)),
            pl.BlockSpec((1, n_new, tail_w), lambda h: (h, 0, 0)),
            pl.BlockSpec((1, 1), lambda h: (0, 0)),
        ],
        out_shape=[
            jax.ShapeDtypeStruct((HEADS, 2, Q_TILE, Q_TILE), F32),
            jax.ShapeDtypeStruct((HEADS, Q_TILE, N_META), F32),
            jax.ShapeDtypeStruct((HEADS, n_new, tail_w), F32),
            jax.ShapeDtypeStruct((1, 1), F32),
        ],
        name="bias_tables",
    )(rel_bias, lq1, lk1, lq2, lk2)


def _store_heads(o_ref, x):
    rows = x.shape[0]
    for h in range(HEADS):
        o_ref[pl.ds(h, rows, stride=HEADS), :] = x[:, h * HEAD_DIM:(h + 1) * HEAD_DIM]


def _store_column(c, y, u_ref, q_ref, k_ref, v_ref):
    if c == 0:
        u_ref[...] = y
    elif c == 1:
        q_ref[...] = (y * (ATT_SCALE * LOG2E)).astype(BF16)
    else:
        _store_heads(k_ref if c == 2 else v_ref, y)


def _project_store(xn, w_ref, u_ref, q_ref, k_ref, v_ref):
    w = POOL_WIDTH
    for c in range(4):
        _store_column(c, _dot(xn, w_ref[:, c * w:(c + 1) * w]), u_ref, q_ref, k_ref, v_ref)


def _proj_rows_kernel(x_ref, g_ref, w_ref, u_ref, q_ref, k_ref, v_ref, wb_ref, xn_ref):
    j = pl.program_id(0)
    rows = x_ref.shape[0]
    tile = pl.ds(pl.multiple_of(pl.program_id(1) * rows, rows), rows)

    @pl.when(j == 0)
    def _():
        xn_ref[tile, :] = _rms(x_ref[...], g_ref[...], EPS).astype(BF16)

    wb = w_ref[...].astype(BF16)
    wb_ref[...] = wb
    y = _dot(xn_ref[tile, :], wb)
    for c in range(4):
        @pl.when(j == c)
        def _(c=c):
            _store_column(c, y, u_ref, q_ref, k_ref, v_ref)


def _proj_prompt_kernel(x_ref, meta_ref, g_ref, w_ref, u_ref, q_ref, k_ref, v_ref, xn_ref, carry_ref):
    i = pl.program_id(1)
    g = g_ref[...]
    rows = xn_ref.shape[0]

    @pl.when(i == 0)
    def _():
        xn_ref[0:N_META, :] = _rms(meta_ref[...], g, EPS).astype(BF16)

    @pl.when(i > 0)
    def _():
        xn_ref[0:N_META, :] = carry_ref[...]

    xn_ref[N_META:rows, :] = _rms(x_ref[0:rows - N_META, :], g, EPS).astype(BF16)

    @pl.when(i < pl.num_programs(1) - 1)
    def _():
        carry_ref[...] = _rms(x_ref[rows - N_META:rows, :], g, EPS).astype(BF16)

    _project_store(xn_ref[...], w_ref, u_ref, q_ref, k_ref, v_ref)


def _proj_out_shapes(lead, n):
    return [jax.ShapeDtypeStruct((*lead, n, POOL_WIDTH), F32),
            jax.ShapeDtypeStruct((*lead, n, POOL_WIDTH), BF16),
            jax.ShapeDtypeStruct((*lead, n * HEADS, HEAD_DIM), F32),
            jax.ShapeDtypeStruct((*lead, n * HEADS, HEAD_DIM), F32)]


def _project_rows(x, g, w_f32, rows):
    n = x.shape[0]
    w = POOL_WIDTH
    last = n // rows - 1

    def rows_of(c):
        return lambda j, i: (jnp.where(j < c, 0, jnp.where(j == c, i, last)), 0)

    uq = [pl.BlockSpec((rows, w), rows_of(c)) for c in (0, 1)]
    kv = [pl.BlockSpec((rows * HEADS, HEAD_DIM), rows_of(c)) for c in (2, 3)]
    w_spec = pl.BlockSpec((D_MODEL, w), lambda j, i: (0, j))
    return pl.pallas_call(
        _proj_rows_kernel,
        grid=(w_f32.shape[1] // w, n // rows),
        in_specs=[pl.BlockSpec((rows, D_MODEL), rows_of(0)), _resident((1, D_MODEL)), w_spec],
        out_specs=[*uq, *kv, w_spec],
        out_shape=_proj_out_shapes((), n) + [jax.ShapeDtypeStruct(w_f32.shape, BF16)],
        scratch_shapes=[pltpu.VMEM((n, D_MODEL), BF16)],
        compiler_params=pltpu.CompilerParams(
            dimension_semantics=("arbitrary", "arbitrary"), vmem_limit_bytes=VMEM_LIMIT),
        name="in_proj_rows",
    )(x, g, w_f32)


def _project_prompt(x, meta, g, w_bf16, rows):
    batch, seq, _ = x.shape
    total = N_META + seq
    x_spec = pl.BlockSpec((None, rows, D_MODEL), lambda b, i: (b, i, 0))
    uq = pl.BlockSpec((None, rows, POOL_WIDTH), lambda b, i: (b, i, 0))
    kv = pl.BlockSpec((None, rows * HEADS, HEAD_DIM), lambda b, i: (b, i, 0))
    return pl.pallas_call(
        _proj_prompt_kernel,
        grid=(batch, total // rows),
        in_specs=[x_spec, _resident(meta.shape), _resident((1, D_MODEL)), _resident(w_bf16.shape)],
        out_specs=[uq, uq, kv, kv],
        out_shape=_proj_out_shapes((batch,), total),
        scratch_shapes=[pltpu.VMEM((rows, D_MODEL), BF16), pltpu.VMEM((N_META, D_MODEL), BF16)],
        compiler_params=pltpu.CompilerParams(
            dimension_semantics=("parallel", "arbitrary"), vmem_limit_bytes=VMEM_LIMIT),
        name="in_proj_prompt",
    )(x, meta, g, w_bf16)


def _stack_maps(q):
    lane = lax.broadcasted_iota(jnp.int32, q.shape, 1)
    zero = jnp.zeros_like(q)
    return jnp.concatenate(
        [jnp.where(lane < HALF_DIM, q, zero), jnp.where(lane >= HALF_DIM, q, zero)], axis=0)


def _finish_head(o, gs):
    return (_rms(o, gs, SUBLN_EPS) * (1.0 - LAMBDA_INIT)).astype(BF16)


def _attn_prompt_kernel(lam_ref, q_ref, k_ref, v_ref, diag_ref, meta_ref, gs_ref,
                        o_ref, kb_ref, vb_ref, q2_ref, s_ref, sm_ref, w_ref, wm_ref, rl_ref):
    tq = Q_TILE
    nt = o_ref.shape[0] // tq
    rb = SOFTMAX_ROWS
    lam = lam_ref[0, 0]
    head_rows = pl.ds(pl.program_id(1), kb_ref.shape[0], stride=HEADS)
    kb_ref[...] = k_ref[0, head_rows, :].astype(BF16)
    vb_ref[...] = v_ref[0, head_rows, :].astype(BF16)

    def score_steps(c):
        slot = c % 2

        def stack():
            q2_ref[slot] = _stack_maps(q_ref[N_META + c * tq:N_META + (c + 1) * tq, :])

        def frames(j):
            d = _dot_nt(q2_ref[slot], kb_ref[N_META + j * tq:N_META + (j + 1) * tq, :])
            cols = slice(j * tq, (j + 1) * tq)
            if j >= c - 1:
                bias = diag_ref[0, c - j]
                s_ref[slot, :tq, cols] = d[:tq] + bias
                s_ref[slot, tq:, cols] = d[tq:] + bias
            else:
                s_ref[slot, :, cols] = d

        def metas():
            sm = _dot_nt(q2_ref[slot], kb_ref[0:N_META, :])
            if c == 0:
                sm = sm + jnp.concatenate([meta_ref[0], meta_ref[0]], axis=0)
            sm_ref[slot] = sm

        return [stack, metas] + [functools.partial(frames, j) for j in range(c + 1)]

    def softmax_step(c, i):
        slot = c % 2
        n = (c + 1) * tq
        last_chunk = ((i + 1) * rb - 1) // CHUNK
        seen = c * tq + -(-(last_chunk + 1) * CHUNK // HEAD_DIM) * HEAD_DIM
        rows = (slice(i * rb, (i + 1) * rb), slice(tq + i * rb, tq + (i + 1) * rb))
        p, pm, l = [], [], []
        for rmap in rows:
            sm = sm_ref[slot, rmap, :]
            m = jnp.maximum(jnp.max(s_ref[slot, rmap, :seen], axis=-1, keepdims=True),
                            jnp.max(sm, axis=-1, keepdims=True))
            p.append(jnp.exp2(s_ref[slot, rmap, :seen] - m))
            pm.append(jnp.exp2(sm - m))
            l.append(jnp.sum(p[-1], axis=-1, keepdims=True) + jnp.sum(pm[-1], axis=-1, keepdims=True))
        rho = lam * l[0] / l[1]
        w_ref[slot, rows[0], :seen] = (p[0] - p[1] * rho).astype(BF16)
        if seen < n:
            w_ref[slot, rows[0], seen:n] = jnp.zeros((rb, n - seen), BF16)
        wm_ref[slot, rows[0], :] = (pm[0] - pm[1] * rho).astype(BF16)
        rl_ref[slot, rows[0], :] = 1.0 / l[0]

    def value_step(c):
        slot = c % 2
        n = (c + 1) * tq
        o = _dot(w_ref[slot, :, :n], vb_ref[N_META:N_META + n, :]) + _dot(wm_ref[slot], vb_ref[0:N_META, :])
        o_ref[c * tq:(c + 1) * tq, :] = _finish_head(o * rl_ref[slot], gs_ref[...])

    for step in score_steps(0):
        step()
    for c in range(nt):
        ahead = score_steps(c + 1) if c + 1 < nt else []
        blocks = tq // rb
        for i in range(blocks):
            softmax_step(c, i)
            for step in ahead[i * len(ahead) // blocks:(i + 1) * len(ahead) // blocks]:
                step()
        value_step(c)


def _attn_prompt(q, k_all, v_all, diag, meta_b, lam, gs, batch, seq):
    tq = Q_TILE
    kv = pl.BlockSpec((1, (N_META + seq) * HEADS, HEAD_DIM), lambda b, h: (b, 0, 0))
    qi = pl.BlockSpec((None, N_META + seq, HEAD_DIM), lambda b, h: (b, 0, h))
    qo = pl.BlockSpec((seq, HEAD_DIM), lambda b, h: (b, h))
    return pl.pallas_call(
        _attn_prompt_kernel,
        grid=(batch, HEADS),
        in_specs=[
            _smem(), qi, kv, kv,
            pl.BlockSpec((1, 2, tq, tq), lambda b, h: (h, 0, 0, 0)),
            pl.BlockSpec((1, tq, N_META), lambda b, h: (h, 0, 0)),
            pl.BlockSpec((1, HEAD_DIM), lambda b, h: (0, 0)),
        ],
        out_specs=qo,
        out_shape=jax.ShapeDtypeStruct((batch * seq, ATTN_WIDTH), BF16),
        scratch_shapes=[
            pltpu.VMEM((N_META + seq, HEAD_DIM), BF16),
            pltpu.VMEM((N_META + seq, HEAD_DIM), BF16),
            pltpu.VMEM((2, 2 * tq, HEAD_DIM), BF16),
            pltpu.VMEM((2, 2 * tq, seq), F32),
            pltpu.VMEM((2, 2 * tq, N_META), F32),
            pltpu.VMEM((2, tq, seq), BF16),
            pltpu.VMEM((2, tq, N_META), BF16),
            pltpu.VMEM((2, tq, 1), F32),
        ],
        compiler_params=pltpu.CompilerParams(
            dimension_semantics=("parallel", "parallel"), vmem_limit_bytes=VMEM_LIMIT),
        name="attn_prompt",
    )(lam, q, k_all, v_all, diag, meta_b, gs)


def _attn_sample_kernel(lam_ref, q_ref, kn_ref, vn_ref, ck_ref, cv_ref, tail_ref, gs_ref,
                        o_ref, qd_ref, s_ref, w_ref):
    t = q_ref.shape[0]
    n_cache = ck_ref.shape[1] // HEADS
    split = n_cache - SAMPLE_TAIL
    width = n_cache + t
    lam = lam_ref[0, 0]
    hd = HEAD_DIM

    def gather(c_ref, n_ref, h):
        far = pl.ds(h, split, stride=HEADS)
        near = pl.ds(split * HEADS + h, SAMPLE_TAIL, stride=HEADS)
        new = pl.ds(h, t, stride=HEADS)
        return (c_ref[0, far, :].astype(BF16),
                jnp.concatenate([c_ref[0, near, :].astype(BF16), n_ref[new, :].astype(BF16)], axis=0))

    def pair_rows(c_ref, n_ref, hp):
        fa, ta = gather(c_ref, n_ref, 2 * hp)
        fb, tb = gather(c_ref, n_ref, 2 * hp + 1)
        return jnp.concatenate([fa, fb], axis=1), jnp.concatenate([ta, tb], axis=1)

    def score_steps(hp):
        slot = hp % 2

        def queries():
            zero = jnp.zeros((2 * t, hd), BF16)
            qa = _stack_maps(q_ref[:, 2 * hp * hd:(2 * hp + 1) * hd])
            qb = _stack_maps(q_ref[:, (2 * hp + 1) * hd:(2 * hp + 2) * hd])
            qd_ref[slot] = jnp.concatenate(
                [jnp.concatenate([qa, zero], axis=1), jnp.concatenate([zero, qb], axis=1)], axis=0)

        def scores():
            k_far, k_tail = pair_rows(ck_ref, kn_ref, hp)
            s_ref[slot, :, 0:split] = _dot_nt(qd_ref[slot], k_far)
            ba, bb = tail_ref[2 * hp], tail_ref[2 * hp + 1]
            s_ref[slot, :, split:width] = (
                _dot_nt(qd_ref[slot], k_tail) + jnp.concatenate([ba, ba, bb, bb], axis=0))

        return [queries, scores]

    def softmax_step(hp, j):
        slot = hp % 2
        r, p = [], []
        for rows in (slice(2 * t * j, 2 * t * j + t), slice(2 * t * j + t, 2 * t * (j + 1))):
            m = jnp.max(s_ref[slot, rows, 0:width], axis=-1, keepdims=True)
            p.append(jnp.exp2(s_ref[slot, rows, 0:width] - m))
            r.append(1.0 / jnp.sum(p[-1], axis=-1, keepdims=True))
        w_ref[slot, t * j:t * (j + 1), 0:width] = (p[0] * r[0] - p[1] * (lam * r[1])).astype(BF16)

    def value_step(hp):
        slot = hp % 2
        v_far, v_tail = pair_rows(cv_ref, vn_ref, hp)
        o = _dot(w_ref[slot, :, 0:split], v_far) + _dot(w_ref[slot, :, split:width], v_tail)
        gs = gs_ref[...]
        o_ref[:, 2 * hp * hd:(2 * hp + 1) * hd] = _finish_head(o[0:t, 0:hd], gs)
        o_ref[:, (2 * hp + 1) * hd:(2 * hp + 2) * hd] = _finish_head(o[t:2 * t, hd:2 * hd], gs)

    pairs = HEADS // 2
    for step in score_steps(0):
        step()
    for hp in range(pairs):
        ahead = score_steps(hp + 1) if hp + 1 < pairs else [lambda: None, lambda: None]
        softmax_step(hp, 0)
        ahead[0]()
        ahead[1]()
        softmax_step(hp, 1)
        value_step(hp)


def _attn_sample(q, k_new, v_new, cache_k, cache_v, tail_b, lam, gs, batch, t):
    n_cache = cache_k.shape[1] // HEADS
    row = pl.BlockSpec((t, ATTN_WIDTH), lambda b: (b, 0))
    new = pl.BlockSpec((t * HEADS, HEAD_DIM), lambda b: (b, 0))
    cache = pl.BlockSpec((1,) + cache_k.shape[1:], lambda b: (b, 0, 0))
    width = -(-(n_cache + t) // HEAD_DIM) * HEAD_DIM
    return pl.pallas_call(
        _attn_sample_kernel,
        grid=(batch,),
        in_specs=[_smem(), row, new, new, cache, cache,
                  _resident(tail_b.shape), _resident((1, HEAD_DIM))],
        out_specs=row,
        out_shape=jax.ShapeDtypeStruct((batch * t, ATTN_WIDTH), BF16),
        scratch_shapes=[
            pltpu.VMEM((2, 4 * t, 2 * HEAD_DIM), BF16),
            pltpu.VMEM((2, 4 * t, width), F32),
            pltpu.VMEM((2, 2 * t, width), BF16),
        ],
        compiler_params=pltpu.CompilerParams(
            dimension_semantics=("parallel",), vmem_limit_bytes=VMEM_LIMIT),
        name="attn_sample",
    )(lam, q, k_new, v_new, cache_k, cache_v, tail_b, gs)


def _pool_mix(ext, wp_ref, scale):
    outs = []
    for g, w in enumerate(POOL_WINDOWS):
        xg = ext[:, g * POOL_GROUP_DIM:(g + 1) * POOL_GROUP_DIM]
        s = xg
        span = 1
        while span < w:
            s = s + pltpu.roll(s, span, axis=0)
            span *= 2
        p = (s[N_META:] * (1.0 / w) - xg[N_META:]).astype(BF16)
        outs.append(_dot(p, wp_ref[g]))
    return jnp.concatenate(outs, axis=-1) * scale


def _mix_out(a, b_ref, x_ref, wo, o_ref):
    mix = jnp.concatenate([a.astype(BF16), b_ref[...]], axis=-1)
    o_ref[...] = x_ref[...] + _dot(mix, wo)


def _out_prompt_kernel(u_ref, b_ref, x_ref, wp_ref, ps_ref, wo_ref, o_ref):
    _mix_out(_pool_mix(u_ref[...], wp_ref, ps_ref[...]), b_ref, x_ref, wo_ref[...], o_ref)


def _out_sample_kernel(u_ref, state_ref, b_ref, x_ref, wp_ref, ps_ref, wo_ref, o_ref, wob_ref):
    n_seq, t = state_ref.shape[0], u_ref.shape[0] // state_ref.shape[0]
    parts = []
    for s in range(n_seq):
        ext = jnp.concatenate([state_ref[s], u_ref[s * t:(s + 1) * t, :]], axis=0)
        parts.append(_pool_mix(ext, wp_ref, ps_ref[...]))
    wo = wo_ref[...].astype(BF16)
    wob_ref[...] = wo
    _mix_out(jnp.concatenate(parts, axis=0), b_ref, x_ref, wo, o_ref)


def _out_common_specs(rows, idx):
    return (pl.BlockSpec((rows, ATTN_WIDTH), idx), pl.BlockSpec((rows, D_MODEL), idx))


def _out_prompt(u, b, x, wp, ps, wo, batch, seq):
    rows = PROJ_ROWS
    nt = seq // rows
    idx = lambda bi, i: (bi * nt + i, 0)
    u_spec = pl.BlockSpec(
        (pl.Squeezed(), pl.Element(N_META + rows), pl.Element(POOL_WIDTH)),
        lambda bi, i: (bi, i * rows, 0))
    b_spec, x_spec = _out_common_specs(rows, idx)
    return pl.pallas_call(
        _out_prompt_kernel,
        grid=(batch, nt),
        in_specs=[u_spec, b_spec, x_spec,
                  _resident(wp.shape), _resident(ps.shape), _resident(wo.shape)],
        out_specs=x_spec,
        out_shape=jax.ShapeDtypeStruct(x.shape, F32),
        compiler_params=pltpu.CompilerParams(
            dimension_semantics=("parallel", "parallel"), vmem_limit_bytes=VMEM_LIMIT),
        name="out_proj_prompt",
    )(u, b, x, wp, ps, wo)


def _out_sample(u, state16, b, x, wp, ps, wo, t):
    rows = PROJ_ROWS
    n_seq = rows // t
    idx = lambda i: (i, 0)
    b_spec, x_spec = _out_common_specs(rows, idx)
    return pl.pallas_call(
        _out_sample_kernel,
        grid=(x.shape[0] // rows,),
        in_specs=[
            pl.BlockSpec((rows, POOL_WIDTH), idx),
            pl.BlockSpec((n_seq, N_META, POOL_WIDTH), lambda i: (i, 0, 0)),
            b_spec, x_spec,
            _resident(wp.shape), _resident(ps.shape), _resident(wo.shape),
        ],
        out_specs=[x_spec, pl.BlockSpec(wo.shape, lambda i: (0, 0))],
        out_shape=[jax.ShapeDtypeStruct(x.shape, F32), jax.ShapeDtypeStruct(wo.shape, BF16)],
        compiler_params=pltpu.CompilerParams(
            dimension_semantics=("arbitrary",), vmem_limit_bytes=VMEM_LIMIT),
        name="out_proj_sample",
    )(u, state16, b, x, wp, ps, wo)


def _ffn_kernel(h_ref, gf_ref, wg_ref, wu_ref, wd_ref, gfin_ref, y_ref, *rest):
    n_ref = rest[-1]
    f = pl.program_id(1)

    @pl.when(f == 0)
    def _():
        h = h_ref[...]
        n_ref[...] = _rms(h, gf_ref[...], EPS).astype(BF16)
        y_ref[...] = h

    weights = [wg_ref[...], wu_ref[...], wd_ref[...]]
    if len(rest) > 1:
        weights = [w.astype(BF16) for w in weights]
        for out_ref, w in zip(rest[:3], weights):
            out_ref[...] = w
    wg, wu, wd = weights
    n = n_ref[...]
    g = _dot(n, wg)
    u = _dot(n, wu)
    act = (g * (1.0 / (1.0 + jnp.exp(-g))) * u).astype(BF16)
    y_ref[...] += _dot(act, wd)

    @pl.when(f == pl.num_programs(1) - 1)
    def _():
        y_ref[...] = _rms(y_ref[...], gfin_ref[...], EPS)


def _ffn(h, gf, wg, wu, wd, gfin, cols):
    rows = FFN_ROWS
    emit = wg.dtype == F32
    row = pl.BlockSpec((rows, D_MODEL), lambda i, f: (i, 0))
    vec = pl.BlockSpec((1, D_MODEL), lambda i, f: (0, 0))
    w_in = pl.BlockSpec((D_MODEL, cols), lambda i, f: (0, f))
    w_out = pl.BlockSpec((cols, D_MODEL), lambda i, f: (f, 0))
    out_specs, out_shape = [row], [jax.ShapeDtypeStruct(h.shape, F32)]
    if emit:
        out_specs += [w_in, w_in, w_out]
        out_shape += [jax.ShapeDtypeStruct(w.shape, BF16) for w in (wg, wu, wd)]
    outs = pl.pallas_call(
        _ffn_kernel,
        grid=(h.shape[0] // rows, D_FF // cols),
        in_specs=[row, vec, w_in, w_in, w_out, vec],
        out_specs=out_specs,
        out_shape=out_shape,
        scratch_shapes=[pltpu.VMEM((rows, D_MODEL), BF16)],
        compiler_params=pltpu.CompilerParams(
            dimension_semantics=("arbitrary" if emit else "parallel", "arbitrary"),
            vmem_limit_bytes=VMEM_LIMIT),
        name="swiglu_f32w" if emit else "swiglu",
    )(h, gf, wg, wu, wd, gfin)
    return outs if emit else outs[0]


def kernel(x_prompt, x_sample, cache_k, cache_v, state_pool, meta, g_mix, w_in, w_pool,
           pool_scale, lambda_q1, lambda_k1, lambda_q2, lambda_k2, g_subln, w_out, g_ffn,
           w_gate, w_up, w_down, rel_bias, g_final):
    batch, seq, d = x_prompt.shape
    dec_batch, t, _ = x_sample.shape
    depth = w_in.shape[0]
    assert depth == 1 and d == D_MODEL and meta.shape == (N_META, D_MODEL)
    assert seq % PROJ_ROWS == 0 and seq % Q_TILE == 0 and PROJ_ROWS % t == 0
    assert (dec_batch * t) % PROJ_ROWS == 0 and t >= POOL_STATE
    assert (N_META + seq) % (PROMPT_TILES * 16) == 0
    prompt_rows = (N_META + seq) // PROMPT_TILES
    n_cache = cache_k.shape[2]
    assert (n_cache - N_META) % CHUNK == 0 and n_cache > SAMPLE_TAIL

    w_pool_b = w_pool[0].astype(BF16)
    g_mix2 = g_mix[0].reshape(1, D_MODEL)
    g_ffn2 = g_ffn[0].reshape(1, D_MODEL)
    g_fin2 = g_final.reshape(1, D_MODEL)
    g_sub2 = g_subln[0].reshape(1, HEAD_DIM)
    ps2 = pool_scale[0].reshape(1, POOL_WIDTH)

    diag_b, meta_b, tail_b, lam = _bias_tables(
        rel_bias, lambda_q1, lambda_k1, lambda_q2, lambda_k2, t)

    xp = x_prompt.reshape(batch * seq, D_MODEL)
    xs = x_sample.reshape(dec_batch * t, D_MODEL)
    u_s, q_s, k_s, v_s, w_in_b = _project_rows(xs, g_mix2, w_in[0], PROJ_ROWS)
    u_p, q_p, k_all, v_all = _project_prompt(x_prompt, meta, g_mix2, w_in_b, prompt_rows)

    b_p = _attn_prompt(q_p, k_all, v_all, diag_b, meta_b, lam, g_sub2, batch, seq)
    b_s = _attn_sample(q_s, k_s, v_s,
                       cache_k[0].reshape(dec_batch, n_cache * HEADS, HEAD_DIM),
                       cache_v[0].reshape(dec_batch, n_cache * HEADS, HEAD_DIM),
                       tail_b, lam, g_sub2, dec_batch, t)

    state16 = jnp.pad(state_pool[0], ((0, 0), (N_META - POOL_STATE, 0), (0, 0)))
    h_s, w_out_b = _out_sample(u_s, state16, b_s, xs, w_pool_b, ps2, w_out[0], t)
    h_p = _out_prompt(u_p, b_p, xp, w_pool_b, ps2, w_out_b, batch, seq)

    y_s, wg_b, wu_b, wd_b = _ffn(h_s, g_ffn2, w_gate[0], w_up[0], w_down[0], g_fin2, FFN_COLS_F32)
    y_p = _ffn(h_p, g_ffn2, wg_b, wu_b, wd_b, g_fin2, FFN_COLS)

    hd = (HEADS, HEAD_DIM)
    return (
        y_p.reshape(batch, seq, D_MODEL),
        y_s.reshape(dec_batch, t, D_MODEL),
        k_all.reshape(1, batch, N_META + seq, *hd),
        v_all.reshape(1, batch, N_META + seq, *hd),
        u_p[None, :, N_META + seq - POOL_STATE:],
        k_s.reshape(1, dec_batch, t, *hd),
        v_s.reshape(1, dec_batch, t, *hd),
        u_s.reshape(dec_batch, t, POOL_WIDTH)[None, :, t - POOL_STATE:],
    )
```

```python
import functools
import math

import jax
import jax.numpy as jnp
from jax import lax
from jax.experimental import pallas as pl
from jax.experimental.pallas import tpu as pltpu

F32 = jnp.float32
BF16 = jnp.bfloat16

D_MODEL = 2048
CHUNK = 64
N_META = 16
POOL_WINDOWS = (2, 4, 8, 16)
POOL_GROUP_DIM = 256
POOL_WIDTH = 1024
POOL_STATE = 15
POOL_HISTORY = 16
HEADS = 8
HEAD_DIM = 128
HALF_DIM = 64
ATTN_WIDTH = 1024
ATT_SCALE = HALF_DIM ** -0.5
LOG2E = math.log2(math.e)
D_FF = 5632
EPS = 1e-6
SUBLN_EPS = 1e-5
NEG_INF = -1e30
LAMBDA_INIT = 0.8 - 0.6 * math.exp(-0.3 * 0)

_BUCKET_THRESHOLDS = (1, 2, 3, 4, 5, 6, 7, 8, 12, 16, 23, 32, 46, 64, 91)

Q_TILE = 256
SOFTMAX_ROWS = 32
PROJ_ROWS = 512
PROMPT_TILES = 3
FFN_ROWS = 1024
FFN_COLS = 512
FFN_COLS_F32 = 256
SAMPLE_TAIL = 272
VMEM_LIMIT = 56 * 1024 * 1024
BF16_ROWS = 16


def _rms(x, g, eps):
    ms = jnp.mean(x * x, axis=-1, keepdims=True)
    return x * lax.rsqrt(ms + eps) * g


def _dot(a, b):
    return jnp.dot(a, b, preferred_element_type=F32)


def _dot_nt(a, b):
    return lax.dot_general(a, b, (((1,), (1,)), ((), ())), preferred_element_type=F32)


def _resident(shape):
    nd = len(shape)
    return pl.BlockSpec(shape, lambda *_: (0,) * nd, pipeline_mode=pl.Buffered(1))


def _smem():
    return pl.BlockSpec(memory_space=pltpu.SMEM)


def _bias_of_rel(rel, table_ref, h):
    n = jnp.abs(rel)
    far = table_ref[len(_BUCKET_THRESHOLDS), h]

    def entry(b):
        return (table_ref[b, h] - far) * LOG2E

    neg = jnp.full(rel.shape, entry(0), F32)
    pos = jnp.full(rel.shape, entry(16), F32)
    for b, thr in enumerate(_BUCKET_THRESHOLDS, start=1):
        ge = n >= thr
        neg = jnp.where(ge, entry(b), neg)
        pos = jnp.where(ge, entry(16 + b), pos)
    return jnp.where(rel > 0, pos, neg)


def _bias_kernel(table_ref, lq1_ref, lk1_ref, lq2_ref, lk2_ref,
                 diag_ref, meta_ref, tail_ref, lam_ref):
    h = pl.program_id(0)
    t = Q_TILE
    qi = lax.broadcasted_iota(jnp.int32, (t, t), 0)
    kj = lax.broadcasted_iota(jnp.int32, (t, t), 1)
    visible = (kj // CHUNK) <= (qi // CHUNK)
    diag_ref[0, 0] = jnp.where(visible, _bias_of_rel(kj - qi, table_ref, h), NEG_INF)
    diag_ref[0, 1] = _bias_of_rel(kj - t - qi, table_ref, h)
    qm = lax.broadcasted_iota(jnp.int32, (t, N_META), 0)
    km = lax.broadcasted_iota(jnp.int32, (t, N_META), 1)
    meta_ref[0] = _bias_of_rel(km - N_META - qm, table_ref, h)
    qs = lax.broadcasted_iota(jnp.int32, tail_ref.shape[1:], 0)
    ks = lax.broadcasted_iota(jnp.int32, tail_ref.shape[1:], 1)
    tail_ref[0] = _bias_of_rel(ks - SAMPLE_TAIL - qs, table_ref, h)
    s1 = jnp.sum(lq1_ref[...] * lk1_ref[...], axis=-1, keepdims=True)
    s2 = jnp.sum(lq2_ref[...] * lk2_ref[...], axis=-1, keepdims=True)
    lam_ref[...] = jnp.exp(s1) - jnp.exp(s2) + LAMBDA_INIT


def _bias_tables(rel_bias, lq1, lk1, lq2, lk2, n_new):
    tail_w = SAMPLE_TAIL + n_new
    vec = pl.BlockSpec((1, HALF_DIM), lambda h: (0, 0))
    return pl.pallas_call(
        _bias_kernel,
        grid=(HEADS,),
        in_specs=[_smem(), vec, vec, vec, vec],
        out_specs=[
            pl.BlockSpec((1, 2, Q_TILE, Q_TILE), lambda h: (h, 0, 0, 0)),
            pl.BlockSpec((1, Q_TILE, N_META), lambda h: (h, 0, 0)),
            pl.BlockSpec((1, n_new, tail_w), lambda h: (h, 0, 0)),
            pl.BlockSpec((1, 1), lambda h: (0, 0)),
        ],
        out_shape=[
            jax.ShapeDtypeStruct((HEADS, 2, Q_TILE, Q_TILE), F32),
            jax.ShapeDtypeStruct((HEADS, Q_TILE, N_META), F32),
            jax.ShapeDtypeStruct((HEADS, n_new, tail_w), F32),
            jax.ShapeDtypeStruct((1, 1), F32),
        ],
        name="bias_tables",
    )(rel_bias, lq1, lk1, lq2, lk2)


def _store_heads(o_ref, x, row0=0):
    rows = x.shape[0]
    for h in range(HEADS):
        o_ref[pl.ds(row0 * HEADS + h, rows, stride=HEADS), :] = x[:, h * HEAD_DIM:(h + 1) * HEAD_DIM]


def _store_column(c, y, u_ref, q_ref, k_ref, v_ref, row0=0):
    rows = slice(row0, row0 + y.shape[0])
    if c == 0:
        u_ref[rows, :] = y
    elif c == 1:
        q_ref[rows, :] = (y * (ATT_SCALE * LOG2E)).astype(BF16)
    else:
        _store_heads(k_ref if c == 2 else v_ref, y, row0)


def _project_store(xn, w_ref, u_ref, q_ref, k_ref, v_ref, row0=0):
    w = POOL_WIDTH
    for c in range(4):
        _store_column(c, _dot(xn, w_ref[:, c * w:(c + 1) * w]), u_ref, q_ref, k_ref, v_ref, row0)


def _proj_rows_kernel(x_ref, g_ref, w_ref, u_ref, q_ref, k_ref, v_ref, wb_ref, xn_ref):
    j = pl.program_id(0)
    rows = x_ref.shape[0]
    tile = pl.ds(pl.multiple_of(pl.program_id(1) * rows, rows), rows)

    @pl.when(j == 0)
    def _():
        xn_ref[tile, :] = _rms(x_ref[...], g_ref[...], EPS).astype(BF16)

    wb = w_ref[...].astype(BF16)
    wb_ref[...] = wb
    y = _dot(xn_ref[tile, :], wb)
    for c in range(4):
        @pl.when(j == c)
        def _(c=c):
            _store_column(c, y, u_ref, q_ref, k_ref, v_ref)


def _proj_prompt_kernel(x_ref, meta_ref, g_ref, w_ref, u_ref, q_ref, k_ref, v_ref, xn_ref, carry_ref):
    i = pl.program_id(1)
    g = g_ref[...]
    rows = xn_ref.shape[0]
    split = -(-(rows // 2) // BF16_ROWS) * BF16_ROWS

    @pl.when(i == 0)
    def _():
        xn_ref[0:N_META, :] = _rms(meta_ref[...], g, EPS).astype(BF16)

    @pl.when(i > 0)
    def _():
        xn_ref[0:N_META, :] = carry_ref[...]

    xn_ref[N_META:split, :] = _rms(x_ref[0:split - N_META, :], g, EPS).astype(BF16)
    _project_store(xn_ref[0:split, :], w_ref, u_ref, q_ref, k_ref, v_ref, 0)
    xn_ref[split:rows, :] = _rms(x_ref[split - N_META:rows - N_META, :], g, EPS).astype(BF16)
    _project_store(xn_ref[split:rows, :], w_ref, u_ref, q_ref, k_ref, v_ref, split)

    @pl.when(i < pl.num_programs(1) - 1)
    def _():
        carry_ref[...] = _rms(x_ref[rows - N_META:rows, :], g, EPS).astype(BF16)


def _proj_out_shapes(lead, n):
    return [jax.ShapeDtypeStruct((*lead, n, POOL_WIDTH), F32),
            jax.ShapeDtypeStruct((*lead, n, POOL_WIDTH), BF16),
            jax.ShapeDtypeStruct((*lead, n * HEADS, HEAD_DIM), F32),
            jax.ShapeDtypeStruct((*lead, n * HEADS, HEAD_DIM), F32)]


def _project_rows(x, g, w_f32, rows):
    n = x.shape[0]
    w = POOL_WIDTH
    last = n // rows - 1

    def rows_of(c):
        return lambda j, i: (jnp.where(j < c, 0, jnp.where(j == c, i, last)), 0)

    uq = [pl.BlockSpec((rows, w), rows_of(c)) for c in (0, 1)]
    kv = [pl.BlockSpec((rows * HEADS, HEAD_DIM), rows_of(c)) for c in (2, 3)]
    w_spec = pl.BlockSpec((D_MODEL, w), lambda j, i: (0, j))
    return pl.pallas_call(
        _proj_rows_kernel,
        grid=(w_f32.shape[1] // w, n // rows),
        in_specs=[pl.BlockSpec((rows, D_MODEL), rows_of(0)), _resident((1, D_MODEL)), w_spec],
        out_specs=[*uq, *kv, w_spec],
        out_shape=_proj_out_shapes((), n) + [jax.ShapeDtypeStruct(w_f32.shape, BF16)],
        scratch_shapes=[pltpu.VMEM((n, D_MODEL), BF16)],
        compiler_params=pltpu.CompilerParams(
            dimension_semantics=("arbitrary", "arbitrary"), vmem_limit_bytes=VMEM_LIMIT),
        name="in_proj_rows",
    )(x, g, w_f32)


def _project_prompt(x, meta, g, w_bf16, rows):
    batch, seq, _ = x.shape
    total = N_META + seq
    x_spec = pl.BlockSpec((None, rows, D_MODEL), lambda b, i: (b, i, 0))
    uq = pl.BlockSpec((None, rows, POOL_WIDTH), lambda b, i: (b, i, 0))
    kv = pl.BlockSpec((None, rows * HEADS, HEAD_DIM), lambda b, i: (b, i, 0))
    return pl.pallas_call(
        _proj_prompt_kernel,
        grid=(batch, total // rows),
        in_specs=[x_spec, _resident(meta.shape), _resident((1, D_MODEL)), _resident(w_bf16.shape)],
        out_specs=[uq, uq, kv, kv],
        out_shape=_proj_out_shapes((batch,), total),
        scratch_shapes=[pltpu.VMEM((rows, D_MODEL), BF16), pltpu.VMEM((N_META, D_MODEL), BF16)],
        compiler_params=pltpu.CompilerParams(
            dimension_semantics=("parallel", "arbitrary"), vmem_limit_bytes=VMEM_LIMIT),
        name="in_proj_prompt",
    )(x, meta, g, w_bf16)


def _stack_maps(q):
    lane = lax.broadcasted_iota(jnp.int32, q.shape, 1)
    zero = jnp.zeros_like(q)
    return jnp.concatenate(
        [jnp.where(lane < HALF_DIM, q, zero), jnp.where(lane >= HALF_DIM, q, zero)], axis=0)


def _finish_head(o, gs):
    return (_rms(o, gs, SUBLN_EPS) * (1.0 - LAMBDA_INIT)).astype(BF16)


def _attn_prompt_kernel(lam_ref, q_ref, k_ref, v_ref, diag_ref, meta_ref, gs_ref,
                        o_ref, kb_ref, vb_ref, q2_ref, s_ref, sm_ref, w_ref, wm_ref, rl_ref):
    tq = Q_TILE
    nt = o_ref.shape[0] // tq
    rb = SOFTMAX_ROWS
    lam = lam_ref[0, 0]
    head_rows = pl.ds(pl.program_id(1), kb_ref.shape[0], stride=HEADS)
    kb_ref[...] = k_ref[0, head_rows, :].astype(BF16)
    vb_ref[...] = v_ref[0, head_rows, :].astype(BF16)

    def score_steps(c):
        slot = c % 2

        def stack():
            q2_ref[slot] = _stack_maps(q_ref[N_META + c * tq:N_META + (c + 1) * tq, :])

        def frames(j):
            d = _dot_nt(q2_ref[slot], kb_ref[N_META + j * tq:N_META + (j + 1) * tq, :])
            cols = slice(j * tq, (j + 1) * tq)
            if j >= c - 1:
                bias = diag_ref[0, c - j]
                s_ref[slot, :tq, cols] = d[:tq] + bias
                s_ref[slot, tq:, cols] = d[tq:] + bias
            else:
                s_ref[slot, :, cols] = d

        def metas():
            sm = _dot_nt(q2_ref[slot], kb_ref[0:N_META, :])
            if c == 0:
                sm = sm + jnp.concatenate([meta_ref[0], meta_ref[0]], axis=0)
            sm_ref[slot] = sm

        return [stack, metas] + [functools.partial(frames, j) for j in range(c + 1)]

    def softmax_step(c, i):
        slot = c % 2
        n = (c + 1) * tq
        last_chunk = ((i + 1) * rb - 1) // CHUNK
        seen = c * tq + -(-(last_chunk + 1) * CHUNK // HEAD_DIM) * HEAD_DIM
        rows = (slice(i * rb, (i + 1) * rb), slice(tq + i * rb, tq + (i + 1) * rb))
        p, pm, l = [], [], []
        for rmap in rows:
            sm = sm_ref[slot, rmap, :]
            m = jnp.maximum(jnp.max(s_ref[slot, rmap, :seen], axis=-1, keepdims=True),
                            jnp.max(sm, axis=-1, keepdims=True))
            p.append(jnp.exp2(s_ref[slot, rmap, :seen] - m))
            pm.append(jnp.exp2(sm - m))
            l.append(jnp.sum(p[-1], axis=-1, keepdims=True) + jnp.sum(pm[-1], axis=-1, keepdims=True))
        rho = lam * l[0] / l[1]
        w_ref[slot, rows[0], :seen] = (p[0] - p[1] * rho).astype(BF16)
        if seen < n:
            w_ref[slot, rows[0], seen:n] = jnp.zeros((rb, n - seen), BF16)
        wm_ref[slot, rows[0], :] = (pm[0] - pm[1] * rho).astype(BF16)
        rl_ref[slot, rows[0], :] = 1.0 / l[0]

    def value_step(c):
        slot = c % 2
        n = (c + 1) * tq
        o = _dot(w_ref[slot, :, :n], vb_ref[N_META:N_META + n, :]) + _dot(wm_ref[slot], vb_ref[0:N_META, :])
        o_ref[c * tq:(c + 1) * tq, :] = _finish_head(o * rl_ref[slot], gs_ref[...])

    for step in score_steps(0):
        step()
    for c in range(nt):
        ahead = score_steps(c + 1) if c + 1 < nt else []
        blocks = tq // rb
        for i in range(blocks):
            softmax_step(c, i)
            for step in ahead[i * len(ahead) // blocks:(i + 1) * len(ahead) // blocks]:
                step()
        value_step(c)


def _attn_prompt(q, k_all, v_all, diag, meta_b, lam, gs, batch, seq):
    tq = Q_TILE
    kv = pl.BlockSpec((1, (N_META + seq) * HEADS, HEAD_DIM), lambda b, h: (b, 0, 0))
    qi = pl.BlockSpec((None, N_META + seq, HEAD_DIM), lambda b, h: (b, 0, h))
    qo = pl.BlockSpec((seq, HEAD_DIM), lambda b, h: (b, h))
    return pl.pallas_call(
        _attn_prompt_kernel,
        grid=(batch, HEADS),
        in_specs=[
            _smem(), qi, kv, kv,
            pl.BlockSpec((1, 2, tq, tq), lambda b, h: (h, 0, 0, 0)),
            pl.BlockSpec((1, tq, N_META), lambda b, h: (h, 0, 0)),
            pl.BlockSpec((1, HEAD_DIM), lambda b, h: (0, 0)),
        ],
        out_specs=qo,
        out_shape=jax.ShapeDtypeStruct((batch * seq, ATTN_WIDTH), BF16),
        scratch_shapes=[
            pltpu.VMEM((N_META + seq, HEAD_DIM), BF16),
            pltpu.VMEM((N_META + seq, HEAD_DIM), BF16),
            pltpu.VMEM((2, 2 * tq, HEAD_DIM), BF16),
            pltpu.VMEM((2, 2 * tq, seq), F32),
            pltpu.VMEM((2, 2 * tq, N_META), F32),
            pltpu.VMEM((2, tq, seq), BF16),
            pltpu.VMEM((2, tq, N_META), BF16),
            pltpu.VMEM((2, tq, 1), F32),
        ],
        compiler_params=pltpu.CompilerParams(
            dimension_semantics=("parallel", "parallel"), vmem_limit_bytes=VMEM_LIMIT),
        name="attn_prompt",
    )(lam, q, k_all, v_all, diag, meta_b, gs)


def _attn_sample_kernel(lam_ref, q_ref, kn_ref, vn_ref, ck_ref, cv_ref, tail_ref, gs_ref,
                        o_ref, qd_ref, s_ref, w_ref):
    t = q_ref.shape[0]
    n_cache = ck_ref.shape[1] // HEADS
    split = n_cache - SAMPLE_TAIL
    width = n_cache + t
    lam = lam_ref[0, 0]
    hd = HEAD_DIM

    def gather(c_ref, n_ref, h):
        far = pl.ds(h, split, stride=HEADS)
        near = pl.ds(split * HEADS + h, SAMPLE_TAIL, stride=HEADS)
        new = pl.ds(h, t, stride=HEADS)
        return (c_ref[0, far, :].astype(BF16),
                jnp.concatenate([c_ref[0, near, :].astype(BF16), n_ref[new, :].astype(BF16)], axis=0))

    def pair_rows(c_ref, n_ref, hp):
        fa, ta = gather(c_ref, n_ref, 2 * hp)
        fb, tb = gather(c_ref, n_ref, 2 * hp + 1)
        return jnp.concatenate([fa, fb], axis=1), jnp.concatenate([ta, tb], axis=1)

    def score_steps(hp):
        slot = hp % 2

        def queries():
            zero = jnp.zeros((2 * t, hd), BF16)
            qa = _stack_maps(q_ref[:, 2 * hp * hd:(2 * hp + 1) * hd])
            qb = _stack_maps(q_ref[:, (2 * hp + 1) * hd:(2 * hp + 2) * hd])
            qd_ref[slot] = jnp.concatenate(
                [jnp.concatenate([qa, zero], axis=1), jnp.concatenate([zero, qb], axis=1)], axis=0)

        def scores():
            k_far, k_tail = pair_rows(ck_ref, kn_ref, hp)
            s_ref[slot, :, 0:split] = _dot_nt(qd_ref[slot], k_far)
            ba, bb = tail_ref[2 * hp], tail_ref[2 * hp + 1]
            s_ref[slot, :, split:width] = (
                _dot_nt(qd_ref[slot], k_tail) + jnp.concatenate([ba, ba, bb, bb], axis=0))

        return [queries, scores]

    def softmax_step(hp, j):
        slot = hp % 2
        r, p = [], []
        for rows in (slice(2 * t * j, 2 * t * j + t), slice(2 * t * j + t, 2 * t * (j + 1))):
            m = jnp.max(s_ref[slot, rows, 0:width], axis=-1, keepdims=True)
            p.append(jnp.exp2(s_ref[slot, rows, 0:width] - m))
            r.append(1.0 / jnp.sum(p[-1], axis=-1, keepdims=True))
        w_ref[slot, t * j:t * (j + 1), 0:width] = (p[0] * r[0] - p[1] * (lam * r[1])).astype(BF16)

    def value_step(hp):
        slot = hp % 2
        v_far, v_tail = pair_rows(cv_ref, vn_ref, hp)
        o = _dot(w_ref[slot, :, 0:split], v_far) + _dot(w_ref[slot, :, split:width], v_tail)
        gs = gs_ref[...]
        o_ref[:, 2 * hp * hd:(2 * hp + 1) * hd] = _finish_head(o[0:t, 0:hd], gs)
        o_ref[:, (2 * hp + 1) * hd:(2 * hp + 2) * hd] = _finish_head(o[t:2 * t, hd:2 * hd], gs)

    pairs = HEADS // 2
    for step in score_steps(0):
        step()
    for hp in range(pairs):
        ahead = score_steps(hp + 1) if hp + 1 < pairs else [lambda: None, lambda: None]
        softmax_step(hp, 0)
        ahead[0]()
        ahead[1]()
        softmax_step(hp, 1)
        value_step(hp)


def _attn_sample(q, k_new, v_new, cache_k, cache_v, tail_b, lam, gs, batch, t):
    n_cache = cache_k.shape[1] // HEADS
    row = pl.BlockSpec((t, ATTN_WIDTH), lambda b: (b, 0))
    new = pl.BlockSpec((t * HEADS, HEAD_DIM), lambda b: (b, 0))
    cache = pl.BlockSpec((1,) + cache_k.shape[1:], lambda b: (b, 0, 0))
    width = -(-(n_cache + t) // HEAD_DIM) * HEAD_DIM
    return pl.pallas_call(
        _attn_sample_kernel,
        grid=(batch,),
        in_specs=[_smem(), row, new, new, cache, cache,
                  _resident(tail_b.shape), _resident((1, HEAD_DIM))],
        out_specs=row,
        out_shape=jax.ShapeDtypeStruct((batch * t, ATTN_WIDTH), BF16),
        scratch_shapes=[
            pltpu.VMEM((2, 4 * t, 2 * HEAD_DIM), BF16),
            pltpu.VMEM((2, 4 * t, width), F32),
            pltpu.VMEM((2, 2 * t, width), BF16),
        ],
        compiler_params=pltpu.CompilerParams(
            dimension_semantics=("parallel",), vmem_limit_bytes=VMEM_LIMIT),
        name="attn_sample",
    )(lam, q, k_new, v_new, cache_k, cache_v, tail_b, gs)


def _pool_mix(ext, wp_ref, scale):
    outs = []
    for g, w in enumerate(POOL_WINDOWS):
        xg = ext[:, g * POOL_GROUP_DIM:(g + 1) * POOL_GROUP_DIM]
        s = xg
        span = 1
        while span < w:
            s = s + pltpu.roll(s, span, axis=0)
            span *= 2
        p = (s[POOL_HISTORY:] * (1.0 / w) - xg[POOL_HISTORY:]).astype(BF16)
        outs.append(_dot(p, wp_ref[g]))
    return jnp.concatenate(outs, axis=-1) * scale


def _mix_out(a, b_ref, x_ref, wo, o_ref):
    mix = jnp.concatenate([a.astype(BF16), b_ref[...]], axis=-1)
    o_ref[...] = x_ref[...] + _dot(mix, wo)


def _out_prompt_kernel(u_ref, b_ref, x_ref, wp_ref, ps_ref, wo_ref, o_ref):
    half = x_ref.shape[0] // 2
    for r in (0, half):
        a = _pool_mix(u_ref[r:r + POOL_HISTORY + half, :], wp_ref, ps_ref[...])
        mix = jnp.concatenate([a.astype(BF16), b_ref[r:r + half, :]], axis=-1)
        o_ref[r:r + half, :] = x_ref[r:r + half, :] + _dot(mix, wo_ref[...])


def _out_sample_kernel(u_ref, state_ref, b_ref, x_ref, wp_ref, ps_ref, wo_ref, o_ref, wob_ref):
    n_seq, t = state_ref.shape[0], u_ref.shape[0] // state_ref.shape[0]
    parts = []
    for s in range(n_seq):
        ext = jnp.concatenate([state_ref[s], u_ref[s * t:(s + 1) * t, :]], axis=0)
        parts.append(_pool_mix(ext, wp_ref, ps_ref[...]))
    wo = wo_ref[...].astype(BF16)
    wob_ref[...] = wo
    _mix_out(jnp.concatenate(parts, axis=0), b_ref, x_ref, wo, o_ref)


def _out_common_specs(rows, idx):
    return (pl.BlockSpec((rows, ATTN_WIDTH), idx), pl.BlockSpec((rows, D_MODEL), idx))


def _out_prompt(u, b, x, wp, ps, wo, batch, seq):
    rows = PROJ_ROWS
    nt = seq // rows
    idx = lambda bi, i: (bi * nt + i, 0)
    u_spec = pl.BlockSpec(
        (pl.Squeezed(), pl.Element(POOL_HISTORY + rows), pl.Element(POOL_WIDTH)),
        lambda bi, i: (bi, N_META - POOL_HISTORY + i * rows, 0))
    b_spec, x_spec = _out_common_specs(rows, idx)
    return pl.pallas_call(
        _out_prompt_kernel,
        grid=(batch, nt),
        in_specs=[u_spec, b_spec, x_spec,
                  _resident(wp.shape), _resident(ps.shape), _resident(wo.shape)],
        out_specs=x_spec,
        out_shape=jax.ShapeDtypeStruct(x.shape, F32),
        compiler_params=pltpu.CompilerParams(
            dimension_semantics=("parallel", "parallel"), vmem_limit_bytes=VMEM_LIMIT),
        name="out_proj_prompt",
    )(u, b, x, wp, ps, wo)


def _out_sample(u, history, b, x, wp, ps, wo, t):
    rows = PROJ_ROWS
    n_seq = rows // t
    idx = lambda i: (i, 0)
    b_spec, x_spec = _out_common_specs(rows, idx)
    return pl.pallas_call(
        _out_sample_kernel,
        grid=(x.shape[0] // rows,),
        in_specs=[
            pl.BlockSpec((rows, POOL_WIDTH), idx),
            pl.BlockSpec((n_seq, POOL_HISTORY, POOL_WIDTH), lambda i: (i, 0, 0)),
            b_spec, x_spec,
            _resident(wp.shape), _resident(ps.shape), _resident(wo.shape),
        ],
        out_specs=[x_spec, pl.BlockSpec(wo.shape, lambda i: (0, 0))],
        out_shape=[jax.ShapeDtypeStruct(x.shape, F32), jax.ShapeDtypeStruct(wo.shape, BF16)],
        compiler_params=pltpu.CompilerParams(
            dimension_semantics=("arbitrary",), vmem_limit_bytes=VMEM_LIMIT),
        name="out_proj_sample",
    )(u, history, b, x, wp, ps, wo)


def _ffn_kernel(h_ref, gf_ref, wg_ref, wu_ref, wd_ref, gfin_ref, y_ref, *rest):
    n_ref = rest[-1]
    f = pl.program_id(1)

    @pl.when(f == 0)
    def _():
        h = h_ref[...]
        n_ref[...] = _rms(h, gf_ref[...], EPS).astype(BF16)
        y_ref[...] = h

    weights = [wg_ref[...], wu_ref[...], wd_ref[...]]
    if len(rest) > 1:
        weights = [w.astype(BF16) for w in weights]
        for out_ref, w in zip(rest[:3], weights):
            out_ref[...] = w
    wg, wu, wd = weights
    n = n_ref[...]
    g = _dot(n, wg)
    u = _dot(n, wu)
    act = (g * (1.0 / (1.0 + jnp.exp(-g))) * u).astype(BF16)
    y_ref[...] += _dot(act, wd)

    @pl.when(f == pl.num_programs(1) - 1)
    def _():
        y_ref[...] = _rms(y_ref[...], gfin_ref[...], EPS)


def _ffn(h, gf, wg, wu, wd, gfin, cols):
    rows = FFN_ROWS
    emit = wg.dtype == F32
    row = pl.BlockSpec((rows, D_MODEL), lambda i, f: (i, 0))
    vec = pl.BlockSpec((1, D_MODEL), lambda i, f: (0, 0))
    w_in = pl.BlockSpec((D_MODEL, cols), lambda i, f: (0, f))
    w_out = pl.BlockSpec((cols, D_MODEL), lambda i, f: (f, 0))
    out_specs, out_shape = [row], [jax.ShapeDtypeStruct(h.shape, F32)]
    if emit:
        out_specs += [w_in, w_in, w_out]
        out_shape += [jax.ShapeDtypeStruct(w.shape, BF16) for w in (wg, wu, wd)]
    outs = pl.pallas_call(
        _ffn_kernel,
        grid=(h.shape[0] // rows, D_FF // cols),
        in_specs=[row, vec, w_in, w_in, w_out, vec],
        out_specs=out_specs,
        out_shape=out_shape,
        scratch_shapes=[pltpu.VMEM((rows, D_MODEL), BF16)],
        compiler_params=pltpu.CompilerParams(
            dimension_semantics=("arbitrary" if emit else "parallel", "arbitrary"),
            vmem_limit_bytes=VMEM_LIMIT),
        name="swiglu_f32w" if emit else "swiglu",
    )(h, gf, wg, wu, wd, gfin)
    return outs if emit else outs[0]


def kernel(x_prompt, x_sample, cache_k, cache_v, state_pool, meta, g_mix, w_in, w_pool,
           pool_scale, lambda_q1, lambda_k1, lambda_q2, lambda_k2, g_subln, w_out, g_ffn,
           w_gate, w_up, w_down, rel_bias, g_final):
    batch, seq, d = x_prompt.shape
    dec_batch, t, _ = x_sample.shape
    depth = w_in.shape[0]
    assert depth == 1 and d == D_MODEL and meta.shape == (N_META, D_MODEL)
    assert seq % PROJ_ROWS == 0 and seq % Q_TILE == 0 and PROJ_ROWS % t == 0
    assert (dec_batch * t) % PROJ_ROWS == 0 and t >= POOL_STATE
    assert POOL_STATE == max(POOL_WINDOWS) - 1 <= POOL_HISTORY <= N_META
    assert (N_META + seq) % (PROMPT_TILES * BF16_ROWS) == 0
    prompt_rows = (N_META + seq) // PROMPT_TILES
    n_cache = cache_k.shape[2]
    assert min(Q_TILE, SAMPLE_TAIL) >= _BUCKET_THRESHOLDS[-1]
    assert (n_cache - N_META) % CHUNK == 0 and t <= CHUNK
    assert n_cache > SAMPLE_TAIL and (n_cache - SAMPLE_TAIL) % HEAD_DIM == 0

    w_pool_b = w_pool[0].astype(BF16)
    g_mix2 = g_mix[0].reshape(1, D_MODEL)
    g_ffn2 = g_ffn[0].reshape(1, D_MODEL)
    g_fin2 = g_final.reshape(1, D_MODEL)
    g_sub2 = g_subln[0].reshape(1, HEAD_DIM)
    ps2 = pool_scale[0].reshape(1, POOL_WIDTH)

    diag_b, meta_b, tail_b, lam = _bias_tables(
        rel_bias, lambda_q1, lambda_k1, lambda_q2, lambda_k2, t)

    xp = x_prompt.reshape(batch * seq, D_MODEL)
    xs = x_sample.reshape(dec_batch * t, D_MODEL)
    u_s, q_s, k_s, v_s, w_in_b = _project_rows(xs, g_mix2, w_in[0], PROJ_ROWS)
    u_p, q_p, k_all, v_all = _project_prompt(x_prompt, meta, g_mix2, w_in_b, prompt_rows)

    b_p = _attn_prompt(q_p, k_all, v_all, diag_b, meta_b, lam, g_sub2, batch, seq)
    b_s = _attn_sample(q_s, k_s, v_s,
                       cache_k[0].reshape(dec_batch, n_cache * HEADS, HEAD_DIM),
                       cache_v[0].reshape(dec_batch, n_cache * HEADS, HEAD_DIM),
                       tail_b, lam, g_sub2, dec_batch, t)

    history = jnp.pad(state_pool[0], ((0, 0), (POOL_HISTORY - POOL_STATE, 0), (0, 0)))
    h_s, w_out_b = _out_sample(u_s, history, b_s, xs, w_pool_b, ps2, w_out[0], t)
    h_p = _out_prompt(u_p, b_p, xp, w_pool_b, ps2, w_out_b, batch, seq)

    y_s, wg_b, wu_b, wd_b = _ffn(h_s, g_ffn2, w_gate[0], w_up[0], w_down[0], g_fin2, FFN_COLS_F32)
    y_p = _ffn(h_p, g_ffn2, wg_b, wu_b, wd_b, g_fin2, FFN_COLS)

    hd = (HEADS, HEAD_DIM)
    return (
        y_p.reshape(batch, seq, D_MODEL),
        y_s.reshape(dec_batch, t, D_MODEL),
        k_all.reshape(1, batch, N_META + seq, *hd),
        v_all.reshape(1, batch, N_META + seq, *hd),
        u_p[None, :, N_META + seq - POOL_STATE:],
        k_s.reshape(1, dec_batch, t, *hd),
        v_s.reshape(1, dec_batch, t, *hd),
        u_s.reshape(dec_batch, t, POOL_WIDTH)[None, :, t - POOL_STATE:],
    )
```

```python
import functools
import math

import jax
import jax.numpy as jnp
from jax import lax
from jax.experimental import pallas as pl
from jax.experimental.pallas import tpu as pltpu

F32 = jnp.float32
BF16 = jnp.bfloat16

D_MODEL = 2048
CHUNK = 64
N_META = 16
POOL_WINDOWS = (2, 4, 8, 16)
POOL_GROUP_DIM = 256
POOL_WIDTH = 1024
POOL_STATE = 15
POOL_HISTORY = 16
HEADS = 8
HEAD_DIM = 128
HALF_DIM = 64
ATTN_WIDTH = 1024
ATT_SCALE = HALF_DIM ** -0.5
LOG2E = math.log2(math.e)
D_FF = 5632
EPS = 1e-6
SUBLN_EPS = 1e-5
NEG_INF = -1e30
LAMBDA_INIT = 0.8 - 0.6 * math.exp(-0.3 * 0)

_BUCKET_THRESHOLDS = (1, 2, 3, 4, 5, 6, 7, 8, 12, 16, 23, 32, 46, 64, 91)

Q_TILE = 256
SOFTMAX_ROWS = 32
PROJ_ROWS = 512
PROMPT_TILES = 3
FFN_ROWS = 1024
FFN_COLS = 512
FFN_COLS_F32 = 256
SAMPLE_TAIL = 272
VMEM_LIMIT = 56 * 1024 * 1024
BF16_ROWS = 16


def _rms(x, g, eps):
    ms = jnp.mean(x * x, axis=-1, keepdims=True)
    return x * lax.rsqrt(ms + eps) * g


def _dot(a, b):
    return jnp.dot(a, b, preferred_element_type=F32)


def _dot_nt(a, b):
    return lax.dot_general(a, b, (((1,), (1,)), ((), ())), preferred_element_type=F32)


def _resident(shape):
    nd = len(shape)
    return pl.BlockSpec(shape, lambda *_: (0,) * nd, pipeline_mode=pl.Buffered(1))


def _smem():
    return pl.BlockSpec(memory_space=pltpu.SMEM)


def _bias_of_rel(rel, table_ref, h):
    n = jnp.abs(rel)
    far = table_ref[len(_BUCKET_THRESHOLDS), h]

    def entry(b):
        return (table_ref[b, h] - far) * LOG2E

    neg = jnp.full(rel.shape, entry(0), F32)
    pos = jnp.full(rel.shape, entry(16), F32)
    for b, thr in enumerate(_BUCKET_THRESHOLDS, start=1):
        ge = n >= thr
        neg = jnp.where(ge, entry(b), neg)
        pos = jnp.where(ge, entry(16 + b), pos)
    return jnp.where(rel > 0, pos, neg)


def _bias_kernel(table_ref, lq1_ref, lk1_ref, lq2_ref, lk2_ref,
                 diag_ref, meta_ref, tail_ref, lam_ref):
    h = pl.program_id(0)
    t = Q_TILE
    qi = lax.broadcasted_iota(jnp.int32, (t, t), 0)
    kj = lax.broadcasted_iota(jnp.int32, (t, t), 1)
    visible = (kj // CHUNK) <= (qi // CHUNK)
    diag_ref[0, 0] = jnp.where(visible, _bias_of_rel(kj - qi, table_ref, h), NEG_INF)
    diag_ref[0, 1] = _bias_of_rel(kj - t - qi, table_ref, h)
    qm = lax.broadcasted_iota(jnp.int32, (t, N_META), 0)
    km = lax.broadcasted_iota(jnp.int32, (t, N_META), 1)
    meta_ref[0] = _bias_of_rel(km - N_META - qm, table_ref, h)
    qs = lax.broadcasted_iota(jnp.int32, tail_ref.shape[1:], 0)
    ks = lax.broadcasted_iota(jnp.int32, tail_ref.shape[1:], 1)
    tail_ref[0] = _bias_of_rel(ks - SAMPLE_TAIL - qs, table_ref, h)
    s1 = jnp.sum(lq1_ref[...] * lk1_ref[...], axis=-1, keepdims=True)
    s2 = jnp.sum(lq2_ref[...] * lk2_ref[...], axis=-1, keepdims=True)
    lam_ref[...] = jnp.exp(s1) - jnp.exp(s2) + LAMBDA_INIT


def _bias_tables(rel_bias, lq1, lk1, lq2, lk2, n_new):
    tail_w = SAMPLE_TAIL + n_new
    vec = pl.BlockSpec((1, HALF_DIM), lambda h: (0, 0))
    return pl.pallas_call(
        _bias_kernel,
        grid=(HEADS,),
        in_specs=[_smem(), vec, vec, vec, vec],
        out_specs=[
            pl.BlockSpec((1, 2, Q_TILE, Q_TILE), lambda h: (h, 0, 0, 0)),
            pl.BlockSpec((1, Q_TILE, N_META), lambda h: (h, 0, 0)),
            pl.BlockSpec((1, n_new, tail_w), lambda h: (h, 0, 0)),
            pl.BlockSpec((1, 1), lambda h: (0, 0)),
        ],
        out_shape=[
            jax.ShapeDtypeStruct((HEADS, 2, Q_TILE, Q_TILE), F32),
            jax.ShapeDtypeStruct((HEADS, Q_TILE, N_META), F32),
            jax.ShapeDtypeStruct((HEADS, n_new, tail_w), F32),
            jax.ShapeDtypeStruct((1, 1), F32),
        ],
        name="bias_tables",
    )(rel_bias, lq1, lk1, lq2, lk2)


def _store_heads(o_ref, x, row0=0):
    rows = x.shape[0]
    for h in range(HEADS):
        o_ref[pl.ds(row0 * HEADS + h, rows, stride=HEADS), :] = x[:, h * HEAD_DIM:(h + 1) * HEAD_DIM]


def _store_column(c, y, u_ref, q_ref, k_ref, v_ref, row0=0, by_head=None):
    rows = slice(row0, row0 + y.shape[0])
    if c == 0:
        u_ref[rows, :] = y
    elif c == 1:
        q_ref[rows, :] = (y * (ATT_SCALE * LOG2E)).astype(BF16)
    else:
        _store_heads(k_ref if c == 2 else v_ref, y, row0)
        if by_head is not None:
            for h in range(HEADS):
                by_head[c - 2][h, rows, :] = y[:, h * HEAD_DIM:(h + 1) * HEAD_DIM].astype(BF16)


def _project_store(xn, w_ref, u_ref, q_ref, k_ref, v_ref, row0=0, by_head=None):
    w = POOL_WIDTH
    for c in range(4):
        _store_column(c, _dot(xn, w_ref[:, c * w:(c + 1) * w]), u_ref, q_ref, k_ref, v_ref, row0, by_head)


def _proj_rows_kernel(x_ref, g_ref, w_ref, u_ref, q_ref, k_ref, v_ref, wb_ref, xn_ref):
    j = pl.program_id(0)
    rows = x_ref.shape[0]
    tile = pl.ds(pl.multiple_of(pl.program_id(1) * rows, rows), rows)

    @pl.when(j == 0)
    def _():
        xn_ref[tile, :] = _rms(x_ref[...], g_ref[...], EPS).astype(BF16)

    wb = w_ref[...].astype(BF16)
    wb_ref[...] = wb
    y = _dot(xn_ref[tile, :], wb)
    for c in range(4):
        @pl.when(j == c)
        def _(c=c):
            _store_column(c, y, u_ref, q_ref, k_ref, v_ref)


def _proj_prompt_kernel(x_ref, meta_ref, g_ref, w_ref, u_ref, q_ref, k_ref, v_ref, kb_ref, vb_ref,
                        xn_ref, carry_ref):
    i = pl.program_id(1)
    g = g_ref[...]
    rows = xn_ref.shape[0]
    split = -(-(rows // 2) // BF16_ROWS) * BF16_ROWS

    @pl.when(i == 0)
    def _():
        xn_ref[0:N_META, :] = _rms(meta_ref[...], g, EPS).astype(BF16)

    @pl.when(i > 0)
    def _():
        xn_ref[0:N_META, :] = carry_ref[...]

    xn_ref[N_META:split, :] = _rms(x_ref[0:split - N_META, :], g, EPS).astype(BF16)
    by_head = (kb_ref, vb_ref)
    _project_store(xn_ref[0:split, :], w_ref, u_ref, q_ref, k_ref, v_ref, 0, by_head)
    xn_ref[split:rows, :] = _rms(x_ref[split - N_META:rows - N_META, :], g, EPS).astype(BF16)
    _project_store(xn_ref[split:rows, :], w_ref, u_ref, q_ref, k_ref, v_ref, split, by_head)

    @pl.when(i < pl.num_programs(1) - 1)
    def _():
        carry_ref[...] = _rms(x_ref[rows - N_META:rows, :], g, EPS).astype(BF16)


def _proj_out_shapes(lead, n):
    return [jax.ShapeDtypeStruct((*lead, n, POOL_WIDTH), F32),
            jax.ShapeDtypeStruct((*lead, n, POOL_WIDTH), BF16),
            jax.ShapeDtypeStruct((*lead, n * HEADS, HEAD_DIM), F32),
            jax.ShapeDtypeStruct((*lead, n * HEADS, HEAD_DIM), F32)]


def _project_rows(x, g, w_f32, rows):
    n = x.shape[0]
    w = POOL_WIDTH
    last = n // rows - 1

    def rows_of(c):
        return lambda j, i: (jnp.where(j < c, 0, jnp.where(j == c, i, last)), 0)

    uq = [pl.BlockSpec((rows, w), rows_of(c)) for c in (0, 1)]
    kv = [pl.BlockSpec((rows * HEADS, HEAD_DIM), rows_of(c)) for c in (2, 3)]
    w_spec = pl.BlockSpec((D_MODEL, w), lambda j, i: (0, j))
    return pl.pallas_call(
        _proj_rows_kernel,
        grid=(w_f32.shape[1] // w, n // rows),
        in_specs=[pl.BlockSpec((rows, D_MODEL), rows_of(0)), _resident((1, D_MODEL)), w_spec],
        out_specs=[*uq, *kv, w_spec],
        out_shape=_proj_out_shapes((), n) + [jax.ShapeDtypeStruct(w_f32.shape, BF16)],
        scratch_shapes=[pltpu.VMEM((n, D_MODEL), BF16)],
        compiler_params=pltpu.CompilerParams(
            dimension_semantics=("arbitrary", "arbitrary"), vmem_limit_bytes=VMEM_LIMIT),
        name="in_proj_rows",
    )(x, g, w_f32)


def _project_prompt(x, meta, g, w_bf16, rows):
    batch, seq, _ = x.shape
    total = N_META + seq
    x_spec = pl.BlockSpec((None, rows, D_MODEL), lambda b, i: (b, i, 0))
    uq = pl.BlockSpec((None, rows, POOL_WIDTH), lambda b, i: (b, i, 0))
    kv = pl.BlockSpec((None, rows * HEADS, HEAD_DIM), lambda b, i: (b, i, 0))
    kvb = pl.BlockSpec((None, HEADS, rows, HEAD_DIM), lambda b, i: (b, 0, i, 0))
    return pl.pallas_call(
        _proj_prompt_kernel,
        grid=(batch, total // rows),
        in_specs=[x_spec, _resident(meta.shape), _resident((1, D_MODEL)), _resident(w_bf16.shape)],
        out_specs=[uq, uq, kv, kv, kvb, kvb],
        out_shape=_proj_out_shapes((batch,), total)
        + [jax.ShapeDtypeStruct((batch, HEADS, total, HEAD_DIM), BF16)] * 2,
        scratch_shapes=[pltpu.VMEM((rows, D_MODEL), BF16), pltpu.VMEM((N_META, D_MODEL), BF16)],
        compiler_params=pltpu.CompilerParams(
            dimension_semantics=("parallel", "arbitrary"), vmem_limit_bytes=VMEM_LIMIT),
        name="in_proj_prompt",
    )(x, meta, g, w_bf16)


def _stack_maps(q):
    lane = lax.broadcasted_iota(jnp.int32, q.shape, 1)
    zero = jnp.zeros_like(q)
    return jnp.concatenate(
        [jnp.where(lane < HALF_DIM, q, zero), jnp.where(lane >= HALF_DIM, q, zero)], axis=0)


def _finish_head(o, gs):
    return (_rms(o, gs, SUBLN_EPS) * (1.0 - LAMBDA_INIT)).astype(BF16)


def _attn_prompt_kernel(lam_ref, q_ref, kb_ref, vb_ref, diag_ref, meta_ref, gs_ref,
                        o_ref, q2_ref, s_ref, sm_ref, w_ref, wm_ref, rl_ref):
    tq = Q_TILE
    nt = o_ref.shape[0] // tq
    rb = SOFTMAX_ROWS
    lam = lam_ref[0, 0]

    def score_steps(c):
        slot = c % 2

        def stack():
            q2_ref[slot] = _stack_maps(q_ref[N_META + c * tq:N_META + (c + 1) * tq, :])

        def frames(j):
            d = _dot_nt(q2_ref[slot], kb_ref[N_META + j * tq:N_META + (j + 1) * tq, :])
            cols = slice(j * tq, (j + 1) * tq)
            if j >= c - 1:
                bias = diag_ref[0, c - j]
                s_ref[slot, :tq, cols] = d[:tq] + bias
                s_ref[slot, tq:, cols] = d[tq:] + bias
            else:
                s_ref[slot, :, cols] = d

        def metas():
            sm = _dot_nt(q2_ref[slot], kb_ref[0:N_META, :])
            if c == 0:
                sm = sm + jnp.concatenate([meta_ref[0], meta_ref[0]], axis=0)
            sm_ref[slot] = sm

        return [stack, metas] + [functools.partial(frames, j) for j in range(c + 1)]

    def softmax_step(c, i):
        slot = c % 2
        n = (c + 1) * tq
        last_chunk = ((i + 1) * rb - 1) // CHUNK
        seen = c * tq + -(-(last_chunk + 1) * CHUNK // HEAD_DIM) * HEAD_DIM
        rows = (slice(i * rb, (i + 1) * rb), slice(tq + i * rb, tq + (i + 1) * rb))
        p, pm, l = [], [], []
        for rmap in rows:
            sm = sm_ref[slot, rmap, :]
            m = jnp.maximum(jnp.max(s_ref[slot, rmap, :seen], axis=-1, keepdims=True),
                            jnp.max(sm, axis=-1, keepdims=True))
            p.append(jnp.exp2(s_ref[slot, rmap, :seen] - m))
            pm.append(jnp.exp2(sm - m))
            l.append(jnp.sum(p[-1], axis=-1, keepdims=True) + jnp.sum(pm[-1], axis=-1, keepdims=True))
        rho = lam * l[0] / l[1]
        w_ref[slot, rows[0], :seen] = (p[0] - p[1] * rho).astype(BF16)
        if seen < n:
            w_ref[slot, rows[0], seen:n] = jnp.zeros((rb, n - seen), BF16)
        wm_ref[slot, rows[0], :] = (pm[0] - pm[1] * rho).astype(BF16)
        rl_ref[slot, rows[0], :] = 1.0 / l[0]

    def value_step(c):
        slot = c % 2
        n = (c + 1) * tq
        o = _dot(w_ref[slot, :, :n], vb_ref[N_META:N_META + n, :]) + _dot(wm_ref[slot], vb_ref[0:N_META, :])
        o_ref[c * tq:(c + 1) * tq, :] = _finish_head(o * rl_ref[slot], gs_ref[...])

    for step in score_steps(0):
        step()
    for c in range(nt):
        ahead = score_steps(c + 1) if c + 1 < nt else []
        blocks = tq // rb
        for i in range(blocks):
            softmax_step(c, i)
            for step in ahead[i * len(ahead) // blocks:(i + 1) * len(ahead) // blocks]:
                step()
        value_step(c)


def _attn_prompt(q, kb, vb, diag, meta_b, lam, gs, batch, seq):
    tq = Q_TILE
    kv = pl.BlockSpec((None, None, N_META + seq, HEAD_DIM), lambda b, h: (b, h, 0, 0))
    qi = pl.BlockSpec((None, N_META + seq, HEAD_DIM), lambda b, h: (b, 0, h))
    qo = pl.BlockSpec((seq, HEAD_DIM), lambda b, h: (b, h))
    return pl.pallas_call(
        _attn_prompt_kernel,
        grid=(batch, HEADS),
        in_specs=[
            _smem(), qi, kv, kv,
            pl.BlockSpec((1, 2, tq, tq), lambda b, h: (h, 0, 0, 0)),
            pl.BlockSpec((1, tq, N_META), lambda b, h: (h, 0, 0)),
            pl.BlockSpec((1, HEAD_DIM), lambda b, h: (0, 0)),
        ],
        out_specs=qo,
        out_shape=jax.ShapeDtypeStruct((batch * seq, ATTN_WIDTH), BF16),
        scratch_shapes=[
            pltpu.VMEM((2, 2 * tq, HEAD_DIM), BF16),
            pltpu.VMEM((2, 2 * tq, seq), F32),
            pltpu.VMEM((2, 2 * tq, N_META), F32),
            pltpu.VMEM((2, tq, seq), BF16),
            pltpu.VMEM((2, tq, N_META), BF16),
            pltpu.VMEM((2, tq, 1), F32),
        ],
        compiler_params=pltpu.CompilerParams(
            dimension_semantics=("parallel", "parallel"), vmem_limit_bytes=VMEM_LIMIT),
        name="attn_prompt",
    )(lam, q, kb, vb, diag, meta_b, gs)


def _attn_sample_kernel(lam_ref, q_ref, kn_ref, vn_ref, ck_ref, cv_ref, tail_ref, gs_ref,
                        o_ref, qd_ref, s_ref, w_ref):
    t = q_ref.shape[0]
    n_cache = ck_ref.shape[1] // HEADS
    split = n_cache - SAMPLE_TAIL
    width = n_cache + t
    lam = lam_ref[0, 0]
    hd = HEAD_DIM

    def gather(c_ref, n_ref, h):
        far = pl.ds(h, split, stride=HEADS)
        near = pl.ds(split * HEADS + h, SAMPLE_TAIL, stride=HEADS)
        new = pl.ds(h, t, stride=HEADS)
        return (c_ref[0, far, :].astype(BF16),
                jnp.concatenate([c_ref[0, near, :].astype(BF16), n_ref[new, :].astype(BF16)], axis=0))

    def pair_rows(c_ref, n_ref, hp):
        fa, ta = gather(c_ref, n_ref, 2 * hp)
        fb, tb = gather(c_ref, n_ref, 2 * hp + 1)
        return jnp.concatenate([fa, fb], axis=1), jnp.concatenate([ta, tb], axis=1)

    def score_steps(hp):
        slot = hp % 2

        def queries():
            zero = jnp.zeros((2 * t, hd), BF16)
            qa = _stack_maps(q_ref[:, 2 * hp * hd:(2 * hp + 1) * hd])
            qb = _stack_maps(q_ref[:, (2 * hp + 1) * hd:(2 * hp + 2) * hd])
            qd_ref[slot] = jnp.concatenate(
                [jnp.concatenate([qa, zero], axis=1), jnp.concatenate([zero, qb], axis=1)], axis=0)

        def scores():
            k_far, k_tail = pair_rows(ck_ref, kn_ref, hp)
            s_ref[slot, :, 0:split] = _dot_nt(qd_ref[slot], k_far)
            ba, bb = tail_ref[2 * hp], tail_ref[2 * hp + 1]
            s_ref[slot, :, split:width] = (
                _dot_nt(qd_ref[slot], k_tail) + jnp.concatenate([ba, ba, bb, bb], axis=0))

        return [queries, scores]

    def softmax_step(hp, j):
        slot = hp % 2
        r, p = [], []
        for rows in (slice(2 * t * j, 2 * t * j + t), slice(2 * t * j + t, 2 * t * (j + 1))):
            m = jnp.max(s_ref[slot, rows, 0:width], axis=-1, keepdims=True)
            p.append(jnp.exp2(s_ref[slot, rows, 0:width] - m))
            r.append(1.0 / jnp.sum(p[-1], axis=-1, keepdims=True))
        w_ref[slot, t * j:t * (j + 1), 0:width] = (p[0] * r[0] - p[1] * (lam * r[1])).astype(BF16)

    def value_step(hp):
        slot = hp % 2
        v_far, v_tail = pair_rows(cv_ref, vn_ref, hp)
        o = _dot(w_ref[slot, :, 0:split], v_far) + _dot(w_ref[slot, :, split:width], v_tail)
        gs = gs_ref[...]
        o_ref[:, 2 * hp * hd:(2 * hp + 1) * hd] = _finish_head(o[0:t, 0:hd], gs)
        o_ref[:, (2 * hp + 1) * hd:(2 * hp + 2) * hd] = _finish_head(o[t:2 * t, hd:2 * hd], gs)

    pairs = HEADS // 2
    for step in score_steps(0):
        step()
    for hp in range(pairs):
        ahead = score_steps(hp + 1) if hp + 1 < pairs else [lambda: None, lambda: None]
        softmax_step(hp, 0)
        ahead[0]()
        ahead[1]()
        softmax_step(hp, 1)
        value_step(hp)


def _attn_sample(q, k_new, v_new, cache_k, cache_v, tail_b, lam, gs, batch, t):
    n_cache = cache_k.shape[1] // HEADS
    row = pl.BlockSpec((t, ATTN_WIDTH), lambda b: (b, 0))
    new = pl.BlockSpec((t * HEADS, HEAD_DIM), lambda b: (b, 0))
    cache = pl.BlockSpec((1,) + cache_k.shape[1:], lambda b: (b, 0, 0))
    width = -(-(n_cache + t) // HEAD_DIM) * HEAD_DIM
    return pl.pallas_call(
        _attn_sample_kernel,
        grid=(batch,),
        in_specs=[_smem(), row, new, new, cache, cache,
                  _resident(tail_b.shape), _resident((1, HEAD_DIM))],
        out_specs=row,
        out_shape=jax.ShapeDtypeStruct((batch * t, ATTN_WIDTH), BF16),
        scratch_shapes=[
            pltpu.VMEM((2, 4 * t, 2 * HEAD_DIM), BF16),
            pltpu.VMEM((2, 4 * t, width), F32),
            pltpu.VMEM((2, 2 * t, width), BF16),
        ],
        compiler_params=pltpu.CompilerParams(
            dimension_semantics=("parallel",), vmem_limit_bytes=VMEM_LIMIT),
        name="attn_sample",
    )(lam, q, k_new, v_new, cache_k, cache_v, tail_b, gs)


def _pool_mix(ext, wp_ref, scale):
    outs = []
    for g, w in enumerate(POOL_WINDOWS):
        xg = ext[:, g * POOL_GROUP_DIM:(g + 1) * POOL_GROUP_DIM]
        s = xg
        span = 1
        while span < w:
            s = s + pltpu.roll(s, span, axis=0)
            span *= 2
        p = (s[POOL_HISTORY:] * (1.0 / w) - xg[POOL_HISTORY:]).astype(BF16)
        outs.append(_dot(p, wp_ref[g]))
    return jnp.concatenate(outs, axis=-1) * scale


def _mix_out(a, b_ref, x_ref, wo, o_ref):
    mix = jnp.concatenate([a.astype(BF16), b_ref[...]], axis=-1)
    o_ref[...] = x_ref[...] + _dot(mix, wo)


def _out_prompt_kernel(u_ref, b_ref, x_ref, wp_ref, ps_ref, wo_ref, o_ref):
    half = x_ref.shape[0] // 2
    for r in (0, half):
        a = _pool_mix(u_ref[r:r + POOL_HISTORY + half, :], wp_ref, ps_ref[...])
        mix = jnp.concatenate([a.astype(BF16), b_ref[r:r + half, :]], axis=-1)
        o_ref[r:r + half, :] = x_ref[r:r + half, :] + _dot(mix, wo_ref[...])


def _out_sample_kernel(u_ref, state_ref, b_ref, x_ref, wp_ref, ps_ref, wo_ref, o_ref, wob_ref):
    n_seq, t = state_ref.shape[0], u_ref.shape[0] // state_ref.shape[0]
    parts = []
    for s in range(n_seq):
        ext = jnp.concatenate([state_ref[s], u_ref[s * t:(s + 1) * t, :]], axis=0)
        parts.append(_pool_mix(ext, wp_ref, ps_ref[...]))
    wo = wo_ref[...].astype(BF16)
    wob_ref[...] = wo
    _mix_out(jnp.concatenate(parts, axis=0), b_ref, x_ref, wo, o_ref)


def _out_common_specs(rows, idx):
    return (pl.BlockSpec((rows, ATTN_WIDTH), idx), pl.BlockSpec((rows, D_MODEL), idx))


def _out_prompt(u, b, x, wp, ps, wo, batch, seq):
    rows = PROJ_ROWS
    nt = seq // rows
    idx = lambda bi, i: (bi * nt + i, 0)
    u_spec = pl.BlockSpec(
        (pl.Squeezed(), pl.Element(POOL_HISTORY + rows), pl.Element(POOL_WIDTH)),
        lambda bi, i: (bi, N_META - POOL_HISTORY + i * rows, 0))
    b_spec, x_spec = _out_common_specs(rows, idx)
    return pl.pallas_call(
        _out_prompt_kernel,
        grid=(batch, nt),
        in_specs=[u_spec, b_spec, x_spec,
                  _resident(wp.shape), _resident(ps.shape), _resident(wo.shape)],
        out_specs=x_spec,
        out_shape=jax.ShapeDtypeStruct(x.shape, F32),
        compiler_params=pltpu.CompilerParams(
            dimension_semantics=("parallel", "parallel"), vmem_limit_bytes=VMEM_LIMIT),
        name="out_proj_prompt",
    )(u, b, x, wp, ps, wo)


def _out_sample(u, history, b, x, wp, ps, wo, t):
    rows = PROJ_ROWS
    n_seq = rows // t
    idx = lambda i: (i, 0)
    b_spec, x_spec = _out_common_specs(rows, idx)
    return pl.pallas_call(
        _out_sample_kernel,
        grid=(x.shape[0] // rows,),
        in_specs=[
            pl.BlockSpec((rows, POOL_WIDTH), idx),
            pl.BlockSpec((n_seq, POOL_HISTORY, POOL_WIDTH), lambda i: (i, 0, 0)),
            b_spec, x_spec,
            _resident(wp.shape), _resident(ps.shape), _resident(wo.shape),
        ],
        out_specs=[x_spec, pl.BlockSpec(wo.shape, lambda i: (0, 0))],
        out_shape=[jax.ShapeDtypeStruct(x.shape, F32), jax.ShapeDtypeStruct(wo.shape, BF16)],
        compiler_params=pltpu.CompilerParams(
            dimension_semantics=("arbitrary",), vmem_limit_bytes=VMEM_LIMIT),
        name="out_proj_sample",
    )(u, history, b, x, wp, ps, wo)


def _ffn_kernel(h_ref, gf_ref, wg_ref, wu_ref, wd_ref, gfin_ref, y_ref, *rest):
    n_ref = rest[-1]
    f = pl.program_id(1)

    @pl.when(f == 0)
    def _():
        h = h_ref[...]
        n_ref[...] = _rms(h, gf_ref[...], EPS).astype(BF16)
        y_ref[...] = h

    weights = [wg_ref[...], wu_ref[...], wd_ref[...]]
    if len(rest) > 1:
        weights = [w.astype(BF16) for w in weights]
        for out_ref, w in zip(rest[:3], weights):
            out_ref[...] = w
    wg, wu, wd = weights
    n = n_ref[...]
    g = _dot(n, wg)
    u = _dot(n, wu)
    act = (g * (1.0 / (1.0 + jnp.exp(-g))) * u).astype(BF16)
    y_ref[...] += _dot(act, wd)

    @pl.when(f == pl.num_programs(1) - 1)
    def _():
        y_ref[...] = _rms(y_ref[...], gfin_ref[...], EPS)


def _ffn(h, gf, wg, wu, wd, gfin, cols):
    rows = FFN_ROWS
    emit = wg.dtype == F32
    row = pl.BlockSpec((rows, D_MODEL), lambda i, f: (i, 0))
    vec = pl.BlockSpec((1, D_MODEL), lambda i, f: (0, 0))
    w_in = pl.BlockSpec((D_MODEL, cols), lambda i, f: (0, f))
    w_out = pl.BlockSpec((cols, D_MODEL), lambda i, f: (f, 0))
    out_specs, out_shape = [row], [jax.ShapeDtypeStruct(h.shape, F32)]
    if emit:
        out_specs += [w_in, w_in, w_out]
        out_shape += [jax.ShapeDtypeStruct(w.shape, BF16) for w in (wg, wu, wd)]
    outs = pl.pallas_call(
        _ffn_kernel,
        grid=(h.shape[0] // rows, D_FF // cols),
        in_specs=[row, vec, w_in, w_in, w_out, vec],
        out_specs=out_specs,
        out_shape=out_shape,
        scratch_shapes=[pltpu.VMEM((rows, D_MODEL), BF16)],
        compiler_params=pltpu.CompilerParams(
            dimension_semantics=("arbitrary" if emit else "parallel", "arbitrary"),
            vmem_limit_bytes=VMEM_LIMIT),
        name="swiglu_f32w" if emit else "swiglu",
    )(h, gf, wg, wu, wd, gfin)
    return outs if emit else outs[0]


def kernel(x_prompt, x_sample, cache_k, cache_v, state_pool, meta, g_mix, w_in, w_pool,
           pool_scale, lambda_q1, lambda_k1, lambda_q2, lambda_k2, g_subln, w_out, g_ffn,
           w_gate, w_up, w_down, rel_bias, g_final):
    batch, seq, d = x_prompt.shape
    dec_batch, t, _ = x_sample.shape
    depth = w_in.shape[0]
    assert depth == 1 and d == D_MODEL and meta.shape == (N_META, D_MODEL)
    assert seq % PROJ_ROWS == 0 and seq % Q_TILE == 0 and PROJ_ROWS % t == 0
    assert (dec_batch * t) % PROJ_ROWS == 0 and t >= POOL_STATE
    assert POOL_STATE == max(POOL_WINDOWS) - 1 <= POOL_HISTORY <= N_META
    assert (N_META + seq) % (PROMPT_TILES * BF16_ROWS) == 0
    prompt_rows = (N_META + seq) // PROMPT_TILES
    n_cache = cache_k.shape[2]
    assert min(Q_TILE, SAMPLE_TAIL) >= _BUCKET_THRESHOLDS[-1]
    assert (n_cache - N_META) % CHUNK == 0 and t <= CHUNK
    assert n_cache > SAMPLE_TAIL and (n_cache - SAMPLE_TAIL) % HEAD_DIM == 0

    w_pool_b = w_pool[0].astype(BF16)
    g_mix2 = g_mix[0].reshape(1, D_MODEL)
    g_ffn2 = g_ffn[0].reshape(1, D_MODEL)
    g_fin2 = g_final.reshape(1, D_MODEL)
    g_sub2 = g_subln[0].reshape(1, HEAD_DIM)
    ps2 = pool_scale[0].reshape(1, POOL_WIDTH)

    diag_b, meta_b, tail_b, lam = _bias_tables(
        rel_bias, lambda_q1, lambda_k1, lambda_q2, lambda_k2, t)

    xp = x_prompt.reshape(batch * seq, D_MODEL)
    xs = x_sample.reshape(dec_batch * t, D_MODEL)
    u_s, q_s, k_s, v_s, w_in_b = _project_rows(xs, g_mix2, w_in[0], PROJ_ROWS)
    u_p, q_p, k_all, v_all, kb_p, vb_p = _project_prompt(x_prompt, meta, g_mix2, w_in_b, prompt_rows)

    b_p = _attn_prompt(q_p, kb_p, vb_p, diag_b, meta_b, lam, g_sub2, batch, seq)
    b_s = _attn_sample(q_s, k_s, v_s,
                       cache_k[0].reshape(dec_batch, n_cache * HEADS, HEAD_DIM),
                       cache_v[0].reshape(dec_batch, n_cache * HEADS, HEAD_DIM),
                       tail_b, lam, g_sub2, dec_batch, t)

    history = jnp.pad(state_pool[0], ((0, 0), (POOL_HISTORY - POOL_STATE, 0), (0, 0)))
    h_s, w_out_b = _out_sample(u_s, history, b_s, xs, w_pool_b, ps2, w_out[0], t)
    h_p = _out_prompt(u_p, b_p, xp, w_pool_b, ps2, w_out_b, batch, seq)

    y_s, wg_b, wu_b, wd_b = _ffn(h_s, g_ffn2, w_gate[0], w_up[0], w_down[0], g_fin2, FFN_COLS_F32)
    y_p = _ffn(h_p, g_ffn2, wg_b, wu_b, wd_b, g_fin2, FFN_COLS)

    hd = (HEADS, HEAD_DIM)
    return (
        y_p.reshape(batch, seq, D_MODEL),
        y_s.reshape(dec_batch, t, D_MODEL),
        k_all.reshape(1, batch, N_META + seq, *hd),
        v_all.reshape(1, batch, N_META + seq, *hd),
        u_p[None, :, N_META + seq - POOL_STATE:],
        k_s.reshape(1, dec_batch, t, *hd),
        v_s.reshape(1, dec_batch, t, *hd),
        u_s.reshape(dec_batch, t, POOL_WIDTH)[None, :, t - POOL_STATE:],
    )
```

```python
import functools
import math

import jax
import jax.numpy as jnp
from jax import lax
from jax.experimental import pallas as pl
from jax.experimental.pallas import tpu as pltpu

F32 = jnp.float32
BF16 = jnp.bfloat16

D_MODEL = 2048
CHUNK = 64
N_META = 16
POOL_WINDOWS = (2, 4, 8, 16)
POOL_GROUP_DIM = 256
POOL_WIDTH = 1024
POOL_STATE = 15
POOL_HISTORY = 16
HEADS = 8
HEAD_DIM = 128
HALF_DIM = 64
ATTN_WIDTH = 1024
ATT_SCALE = HALF_DIM ** -0.5
LOG2E = math.log2(math.e)
D_FF = 5632
EPS = 1e-6
SUBLN_EPS = 1e-5
NEG_INF = -1e30
LAMBDA_INIT = 0.8 - 0.6 * math.exp(-0.3 * 0)

_BUCKET_THRESHOLDS = (1, 2, 3, 4, 5, 6, 7, 8, 12, 16, 23, 32, 46, 64, 91)

Q_TILE = 256
SOFTMAX_ROWS = 32
PROJ_ROWS = 512
PROMPT_TILES = 3
FFN_ROWS = 1024
FFN_COLS = 512
FFN_COLS_F32 = 256
SAMPLE_TAIL = 272
VMEM_LIMIT = 56 * 1024 * 1024
BF16_ROWS = 16


def _rms(x, g, eps):
    ms = jnp.mean(x * x, axis=-1, keepdims=True)
    return x * lax.rsqrt(ms + eps) * g


def _dot(a, b):
    return jnp.dot(a, b, preferred_element_type=F32)


def _dot_nt(a, b):
    return lax.dot_general(a, b, (((1,), (1,)), ((), ())), preferred_element_type=F32)


def _resident(shape):
    nd = len(shape)
    return pl.BlockSpec(shape, lambda *_: (0,) * nd, pipeline_mode=pl.Buffered(1))


def _smem():
    return pl.BlockSpec(memory_space=pltpu.SMEM)


def _bias_of_rel(rel, table_ref, h):
    n = jnp.abs(rel)
    far = table_ref[len(_BUCKET_THRESHOLDS), h]

    def entry(b):
        return (table_ref[b, h] - far) * LOG2E

    neg = jnp.full(rel.shape, entry(0), F32)
    pos = jnp.full(rel.shape, entry(16), F32)
    for b, thr in enumerate(_BUCKET_THRESHOLDS, start=1):
        ge = n >= thr
        neg = jnp.where(ge, entry(b), neg)
        pos = jnp.where(ge, entry(16 + b), pos)
    return jnp.where(rel > 0, pos, neg)


def _bias_kernel(table_ref, lq1_ref, lk1_ref, lq2_ref, lk2_ref,
                 diag_ref, meta_ref, tail_ref, lam_ref):
    h = pl.program_id(0)
    t = Q_TILE
    qi = lax.broadcasted_iota(jnp.int32, (t, t), 0)
    kj = lax.broadcasted_iota(jnp.int32, (t, t), 1)
    visible = (kj // CHUNK) <= (qi // CHUNK)
    diag_ref[0, 0] = jnp.where(visible, _bias_of_rel(kj - qi, table_ref, h), NEG_INF)
    diag_ref[0, 1] = _bias_of_rel(kj - t - qi, table_ref, h)
    qm = lax.broadcasted_iota(jnp.int32, (t, N_META), 0)
    km = lax.broadcasted_iota(jnp.int32, (t, N_META), 1)
    meta_ref[0] = _bias_of_rel(km - N_META - qm, table_ref, h)
    qs = lax.broadcasted_iota(jnp.int32, tail_ref.shape[1:], 0)
    ks = lax.broadcasted_iota(jnp.int32, tail_ref.shape[1:], 1)
    tail_ref[0] = _bias_of_rel(ks - SAMPLE_TAIL - qs, table_ref, h)
    s1 = jnp.sum(lq1_ref[...] * lk1_ref[...], axis=-1, keepdims=True)
    s2 = jnp.sum(lq2_ref[...] * lk2_ref[...], axis=-1, keepdims=True)
    lam_ref[...] = jnp.exp(s1) - jnp.exp(s2) + LAMBDA_INIT


def _bias_tables(rel_bias, lq1, lk1, lq2, lk2, n_new):
    tail_w = SAMPLE_TAIL + n_new
    vec = pl.BlockSpec((1, HALF_DIM), lambda h: (0, 0))
    return pl.pallas_call(
        _bias_kernel,
        grid=(HEADS,),
        in_specs=[_smem(), vec, vec, vec, vec],
        out_specs=[
            pl.BlockSpec((1, 2, Q_TILE, Q_TILE), lambda h: (h, 0, 0, 0)),
            pl.BlockSpec((1, Q_TILE, N_META), lambda h: (h, 0, 0)),
            pl.BlockSpec((1, n_new, tail_w), lambda h: (h, 0, 0)),
            pl.BlockSpec((1, 1), lambda h: (0, 0)),
        ],
        out_shape=[
            jax.ShapeDtypeStruct((HEADS, 2, Q_TILE, Q_TILE), F32),
            jax.ShapeDtypeStruct((HEADS, Q_TILE, N_META), F32),
            jax.ShapeDtypeStruct((HEADS, n_new, tail_w), F32),
            jax.ShapeDtypeStruct((1, 1), F32),
        ],
        name="bias_tables",
    )(rel_bias, lq1, lk1, lq2, lk2)


def _store_heads(o_ref, x, row0=0):
    rows = x.shape[0]
    for h in range(HEADS):
        o_ref[pl.ds(row0 * HEADS + h, rows, stride=HEADS), :] = x[:, h * HEAD_DIM:(h + 1) * HEAD_DIM]


def _store_column(c, y, u_ref, q_ref, k_ref, v_ref, row0=0, by_head=None):
    rows = slice(row0, row0 + y.shape[0])
    if c == 0:
        u_ref[rows, :] = y
    elif c == 1:
        q = (y * (ATT_SCALE * LOG2E)).astype(BF16)
        if by_head is None:
            q_ref[rows, :] = q
        else:
            for h in range(HEADS):
                q_ref[h, rows, :] = q[:, h * HEAD_DIM:(h + 1) * HEAD_DIM]
    else:
        _store_heads(k_ref if c == 2 else v_ref, y, row0)
        if by_head is not None:
            for h in range(HEADS):
                by_head[c - 2][h, rows, :] = y[:, h * HEAD_DIM:(h + 1) * HEAD_DIM].astype(BF16)


def _project_store(xn, w_ref, u_ref, q_ref, k_ref, v_ref, row0=0, by_head=None):
    w = POOL_WIDTH
    for c in range(4):
        _store_column(c, _dot(xn, w_ref[:, c * w:(c + 1) * w]), u_ref, q_ref, k_ref, v_ref, row0, by_head)


def _proj_rows_kernel(x_ref, g_ref, w_ref, u_ref, q_ref, k_ref, v_ref, wb_ref, xn_ref):
    j = pl.program_id(0)
    rows = x_ref.shape[0]
    tile = pl.ds(pl.multiple_of(pl.program_id(1) * rows, rows), rows)

    @pl.when(j == 0)
    def _():
        xn_ref[tile, :] = _rms(x_ref[...], g_ref[...], EPS).astype(BF16)

    wb = w_ref[...].astype(BF16)
    wb_ref[...] = wb
    y = _dot(xn_ref[tile, :], wb)
    for c in range(4):
        @pl.when(j == c)
        def _(c=c):
            _store_column(c, y, u_ref, q_ref, k_ref, v_ref)


def _proj_prompt_kernel(x_ref, meta_ref, g_ref, w_ref, u_ref, q_ref, k_ref, v_ref, kb_ref, vb_ref,
                        xn_ref, carry_ref):
    i = pl.program_id(1)
    g = g_ref[...]
    rows = xn_ref.shape[0]
    split = -(-(rows // 2) // BF16_ROWS) * BF16_ROWS

    @pl.when(i == 0)
    def _():
        xn_ref[0:N_META, :] = _rms(meta_ref[...], g, EPS).astype(BF16)

    @pl.when(i > 0)
    def _():
        xn_ref[0:N_META, :] = carry_ref[...]

    xn_ref[N_META:split, :] = _rms(x_ref[0:split - N_META, :], g, EPS).astype(BF16)
    by_head = (kb_ref, vb_ref)
    _project_store(xn_ref[0:split, :], w_ref, u_ref, q_ref, k_ref, v_ref, 0, by_head)
    xn_ref[split:rows, :] = _rms(x_ref[split - N_META:rows - N_META, :], g, EPS).astype(BF16)
    _project_store(xn_ref[split:rows, :], w_ref, u_ref, q_ref, k_ref, v_ref, split, by_head)

    @pl.when(i < pl.num_programs(1) - 1)
    def _():
        carry_ref[...] = _rms(x_ref[rows - N_META:rows, :], g, EPS).astype(BF16)


def _proj_out_shapes(lead, n):
    return [jax.ShapeDtypeStruct((*lead, n, POOL_WIDTH), F32),
            jax.ShapeDtypeStruct((*lead, n, POOL_WIDTH), BF16),
            jax.ShapeDtypeStruct((*lead, n * HEADS, HEAD_DIM), F32),
            jax.ShapeDtypeStruct((*lead, n * HEADS, HEAD_DIM), F32)]


def _project_rows(x, g, w_f32, rows):
    n = x.shape[0]
    w = POOL_WIDTH
    last = n // rows - 1

    def rows_of(c):
        return lambda j, i: (jnp.where(j < c, 0, jnp.where(j == c, i, last)), 0)

    uq = [pl.BlockSpec((rows, w), rows_of(c)) for c in (0, 1)]
    kv = [pl.BlockSpec((rows * HEADS, HEAD_DIM), rows_of(c)) for c in (2, 3)]
    w_spec = pl.BlockSpec((D_MODEL, w), lambda j, i: (0, j))
    return pl.pallas_call(
        _proj_rows_kernel,
        grid=(w_f32.shape[1] // w, n // rows),
        in_specs=[pl.BlockSpec((rows, D_MODEL), rows_of(0)), _resident((1, D_MODEL)), w_spec],
        out_specs=[*uq, *kv, w_spec],
        out_shape=_proj_out_shapes((), n) + [jax.ShapeDtypeStruct(w_f32.shape, BF16)],
        scratch_shapes=[pltpu.VMEM((n, D_MODEL), BF16)],
        compiler_params=pltpu.CompilerParams(
            dimension_semantics=("arbitrary", "arbitrary"), vmem_limit_bytes=VMEM_LIMIT),
        name="in_proj_rows",
    )(x, g, w_f32)


def _project_prompt(x, meta, g, w_bf16, rows):
    batch, seq, _ = x.shape
    total = N_META + seq
    x_spec = pl.BlockSpec((None, rows, D_MODEL), lambda b, i: (b, i, 0))
    uq = pl.BlockSpec((None, rows, POOL_WIDTH), lambda b, i: (b, i, 0))
    kv = pl.BlockSpec((None, rows * HEADS, HEAD_DIM), lambda b, i: (b, i, 0))
    kvb = pl.BlockSpec((None, HEADS, rows, HEAD_DIM), lambda b, i: (b, 0, i, 0))
    u_shape, _, k_shape, v_shape = _proj_out_shapes((batch,), total)
    by_head = jax.ShapeDtypeStruct((batch, HEADS, total, HEAD_DIM), BF16)
    return pl.pallas_call(
        _proj_prompt_kernel,
        grid=(batch, total // rows),
        in_specs=[x_spec, _resident(meta.shape), _resident((1, D_MODEL)), _resident(w_bf16.shape)],
        out_specs=[uq, kvb, kv, kv, kvb, kvb],
        out_shape=[u_shape, by_head, k_shape, v_shape, by_head, by_head],
        scratch_shapes=[pltpu.VMEM((rows, D_MODEL), BF16), pltpu.VMEM((N_META, D_MODEL), BF16)],
        compiler_params=pltpu.CompilerParams(
            dimension_semantics=("parallel", "arbitrary"), vmem_limit_bytes=VMEM_LIMIT),
        name="in_proj_prompt",
    )(x, meta, g, w_bf16)


def _stack_maps(q):
    lane = lax.broadcasted_iota(jnp.int32, q.shape, 1)
    zero = jnp.zeros_like(q)
    return jnp.concatenate(
        [jnp.where(lane < HALF_DIM, q, zero), jnp.where(lane >= HALF_DIM, q, zero)], axis=0)


def _finish_head(o, gs):
    return (_rms(o, gs, SUBLN_EPS) * (1.0 - LAMBDA_INIT)).astype(BF16)


def _attn_prompt_kernel(lam_ref, q_ref, kb_ref, vb_ref, diag_ref, meta_ref, gs_ref,
                        o_ref, q2_ref, s_ref, sm_ref, w_ref, wm_ref, rl_ref):
    tq = Q_TILE
    nt = o_ref.shape[0] // tq
    rb = SOFTMAX_ROWS
    lam = lam_ref[0, 0]

    def score_steps(c):
        slot = c % 2

        def stack():
            q2_ref[slot] = _stack_maps(q_ref[N_META + c * tq:N_META + (c + 1) * tq, :])

        def frames(j):
            d = _dot_nt(q2_ref[slot], kb_ref[N_META + j * tq:N_META + (j + 1) * tq, :])
            cols = slice(j * tq, (j + 1) * tq)
            if j >= c - 1:
                bias = diag_ref[0, c - j]
                s_ref[slot, :tq, cols] = d[:tq] + bias
                s_ref[slot, tq:, cols] = d[tq:] + bias
            else:
                s_ref[slot, :, cols] = d

        def metas():
            sm = _dot_nt(q2_ref[slot], kb_ref[0:N_META, :])
            if c == 0:
                sm = sm + jnp.concatenate([meta_ref[0], meta_ref[0]], axis=0)
            sm_ref[slot] = sm

        return [stack, metas] + [functools.partial(frames, j) for j in range(c + 1)]

    def softmax_step(c, i):
        slot = c % 2
        n = (c + 1) * tq
        last_chunk = ((i + 1) * rb - 1) // CHUNK
        seen = c * tq + -(-(last_chunk + 1) * CHUNK // HEAD_DIM) * HEAD_DIM
        rows = (slice(i * rb, (i + 1) * rb), slice(tq + i * rb, tq + (i + 1) * rb))
        p, pm, l = [], [], []
        for rmap in rows:
            sm = sm_ref[slot, rmap, :]
            m = jnp.maximum(jnp.max(s_ref[slot, rmap, :seen], axis=-1, keepdims=True),
                            jnp.max(sm, axis=-1, keepdims=True))
            p.append(jnp.exp2(s_ref[slot, rmap, :seen] - m))
            pm.append(jnp.exp2(sm - m))
            l.append(jnp.sum(p[-1], axis=-1, keepdims=True) + jnp.sum(pm[-1], axis=-1, keepdims=True))
        rho = lam * l[0] / l[1]
        w_ref[slot, rows[0], :seen] = (p[0] - p[1] * rho).astype(BF16)
        if seen < n:
            w_ref[slot, rows[0], seen:n] = jnp.zeros((rb, n - seen), BF16)
        wm_ref[slot, rows[0], :] = (pm[0] - pm[1] * rho).astype(BF16)
        rl_ref[slot, rows[0], :] = 1.0 / l[0]

    def value_step(c):
        slot = c % 2
        n = (c + 1) * tq
        o = _dot(w_ref[slot, :, :n], vb_ref[N_META:N_META + n, :]) + _dot(wm_ref[slot], vb_ref[0:N_META, :])
        o_ref[c * tq:(c + 1) * tq, :] = _finish_head(o * rl_ref[slot], gs_ref[...])

    for step in score_steps(0):
        step()
    for c in range(nt):
        ahead = score_steps(c + 1) if c + 1 < nt else []
        blocks = tq // rb
        for i in range(blocks):
            softmax_step(c, i)
            for step in ahead[i * len(ahead) // blocks:(i + 1) * len(ahead) // blocks]:
                step()
        value_step(c)


def _attn_prompt(q, kb, vb, diag, meta_b, lam, gs, batch, seq):
    tq = Q_TILE
    kv = pl.BlockSpec((None, None, N_META + seq, HEAD_DIM), lambda b, h: (b, h, 0, 0))
    qo = pl.BlockSpec((None, None, seq, HEAD_DIM), lambda b, h: (b, h, 0, 0))
    return pl.pallas_call(
        _attn_prompt_kernel,
        grid=(batch, HEADS),
        in_specs=[
            _smem(), kv, kv, kv,
            pl.BlockSpec((1, 2, tq, tq), lambda b, h: (h, 0, 0, 0)),
            pl.BlockSpec((1, tq, N_META), lambda b, h: (h, 0, 0)),
            pl.BlockSpec((1, HEAD_DIM), lambda b, h: (0, 0)),
        ],
        out_specs=qo,
        out_shape=jax.ShapeDtypeStruct((batch, HEADS, seq, HEAD_DIM), BF16),
        scratch_shapes=[
            pltpu.VMEM((2, 2 * tq, HEAD_DIM), BF16),
            pltpu.VMEM((2, 2 * tq, seq), F32),
            pltpu.VMEM((2, 2 * tq, N_META), F32),
            pltpu.VMEM((2, tq, seq), BF16),
            pltpu.VMEM((2, tq, N_META), BF16),
            pltpu.VMEM((2, tq, 1), F32),
        ],
        compiler_params=pltpu.CompilerParams(
            dimension_semantics=("parallel", "parallel"), vmem_limit_bytes=VMEM_LIMIT),
        name="attn_prompt",
    )(lam, q, kb, vb, diag, meta_b, gs)


def _attn_sample_kernel(lam_ref, q_ref, kn_ref, vn_ref, ck_ref, cv_ref, tail_ref, gs_ref,
                        o_ref, qd_ref, s_ref, w_ref):
    t = q_ref.shape[0]
    n_cache = ck_ref.shape[1] // HEADS
    split = n_cache - SAMPLE_TAIL
    width = n_cache + t
    lam = lam_ref[0, 0]
    hd = HEAD_DIM

    def gather(c_ref, n_ref, h):
        far = pl.ds(h, split, stride=HEADS)
        near = pl.ds(split * HEADS + h, SAMPLE_TAIL, stride=HEADS)
        new = pl.ds(h, t, stride=HEADS)
        return (c_ref[0, far, :].astype(BF16),
                jnp.concatenate([c_ref[0, near, :].astype(BF16), n_ref[new, :].astype(BF16)], axis=0))

    def pair_rows(c_ref, n_ref, hp):
        fa, ta = gather(c_ref, n_ref, 2 * hp)
        fb, tb = gather(c_ref, n_ref, 2 * hp + 1)
        return jnp.concatenate([fa, fb], axis=1), jnp.concatenate([ta, tb], axis=1)

    def score_steps(hp):
        slot = hp % 2

        def queries():
            zero = jnp.zeros((2 * t, hd), BF16)
            qa = _stack_maps(q_ref[:, 2 * hp * hd:(2 * hp + 1) * hd])
            qb = _stack_maps(q_ref[:, (2 * hp + 1) * hd:(2 * hp + 2) * hd])
            qd_ref[slot] = jnp.concatenate(
                [jnp.concatenate([qa, zero], axis=1), jnp.concatenate([zero, qb], axis=1)], axis=0)

        def scores():
            k_far, k_tail = pair_rows(ck_ref, kn_ref, hp)
            s_ref[slot, :, 0:split] = _dot_nt(qd_ref[slot], k_far)
            ba, bb = tail_ref[2 * hp], tail_ref[2 * hp + 1]
            s_ref[slot, :, split:width] = (
                _dot_nt(qd_ref[slot], k_tail) + jnp.concatenate([ba, ba, bb, bb], axis=0))

        return [queries, scores]

    def softmax_step(hp, j):
        slot = hp % 2
        r, p = [], []
        for rows in (slice(2 * t * j, 2 * t * j + t), slice(2 * t * j + t, 2 * t * (j + 1))):
            m = jnp.max(s_ref[slot, rows, 0:width], axis=-1, keepdims=True)
            p.append(jnp.exp2(s_ref[slot, rows, 0:width] - m))
            r.append(1.0 / jnp.sum(p[-1], axis=-1, keepdims=True))
        w_ref[slot, t * j:t * (j + 1), 0:width] = (p[0] * r[0] - p[1] * (lam * r[1])).astype(BF16)

    def value_step(hp):
        slot = hp % 2
        v_far, v_tail = pair_rows(cv_ref, vn_ref, hp)
        o = _dot(w_ref[slot, :, 0:split], v_far) + _dot(w_ref[slot, :, split:width], v_tail)
        gs = gs_ref[...]
        o_ref[:, 2 * hp * hd:(2 * hp + 1) * hd] = _finish_head(o[0:t, 0:hd], gs)
        o_ref[:, (2 * hp + 1) * hd:(2 * hp + 2) * hd] = _finish_head(o[t:2 * t, hd:2 * hd], gs)

    pairs = HEADS // 2
    for step in score_steps(0):
        step()
    for hp in range(pairs):
        ahead = score_steps(hp + 1) if hp + 1 < pairs else [lambda: None, lambda: None]
        softmax_step(hp, 0)
        ahead[0]()
        ahead[1]()
        softmax_step(hp, 1)
        value_step(hp)


def _attn_sample(q, k_new, v_new, cache_k, cache_v, tail_b, lam, gs, batch, t):
    n_cache = cache_k.shape[1] // HEADS
    row = pl.BlockSpec((t, ATTN_WIDTH), lambda b: (b, 0))
    new = pl.BlockSpec((t * HEADS, HEAD_DIM), lambda b: (b, 0))
    cache = pl.BlockSpec((1,) + cache_k.shape[1:], lambda b: (b, 0, 0))
    width = -(-(n_cache + t) // HEAD_DIM) * HEAD_DIM
    return pl.pallas_call(
        _attn_sample_kernel,
        grid=(batch,),
        in_specs=[_smem(), row, new, new, cache, cache,
                  _resident(tail_b.shape), _resident((1, HEAD_DIM))],
        out_specs=row,
        out_shape=jax.ShapeDtypeStruct((batch * t, ATTN_WIDTH), BF16),
        scratch_shapes=[
            pltpu.VMEM((2, 4 * t, 2 * HEAD_DIM), BF16),
            pltpu.VMEM((2, 4 * t, width), F32),
            pltpu.VMEM((2, 2 * t, width), BF16),
        ],
        compiler_params=pltpu.CompilerParams(
            dimension_semantics=("parallel",), vmem_limit_bytes=VMEM_LIMIT),
        name="attn_sample",
    )(lam, q, k_new, v_new, cache_k, cache_v, tail_b, gs)


def _pool_mix(ext, wp_ref, scale):
    outs = []
    for g, w in enumerate(POOL_WINDOWS):
        xg = ext[:, g * POOL_GROUP_DIM:(g + 1) * POOL_GROUP_DIM]
        s = xg
        span = 1
        while span < w:
            s = s + pltpu.roll(s, span, axis=0)
            span *= 2
        p = (s[POOL_HISTORY:] * (1.0 / w) - xg[POOL_HISTORY:]).astype(BF16)
        outs.append(_dot(p, wp_ref[g]))
    return jnp.concatenate(outs, axis=-1) * scale


def _mix_out(a, b_ref, x_ref, wo, o_ref):
    mix = jnp.concatenate([a.astype(BF16), b_ref[...]], axis=-1)
    o_ref[...] = x_ref[...] + _dot(mix, wo)


def _out_prompt_kernel(u_ref, b_ref, x_ref, wp_ref, ps_ref, wo_ref, o_ref):
    half = x_ref.shape[0] // 2
    for r in (0, half):
        a = _pool_mix(u_ref[r:r + POOL_HISTORY + half, :], wp_ref, ps_ref[...])
        b = [b_ref[h, r:r + half, :] for h in range(HEADS)]
        mix = jnp.concatenate([a.astype(BF16), *b], axis=-1)
        o_ref[r:r + half, :] = x_ref[r:r + half, :] + _dot(mix, wo_ref[...])


def _out_sample_kernel(u_ref, state_ref, b_ref, x_ref, wp_ref, ps_ref, wo_ref, o_ref, wob_ref):
    n_seq, t = state_ref.shape[0], u_ref.shape[0] // state_ref.shape[0]
    parts = []
    for s in range(n_seq):
        ext = jnp.concatenate([state_ref[s], u_ref[s * t:(s + 1) * t, :]], axis=0)
        parts.append(_pool_mix(ext, wp_ref, ps_ref[...]))
    wo = wo_ref[...].astype(BF16)
    wob_ref[...] = wo
    _mix_out(jnp.concatenate(parts, axis=0), b_ref, x_ref, wo, o_ref)


def _out_common_specs(rows, idx):
    return (pl.BlockSpec((rows, ATTN_WIDTH), idx), pl.BlockSpec((rows, D_MODEL), idx))


def _out_prompt(u, b, x, wp, ps, wo, batch, seq):
    rows = PROJ_ROWS
    nt = seq // rows
    idx = lambda bi, i: (bi * nt + i, 0)
    u_spec = pl.BlockSpec(
        (pl.Squeezed(), pl.Element(POOL_HISTORY + rows), pl.Element(POOL_WIDTH)),
        lambda bi, i: (bi, N_META - POOL_HISTORY + i * rows, 0))
    _, x_spec = _out_common_specs(rows, idx)
    b_spec = pl.BlockSpec((None, HEADS, rows, HEAD_DIM), lambda bi, i: (bi, 0, i, 0))
    return pl.pallas_call(
        _out_prompt_kernel,
        grid=(batch, nt),
        in_specs=[u_spec, b_spec, x_spec,
                  _resident(wp.shape), _resident(ps.shape), _resident(wo.shape)],
        out_specs=x_spec,
        out_shape=jax.ShapeDtypeStruct(x.shape, F32),
        compiler_params=pltpu.CompilerParams(
            dimension_semantics=("parallel", "parallel"), vmem_limit_bytes=VMEM_LIMIT),
        name="out_proj_prompt",
    )(u, b, x, wp, ps, wo)


def _out_sample(u, history, b, x, wp, ps, wo, t):
    rows = PROJ_ROWS
    n_seq = rows // t
    idx = lambda i: (i, 0)
    b_spec, x_spec = _out_common_specs(rows, idx)
    return pl.pallas_call(
        _out_sample_kernel,
        grid=(x.shape[0] // rows,),
        in_specs=[
            pl.BlockSpec((rows, POOL_WIDTH), idx),
            pl.BlockSpec((n_seq, POOL_HISTORY, POOL_WIDTH), lambda i: (i, 0, 0)),
            b_spec, x_spec,
            _resident(wp.shape), _resident(ps.shape), _resident(wo.shape),
        ],
        out_specs=[x_spec, pl.BlockSpec(wo.shape, lambda i: (0, 0))],
        out_shape=[jax.ShapeDtypeStruct(x.shape, F32), jax.ShapeDtypeStruct(wo.shape, BF16)],
        compiler_params=pltpu.CompilerParams(
            dimension_semantics=("arbitrary",), vmem_limit_bytes=VMEM_LIMIT),
        name="out_proj_sample",
    )(u, history, b, x, wp, ps, wo)


def _ffn_kernel(h_ref, gf_ref, wg_ref, wu_ref, wd_ref, gfin_ref, y_ref, *rest):
    n_ref = rest[-1]
    f = pl.program_id(1)

    @pl.when(f == 0)
    def _():
        h = h_ref[...]
        n_ref[...] = _rms(h, gf_ref[...], EPS).astype(BF16)
        y_ref[...] = h

    weights = [wg_ref[...], wu_ref[...], wd_ref[...]]
    if len(rest) > 1:
        weights = [w.astype(BF16) for w in weights]
        for out_ref, w in zip(rest[:3], weights):
            out_ref[...] = w
    wg, wu, wd = weights
    n = n_ref[...]
    g = _dot(n, wg)
    u = _dot(n, wu)
    act = (g * (1.0 / (1.0 + jnp.exp(-g))) * u).astype(BF16)
    y_ref[...] += _dot(act, wd)

    @pl.when(f == pl.num_programs(1) - 1)
    def _():
        y_ref[...] = _rms(y_ref[...], gfin_ref[...], EPS)


def _ffn(h, gf, wg, wu, wd, gfin, cols):
    rows = FFN_ROWS
    emit = wg.dtype == F32
    row = pl.BlockSpec((rows, D_MODEL), lambda i, f: (i, 0))
    vec = pl.BlockSpec((1, D_MODEL), lambda i, f: (0, 0))
    w_in = pl.BlockSpec((D_MODEL, cols), lambda i, f: (0, f))
    w_out = pl.BlockSpec((cols, D_MODEL), lambda i, f: (f, 0))
    out_specs, out_shape = [row], [jax.ShapeDtypeStruct(h.shape, F32)]
    if emit:
        out_specs += [w_in, w_in, w_out]
        out_shape += [jax.ShapeDtypeStruct(w.shape, BF16) for w in (wg, wu, wd)]
    outs = pl.pallas_call(
        _ffn_kernel,
        grid=(h.shape[0] // rows, D_FF // cols),
        in_specs=[row, vec, w_in, w_in, w_out, vec],
        out_specs=out_specs,
        out_shape=out_shape,
        scratch_shapes=[pltpu.VMEM((rows, D_MODEL), BF16)],
        compiler_params=pltpu.CompilerParams(
            dimension_semantics=("arbitrary" if emit else "parallel", "arbitrary"),
            vmem_limit_bytes=VMEM_LIMIT),
        name="swiglu_f32w" if emit else "swiglu",
    )(h, gf, wg, wu, wd, gfin)
    return outs if emit else outs[0]


def kernel(x_prompt, x_sample, cache_k, cache_v, state_pool, meta, g_mix, w_in, w_pool,
           pool_scale, lambda_q1, lambda_k1, lambda_q2, lambda_k2, g_subln, w_out, g_ffn,
           w_gate, w_up, w_down, rel_bias, g_final):
    batch, seq, d = x_prompt.shape
    dec_batch, t, _ = x_sample.shape
    depth = w_in.shape[0]
    assert depth == 1 and d == D_MODEL and meta.shape == (N_META, D_MODEL)
    assert seq % PROJ_ROWS == 0 and seq % Q_TILE == 0 and PROJ_ROWS % t == 0
    assert (dec_batch * t) % PROJ_ROWS == 0 and t >= POOL_STATE
    assert POOL_STATE == max(POOL_WINDOWS) - 1 <= POOL_HISTORY <= N_META
    assert (N_META + seq) % (PROMPT_TILES * BF16_ROWS) == 0
    prompt_rows = (N_META + seq) // PROMPT_TILES
    n_cache = cache_k.shape[2]
    assert min(Q_TILE, SAMPLE_TAIL) >= _BUCKET_THRESHOLDS[-1]
    assert (n_cache - N_META) % CHUNK == 0 and t <= CHUNK
    assert n_cache > SAMPLE_TAIL and (n_cache - SAMPLE_TAIL) % HEAD_DIM == 0

    w_pool_b = w_pool[0].astype(BF16)
    g_mix2 = g_mix[0].reshape(1, D_MODEL)
    g_ffn2 = g_ffn[0].reshape(1, D_MODEL)
    g_fin2 = g_final.reshape(1, D_MODEL)
    g_sub2 = g_subln[0].reshape(1, HEAD_DIM)
    ps2 = pool_scale[0].reshape(1, POOL_WIDTH)

    diag_b, meta_b, tail_b, lam = _bias_tables(
        rel_bias, lambda_q1, lambda_k1, lambda_q2, lambda_k2, t)

    xp = x_prompt.reshape(batch * seq, D_MODEL)
    xs = x_sample.reshape(dec_batch * t, D_MODEL)
    u_s, q_s, k_s, v_s, w_in_b = _project_rows(xs, g_mix2, w_in[0], PROJ_ROWS)
    u_p, q_p, k_all, v_all, kb_p, vb_p = _project_prompt(x_prompt, meta, g_mix2, w_in_b, prompt_rows)

    b_p = _attn_prompt(q_p, kb_p, vb_p, diag_b, meta_b, lam, g_sub2, batch, seq)
    b_s = _attn_sample(q_s, k_s, v_s,
                       cache_k[0].reshape(dec_batch, n_cache * HEADS, HEAD_DIM),
                       cache_v[0].reshape(dec_batch, n_cache * HEADS, HEAD_DIM),
                       tail_b, lam, g_sub2, dec_batch, t)

    history = jnp.pad(state_pool[0], ((0, 0), (POOL_HISTORY - POOL_STATE, 0), (0, 0)))
    h_s, w_out_b = _out_sample(u_s, history, b_s, xs, w_pool_b, ps2, w_out[0], t)
    h_p = _out_prompt(u_p, b_p, xp, w_pool_b, ps2, w_out_b, batch, seq)

    y_s, wg_b, wu_b, wd_b = _ffn(h_s, g_ffn2, w_gate[0], w_up[0], w_down[0], g_fin2, FFN_COLS_F32)
    y_p = _ffn(h_p, g_ffn2, wg_b, wu_b, wd_b, g_fin2, FFN_COLS)

    hd = (HEADS, HEAD_DIM)
    return (
        y_p.reshape(batch, seq, D_MODEL),
        y_s.reshape(dec_batch, t, D_MODEL),
        k_all.reshape(1, batch, N_META + seq, *hd),
        v_all.reshape(1, batch, N_META + seq, *hd),
        u_p[None, :, N_META + seq - POOL_STATE:],
        k_s.reshape(1, dec_batch, t, *hd),
        v_s.reshape(1, dec_batch, t, *hd),
        u_s.reshape(dec_batch, t, POOL_WIDTH)[None, :, t - POOL_STATE:],
    )
```

```python
import functools
import math

import jax
import jax.numpy as jnp
from jax import lax
from jax.experimental import pallas as pl
from jax.experimental.pallas import tpu as pltpu

F32 = jnp.float32
BF16 = jnp.bfloat16

D_MODEL = 2048
CHUNK = 64
N_META = 16
POOL_WINDOWS = (2, 4, 8, 16)
POOL_GROUP_DIM = 256
POOL_WIDTH = 1024
POOL_STATE = 15
POOL_HISTORY = 16
HEADS = 8
HEAD_DIM = 128
HALF_DIM = 64
ATTN_WIDTH = 1024
ATT_SCALE = HALF_DIM ** -0.5
LOG2E = math.log2(math.e)
D_FF = 5632
EPS = 1e-6
SUBLN_EPS = 1e-5
NEG_INF = -1e30
LAMBDA_INIT = 0.8 - 0.6 * math.exp(-0.3 * 0)

_BUCKET_THRESHOLDS = (1, 2, 3, 4, 5, 6, 7, 8, 12, 16, 23, 32, 46, 64, 91)

Q_TILE = 256
SOFTMAX_ROWS = 32
PROJ_ROWS = 512
PROMPT_TILES = 3
FFN_ROWS = 1024
FFN_COLS = 512
FFN_COLS_F32 = 256
SAMPLE_TAIL = 272
VMEM_LIMIT = 56 * 1024 * 1024
BF16_ROWS = 16


def _rms(x, g, eps):
    ms = jnp.mean(x * x, axis=-1, keepdims=True)
    return x * lax.rsqrt(ms + eps) * g


def _dot(a, b):
    return jnp.dot(a, b, preferred_element_type=F32)


def _dot_nt(a, b):
    return lax.dot_general(a, b, (((1,), (1,)), ((), ())), preferred_element_type=F32)


def _resident(shape):
    nd = len(shape)
    return pl.BlockSpec(shape, lambda *_: (0,) * nd, pipeline_mode=pl.Buffered(1))


def _smem():
    return pl.BlockSpec(memory_space=pltpu.SMEM)


def _bias_of_rel(rel, table_ref, h):
    n = jnp.abs(rel)
    far = table_ref[len(_BUCKET_THRESHOLDS), h]

    def entry(b):
        return (table_ref[b, h] - far) * LOG2E

    neg = jnp.full(rel.shape, entry(0), F32)
    pos = jnp.full(rel.shape, entry(16), F32)
    for b, thr in enumerate(_BUCKET_THRESHOLDS, start=1):
        ge = n >= thr
        neg = jnp.where(ge, entry(b), neg)
        pos = jnp.where(ge, entry(16 + b), pos)
    return jnp.where(rel > 0, pos, neg)


def _bias_kernel(table_ref, lq1_ref, lk1_ref, lq2_ref, lk2_ref,
                 diag_ref, meta_ref, tail_ref, lam_ref):
    h = pl.program_id(0)
    t = Q_TILE
    qi = lax.broadcasted_iota(jnp.int32, (t, t), 0)
    kj = lax.broadcasted_iota(jnp.int32, (t, t), 1)
    visible = (kj // CHUNK) <= (qi // CHUNK)
    diag_ref[0, 0] = jnp.where(visible, _bias_of_rel(kj - qi, table_ref, h), NEG_INF)
    diag_ref[0, 1] = _bias_of_rel(kj - t - qi, table_ref, h)
    qm = lax.broadcasted_iota(jnp.int32, (t, N_META), 0)
    km = lax.broadcasted_iota(jnp.int32, (t, N_META), 1)
    meta_ref[0] = _bias_of_rel(km - N_META - qm, table_ref, h)
    qs = lax.broadcasted_iota(jnp.int32, tail_ref.shape[1:], 0)
    ks = lax.broadcasted_iota(jnp.int32, tail_ref.shape[1:], 1)
    tail_ref[0] = _bias_of_rel(ks - SAMPLE_TAIL - qs, table_ref, h)
    s1 = jnp.sum(lq1_ref[...] * lk1_ref[...], axis=-1, keepdims=True)
    s2 = jnp.sum(lq2_ref[...] * lk2_ref[...], axis=-1, keepdims=True)
    lam_ref[...] = jnp.exp(s1) - jnp.exp(s2) + LAMBDA_INIT


def _bias_tables(rel_bias, lq1, lk1, lq2, lk2, n_new):
    tail_w = SAMPLE_TAIL + n_new
    vec = pl.BlockSpec((1, HALF_DIM), lambda h: (0, 0))
    return pl.pallas_call(
        _bias_kernel,
        grid=(HEADS,),
        in_specs=[_smem(), vec, vec, vec, vec],
        out_specs=[
            pl.BlockSpec((1, 2, Q_TILE, Q_TILE), lambda h: (h, 0, 0, 0)),
            pl.BlockSpec((1, Q_TILE, N_META), lambda h: (h, 0, 0)),
            pl.BlockSpec((1, n_new, tail_w), lambda h: (h, 0, 0)),
            pl.BlockSpec((1, 1), lambda h: (0, 0)),
        ],
        out_shape=[
            jax.ShapeDtypeStruct((HEADS, 2, Q_TILE, Q_TILE), F32),
            jax.ShapeDtypeStruct((HEADS, Q_TILE, N_META), F32),
            jax.ShapeDtypeStruct((HEADS, n_new, tail_w), F32),
            jax.ShapeDtypeStruct((1, 1), F32),
        ],
        name="bias_tables",
    )(rel_bias, lq1, lk1, lq2, lk2)


def _store_heads(o_ref, x, row0=0):
    rows = x.shape[0]
    for h in range(HEADS):
        o_ref[pl.ds(row0 * HEADS + h, rows, stride=HEADS), :] = x[:, h * HEAD_DIM:(h + 1) * HEAD_DIM]


def _store_column(c, y, u_ref, q_ref, k_ref, v_ref, row0=0, by_head=None):
    rows = slice(row0, row0 + y.shape[0])
    if c == 0:
        u_ref[rows, :] = y
    elif c == 1:
        q = (y * (ATT_SCALE * LOG2E)).astype(BF16)
        if by_head is None:
            q_ref[rows, :] = q
        else:
            for h in range(HEADS):
                q_ref[h, rows, :] = q[:, h * HEAD_DIM:(h + 1) * HEAD_DIM]
    else:
        _store_heads(k_ref if c == 2 else v_ref, y, row0)
        if by_head is not None:
            for h in range(HEADS):
                by_head[c - 2][h, rows, :] = y[:, h * HEAD_DIM:(h + 1) * HEAD_DIM].astype(BF16)


def _project_store(xn, w_ref, u_ref, q_ref, k_ref, v_ref, row0=0, by_head=None):
    w = POOL_WIDTH
    for c in range(4):
        _store_column(c, _dot(xn, w_ref[:, c * w:(c + 1) * w]), u_ref, q_ref, k_ref, v_ref, row0, by_head)


def _proj_rows_kernel(x_ref, g_ref, w_ref, u_ref, q_ref, k_ref, v_ref, wb_ref, xn_ref):
    j = pl.program_id(0)
    rows = x_ref.shape[0]
    tile = pl.ds(pl.multiple_of(pl.program_id(1) * rows, rows), rows)

    @pl.when(j == 0)
    def _():
        xn_ref[tile, :] = _rms(x_ref[...], g_ref[...], EPS).astype(BF16)

    wb = w_ref[...].astype(BF16)
    wb_ref[...] = wb
    y = _dot(xn_ref[tile, :], wb)
    for c in range(4):
        @pl.when(j == c)
        def _(c=c):
            _store_column(c, y, u_ref, q_ref, k_ref, v_ref)


def _proj_prompt_kernel(x_ref, meta_ref, g_ref, w_ref, u_ref, q_ref, k_ref, v_ref, kb_ref, vb_ref,
                        xn_ref, carry_ref):
    i = pl.program_id(1)
    g = g_ref[...]
    rows = xn_ref.shape[0]
    split = -(-(rows // 2) // BF16_ROWS) * BF16_ROWS

    @pl.when(i == 0)
    def _():
        xn_ref[0:N_META, :] = _rms(meta_ref[...], g, EPS).astype(BF16)

    @pl.when(i > 0)
    def _():
        xn_ref[0:N_META, :] = carry_ref[...]

    xn_ref[N_META:split, :] = _rms(x_ref[0:split - N_META, :], g, EPS).astype(BF16)
    by_head = (kb_ref, vb_ref)
    _project_store(xn_ref[0:split, :], w_ref, u_ref, q_ref, k_ref, v_ref, 0, by_head)
    xn_ref[split:rows, :] = _rms(x_ref[split - N_META:rows - N_META, :], g, EPS).astype(BF16)
    _project_store(xn_ref[split:rows, :], w_ref, u_ref, q_ref, k_ref, v_ref, split, by_head)

    @pl.when(i < pl.num_programs(1) - 1)
    def _():
        carry_ref[...] = _rms(x_ref[rows - N_META:rows, :], g, EPS).astype(BF16)


def _proj_out_shapes(lead, n):
    return [jax.ShapeDtypeStruct((*lead, n, POOL_WIDTH), F32),
            jax.ShapeDtypeStruct((*lead, n, POOL_WIDTH), BF16),
            jax.ShapeDtypeStruct((*lead, n * HEADS, HEAD_DIM), F32),
            jax.ShapeDtypeStruct((*lead, n * HEADS, HEAD_DIM), F32)]


def _project_rows(x, g, w_f32, rows):
    n = x.shape[0]
    w = POOL_WIDTH
    last = n // rows - 1

    def rows_of(c):
        return lambda j, i: (jnp.where(j < c, 0, jnp.where(j == c, i, last)), 0)

    uq = [pl.BlockSpec((rows, w), rows_of(c)) for c in (0, 1)]
    kv = [pl.BlockSpec((rows * HEADS, HEAD_DIM), rows_of(c)) for c in (2, 3)]
    w_spec = pl.BlockSpec((D_MODEL, w), lambda j, i: (0, j))
    return pl.pallas_call(
        _proj_rows_kernel,
        grid=(w_f32.shape[1] // w, n // rows),
        in_specs=[pl.BlockSpec((rows, D_MODEL), rows_of(0)), _resident((1, D_MODEL)), w_spec],
        out_specs=[*uq, *kv, w_spec],
        out_shape=_proj_out_shapes((), n) + [jax.ShapeDtypeStruct(w_f32.shape, BF16)],
        scratch_shapes=[pltpu.VMEM((n, D_MODEL), BF16)],
        compiler_params=pltpu.CompilerParams(
            dimension_semantics=("arbitrary", "arbitrary"), vmem_limit_bytes=VMEM_LIMIT),
        name="in_proj_rows",
    )(x, g, w_f32)


def _project_prompt(x, meta, g, w_bf16, rows):
    batch, seq, _ = x.shape
    total = N_META + seq
    x_spec = pl.BlockSpec((None, rows, D_MODEL), lambda b, i: (b, i, 0))
    uq = pl.BlockSpec((None, rows, POOL_WIDTH), lambda b, i: (b, i, 0))
    kv = pl.BlockSpec((None, rows * HEADS, HEAD_DIM), lambda b, i: (b, i, 0))
    kvb = pl.BlockSpec((None, HEADS, rows, HEAD_DIM), lambda b, i: (b, 0, i, 0))
    u_shape, _, k_shape, v_shape = _proj_out_shapes((batch,), total)
    by_head = jax.ShapeDtypeStruct((batch, HEADS, total, HEAD_DIM), BF16)
    return pl.pallas_call(
        _proj_prompt_kernel,
        grid=(batch, total // rows),
        in_specs=[x_spec, _resident(meta.shape), _resident((1, D_MODEL)), _resident(w_bf16.shape)],
        out_specs=[uq, kvb, kv, kv, kvb, kvb],
        out_shape=[u_shape, by_head, k_shape, v_shape, by_head, by_head],
        scratch_shapes=[pltpu.VMEM((rows, D_MODEL), BF16), pltpu.VMEM((N_META, D_MODEL), BF16)],
        compiler_params=pltpu.CompilerParams(
            dimension_semantics=("parallel", "arbitrary"), vmem_limit_bytes=VMEM_LIMIT),
        name="in_proj_prompt",
    )(x, meta, g, w_bf16)


def _stack_maps(q):
    lane = lax.broadcasted_iota(jnp.int32, q.shape, 1)
    zero = jnp.zeros_like(q)
    return jnp.concatenate(
        [jnp.where(lane < HALF_DIM, q, zero), jnp.where(lane >= HALF_DIM, q, zero)], axis=0)


def _finish_head(o, gs):
    return (_rms(o, gs, SUBLN_EPS) * (1.0 - LAMBDA_INIT)).astype(BF16)


def _attn_prompt_kernel(lam_ref, q_ref, kb_ref, vb_ref, diag_ref, meta_ref, gs_ref,
                        o_ref, q2_ref, s_ref, sm_ref, w_ref, wm_ref, rl_ref):
    tq = Q_TILE
    nt = o_ref.shape[0] // tq
    rb = SOFTMAX_ROWS
    lam = lam_ref[0, 0]

    def score_steps(c):
        slot = c % 2

        def stack():
            q2_ref[slot] = _stack_maps(q_ref[N_META + c * tq:N_META + (c + 1) * tq, :])

        def frames(j):
            d = _dot_nt(q2_ref[slot], kb_ref[N_META + j * tq:N_META + (j + 1) * tq, :])
            cols = slice(j * tq, (j + 1) * tq)
            if j >= c - 1:
                bias = diag_ref[0, c - j]
                s_ref[slot, :tq, cols] = d[:tq] + bias
                s_ref[slot, tq:, cols] = d[tq:] + bias
            else:
                s_ref[slot, :, cols] = d

        def metas():
            sm = _dot_nt(q2_ref[slot], kb_ref[0:N_META, :])
            if c == 0:
                sm = sm + jnp.concatenate([meta_ref[0], meta_ref[0]], axis=0)
            sm_ref[slot] = sm

        return [stack, metas] + [functools.partial(frames, j) for j in range(c + 1)]

    def softmax_step(c, i):
        slot = c % 2
        n = (c + 1) * tq
        last_chunk = ((i + 1) * rb - 1) // CHUNK
        seen = c * tq + -(-(last_chunk + 1) * CHUNK // HEAD_DIM) * HEAD_DIM
        rows = (slice(i * rb, (i + 1) * rb), slice(tq + i * rb, tq + (i + 1) * rb))
        p, pm, l = [], [], []
        for rmap in rows:
            sm = sm_ref[slot, rmap, :]
            m = jnp.maximum(jnp.max(s_ref[slot, rmap, :seen], axis=-1, keepdims=True),
                            jnp.max(sm, axis=-1, keepdims=True))
            p.append(jnp.exp2(s_ref[slot, rmap, :seen] - m))
            pm.append(jnp.exp2(sm - m))
            l.append(jnp.sum(p[-1], axis=-1, keepdims=True) + jnp.sum(pm[-1], axis=-1, keepdims=True))
        rho = lam * l[0] / l[1]
        w_ref[slot, rows[0], :seen] = (p[0] - p[1] * rho).astype(BF16)
        if seen < n:
            w_ref[slot, rows[0], seen:n] = jnp.zeros((rb, n - seen), BF16)
        wm_ref[slot, rows[0], :] = (pm[0] - pm[1] * rho).astype(BF16)
        rl_ref[slot, rows[0], :] = 1.0 / l[0]

    def value_step(c):
        slot = c % 2
        n = (c + 1) * tq
        o = _dot(w_ref[slot, :, :n], vb_ref[N_META:N_META + n, :]) + _dot(wm_ref[slot], vb_ref[0:N_META, :])
        o_ref[c * tq:(c + 1) * tq, :] = _finish_head(o * rl_ref[slot], gs_ref[...])

    for step in score_steps(0):
        step()
    for c in range(nt):
        ahead = score_steps(c + 1) if c + 1 < nt else []
        blocks = tq // rb
        for i in range(blocks):
            softmax_step(c, i)
            for step in ahead[i * len(ahead) // blocks:(i + 1) * len(ahead) // blocks]:
                step()
        value_step(c)


def _attn_prompt(q, kb, vb, diag, meta_b, lam, gs, batch, seq):
    tq = Q_TILE
    kv = pl.BlockSpec((None, None, N_META + seq, HEAD_DIM), lambda b, h: (b, h, 0, 0))
    qo = pl.BlockSpec((None, None, seq, HEAD_DIM), lambda b, h: (b, h, 0, 0))
    return pl.pallas_call(
        _attn_prompt_kernel,
        grid=(batch, HEADS),
        in_specs=[
            _smem(), kv, kv, kv,
            pl.BlockSpec((1, 2, tq, tq), lambda b, h: (h, 0, 0, 0)),
            pl.BlockSpec((1, tq, N_META), lambda b, h: (h, 0, 0)),
            pl.BlockSpec((1, HEAD_DIM), lambda b, h: (0, 0)),
        ],
        out_specs=qo,
        out_shape=jax.ShapeDtypeStruct((batch, HEADS, seq, HEAD_DIM), BF16),
        scratch_shapes=[
            pltpu.VMEM((2, 2 * tq, HEAD_DIM), BF16),
            pltpu.VMEM((2, 2 * tq, seq), F32),
            pltpu.VMEM((2, 2 * tq, N_META), F32),
            pltpu.VMEM((2, tq, seq), BF16),
            pltpu.VMEM((2, tq, N_META), BF16),
            pltpu.VMEM((2, tq, 1), F32),
        ],
        compiler_params=pltpu.CompilerParams(
            dimension_semantics=("parallel", "parallel"), vmem_limit_bytes=VMEM_LIMIT),
        name="attn_prompt",
    )(lam, q, kb, vb, diag, meta_b, gs)


def _attn_sample_kernel(lam_ref, q_ref, kn_ref, vn_ref, ck0_ref, ck1_ref, cv0_ref, cv1_ref,
                        tail_ref, gs_ref, o_ref, qd_ref, s_ref, w_ref):
    t = q_ref.shape[0]
    half = ck0_ref.shape[1] // HEADS
    n_cache = 2 * half
    split = n_cache - SAMPLE_TAIL
    width = n_cache + t
    lam = lam_ref[0, 0]
    hd = HEAD_DIM
    ck_ref, cv_ref = (ck0_ref, ck1_ref), (cv0_ref, cv1_ref)

    def gather(c_ref, n_ref, h):
        far0 = c_ref[0][0, pl.ds(h, half, stride=HEADS), :]
        far1 = c_ref[1][0, pl.ds(h, split - half, stride=HEADS), :]
        near = c_ref[1][0, pl.ds((split - half) * HEADS + h, SAMPLE_TAIL, stride=HEADS), :]
        new = n_ref[pl.ds(h, t, stride=HEADS), :]
        return (jnp.concatenate([far0, far1], axis=0).astype(BF16),
                jnp.concatenate([near.astype(BF16), new.astype(BF16)], axis=0))

    def pair_rows(c_ref, n_ref, hp):
        fa, ta = gather(c_ref, n_ref, 2 * hp)
        fb, tb = gather(c_ref, n_ref, 2 * hp + 1)
        return jnp.concatenate([fa, fb], axis=1), jnp.concatenate([ta, tb], axis=1)

    def score_steps(hp):
        slot = hp % 2

        def queries():
            zero = jnp.zeros((2 * t, hd), BF16)
            qa = _stack_maps(q_ref[:, 2 * hp * hd:(2 * hp + 1) * hd])
            qb = _stack_maps(q_ref[:, (2 * hp + 1) * hd:(2 * hp + 2) * hd])
            qd_ref[slot] = jnp.concatenate(
                [jnp.concatenate([qa, zero], axis=1), jnp.concatenate([zero, qb], axis=1)], axis=0)

        def scores():
            k_far, k_tail = pair_rows(ck_ref, kn_ref, hp)
            s_ref[slot, :, 0:split] = _dot_nt(qd_ref[slot], k_far)
            ba, bb = tail_ref[2 * hp], tail_ref[2 * hp + 1]
            s_ref[slot, :, split:width] = (
                _dot_nt(qd_ref[slot], k_tail) + jnp.concatenate([ba, ba, bb, bb], axis=0))

        return [queries, scores]

    def softmax_step(hp, j):
        slot = hp % 2
        r, p = [], []
        for rows in (slice(2 * t * j, 2 * t * j + t), slice(2 * t * j + t, 2 * t * (j + 1))):
            m = jnp.max(s_ref[slot, rows, 0:width], axis=-1, keepdims=True)
            p.append(jnp.exp2(s_ref[slot, rows, 0:width] - m))
            r.append(1.0 / jnp.sum(p[-1], axis=-1, keepdims=True))
        w_ref[slot, t * j:t * (j + 1), 0:width] = (p[0] * r[0] - p[1] * (lam * r[1])).astype(BF16)

    def value_step(hp):
        slot = hp % 2
        v_far, v_tail = pair_rows(cv_ref, vn_ref, hp)
        o = _dot(w_ref[slot, :, 0:split], v_far) + _dot(w_ref[slot, :, split:width], v_tail)
        gs = gs_ref[...]
        o_ref[:, 2 * hp * hd:(2 * hp + 1) * hd] = _finish_head(o[0:t, 0:hd], gs)
        o_ref[:, (2 * hp + 1) * hd:(2 * hp + 2) * hd] = _finish_head(o[t:2 * t, hd:2 * hd], gs)

    pairs = HEADS // 2
    for step in score_steps(0):
        step()
    for hp in range(pairs):
        ahead = score_steps(hp + 1) if hp + 1 < pairs else [lambda: None, lambda: None]
        softmax_step(hp, 0)
        ahead[0]()
        ahead[1]()
        softmax_step(hp, 1)
        value_step(hp)


def _attn_sample(q, k_new, v_new, cache_k, cache_v, tail_b, lam, gs, batch, t):
    n_cache = cache_k.shape[1] // HEADS
    half_rows = cache_k.shape[1] // 2
    assert n_cache % 2 == 0 and half_rows % (8 * HEADS) == 0 and n_cache // 2 <= n_cache - SAMPLE_TAIL
    row = pl.BlockSpec((t, ATTN_WIDTH), lambda b: (b, 0))
    new = pl.BlockSpec((t * HEADS, HEAD_DIM), lambda b: (b, 0))
    halves = [pl.BlockSpec((1, half_rows, HEAD_DIM), functools.partial(lambda j, b: (b, j, 0), j))
              for j in (0, 1)]
    width = -(-(n_cache + t) // HEAD_DIM) * HEAD_DIM
    return pl.pallas_call(
        _attn_sample_kernel,
        grid=(batch,),
        in_specs=[_smem(), row, new, new, *halves, *halves,
                  _resident(tail_b.shape), _resident((1, HEAD_DIM))],
        out_specs=row,
        out_shape=jax.ShapeDtypeStruct((batch * t, ATTN_WIDTH), BF16),
        scratch_shapes=[
            pltpu.VMEM((2, 4 * t, 2 * HEAD_DIM), BF16),
            pltpu.VMEM((2, 4 * t, width), F32),
            pltpu.VMEM((2, 2 * t, width), BF16),
        ],
        compiler_params=pltpu.CompilerParams(
            dimension_semantics=("parallel",), vmem_limit_bytes=VMEM_LIMIT),
        name="attn_sample",
    )(lam, q, k_new, v_new, cache_k, cache_k, cache_v, cache_v, tail_b, gs)


def _pool_mix(ext, wp_ref, scale):
    outs = []
    for g, w in enumerate(POOL_WINDOWS):
        xg = ext[:, g * POOL_GROUP_DIM:(g + 1) * POOL_GROUP_DIM]
        s = xg
        span = 1
        while span < w:
            s = s + pltpu.roll(s, span, axis=0)
            span *= 2
        p = (s[POOL_HISTORY:] * (1.0 / w) - xg[POOL_HISTORY:]).astype(BF16)
        outs.append(_dot(p, wp_ref[g]))
    return jnp.concatenate(outs, axis=-1) * scale


def _mix_out(a, b_ref, x_ref, wo, o_ref):
    mix = jnp.concatenate([a.astype(BF16), b_ref[...]], axis=-1)
    o_ref[...] = x_ref[...] + _dot(mix, wo)


def _out_prompt_kernel(u_ref, b_ref, x_ref, wp_ref, ps_ref, wo_ref, o_ref):
    half = x_ref.shape[0] // 2
    for r in (0, half):
        a = _pool_mix(u_ref[r:r + POOL_HISTORY + half, :], wp_ref, ps_ref[...])
        b = [b_ref[h, r:r + half, :] for h in range(HEADS)]
        mix = jnp.concatenate([a.astype(BF16), *b], axis=-1)
        o_ref[r:r + half, :] = x_ref[r:r + half, :] + _dot(mix, wo_ref[...])


def _out_sample_kernel(u_ref, state_ref, b_ref, x_ref, wp_ref, ps_ref, wo_ref, o_ref, wob_ref):
    n_seq, t = state_ref.shape[0], u_ref.shape[0] // state_ref.shape[0]
    parts = []
    for s in range(n_seq):
        ext = jnp.concatenate([state_ref[s], u_ref[s * t:(s + 1) * t, :]], axis=0)
        parts.append(_pool_mix(ext, wp_ref, ps_ref[...]))
    wo = wo_ref[...].astype(BF16)
    wob_ref[...] = wo
    _mix_out(jnp.concatenate(parts, axis=0), b_ref, x_ref, wo, o_ref)


def _out_common_specs(rows, idx):
    return (pl.BlockSpec((rows, ATTN_WIDTH), idx), pl.BlockSpec((rows, D_MODEL), idx))


def _out_prompt(u, b, x, wp, ps, wo, batch, seq):
    rows = PROJ_ROWS
    nt = seq // rows
    idx = lambda bi, i: (bi * nt + i, 0)
    u_spec = pl.BlockSpec(
        (pl.Squeezed(), pl.Element(POOL_HISTORY + rows), pl.Element(POOL_WIDTH)),
        lambda bi, i: (bi, N_META - POOL_HISTORY + i * rows, 0))
    _, x_spec = _out_common_specs(rows, idx)
    b_spec = pl.BlockSpec((None, HEADS, rows, HEAD_DIM), lambda bi, i: (bi, 0, i, 0))
    return pl.pallas_call(
        _out_prompt_kernel,
        grid=(batch, nt),
        in_specs=[u_spec, b_spec, x_spec,
                  _resident(wp.shape), _resident(ps.shape), _resident(wo.shape)],
        out_specs=x_spec,
        out_shape=jax.ShapeDtypeStruct(x.shape, F32),
        compiler_params=pltpu.CompilerParams(
            dimension_semantics=("parallel", "parallel"), vmem_limit_bytes=VMEM_LIMIT),
        name="out_proj_prompt",
    )(u, b, x, wp, ps, wo)


def _out_sample(u, history, b, x, wp, ps, wo, t):
    rows = PROJ_ROWS
    n_seq = rows // t
    idx = lambda i: (i, 0)
    b_spec, x_spec = _out_common_specs(rows, idx)
    return pl.pallas_call(
        _out_sample_kernel,
        grid=(x.shape[0] // rows,),
        in_specs=[
            pl.BlockSpec((rows, POOL_WIDTH), idx),
            pl.BlockSpec((n_seq, POOL_HISTORY, POOL_WIDTH), lambda i: (i, 0, 0)),
            b_spec, x_spec,
            _resident(wp.shape), _resident(ps.shape), _resident(wo.shape),
        ],
        out_specs=[x_spec, pl.BlockSpec(wo.shape, lambda i: (0, 0))],
        out_shape=[jax.ShapeDtypeStruct(x.shape, F32), jax.ShapeDtypeStruct(wo.shape, BF16)],
        compiler_params=pltpu.CompilerParams(
            dimension_semantics=("arbitrary",), vmem_limit_bytes=VMEM_LIMIT),
        name="out_proj_sample",
    )(u, history, b, x, wp, ps, wo)


def _ffn_kernel(h_ref, gf_ref, wg_ref, wu_ref, wd_ref, gfin_ref, y_ref, *rest):
    n_ref = rest[-1]
    f = pl.program_id(1)

    @pl.when(f == 0)
    def _():
        h = h_ref[...]
        n_ref[...] = _rms(h, gf_ref[...], EPS).astype(BF16)
        y_ref[...] = h

    weights = [wg_ref[...], wu_ref[...], wd_ref[...]]
    if len(rest) > 1:
        weights = [w.astype(BF16) for w in weights]
        for out_ref, w in zip(rest[:3], weights):
            out_ref[...] = w
    wg, wu, wd = weights
    n = n_ref[...]
    g = _dot(n, wg)
    u = _dot(n, wu)
    act = (g * (1.0 / (1.0 + jnp.exp(-g))) * u).astype(BF16)
    y_ref[...] += _dot(act, wd)

    @pl.when(f == pl.num_programs(1) - 1)
    def _():
        y_ref[...] = _rms(y_ref[...], gfin_ref[...], EPS)


def _ffn(h, gf, wg, wu, wd, gfin, cols):
    rows = FFN_ROWS
    emit = wg.dtype == F32
    row = pl.BlockSpec((rows, D_MODEL), lambda i, f: (i, 0))
    vec = pl.BlockSpec((1, D_MODEL), lambda i, f: (0, 0))
    w_in = pl.BlockSpec((D_MODEL, cols), lambda i, f: (0, f))
    w_out = pl.BlockSpec((cols, D_MODEL), lambda i, f: (f, 0))
    out_specs, out_shape = [row], [jax.ShapeDtypeStruct(h.shape, F32)]
    if emit:
        out_specs += [w_in, w_in, w_out]
        out_shape += [jax.ShapeDtypeStruct(w.shape, BF16) for w in (wg, wu, wd)]
    outs = pl.pallas_call(
        _ffn_kernel,
        grid=(h.shape[0] // rows, D_FF // cols),
        in_specs=[row, vec, w_in, w_in, w_out, vec],
        out_specs=out_specs,
        out_shape=out_shape,
        scratch_shapes=[pltpu.VMEM((rows, D_MODEL), BF16)],
        compiler_params=pltpu.CompilerParams(
            dimension_semantics=("arbitrary" if emit else "parallel", "arbitrary"),
            vmem_limit_bytes=VMEM_LIMIT),
        name="swiglu_f32w" if emit else "swiglu",
    )(h, gf, wg, wu, wd, gfin)
    return outs if emit else outs[0]


def kernel(x_prompt, x_sample, cache_k, cache_v, state_pool, meta, g_mix, w_in, w_pool,
           pool_scale, lambda_q1, lambda_k1, lambda_q2, lambda_k2, g_subln, w_out, g_ffn,
           w_gate, w_up, w_down, rel_bias, g_final):
    batch, seq, d = x_prompt.shape
    dec_batch, t, _ = x_sample.shape
    depth = w_in.shape[0]
    assert depth == 1 and d == D_MODEL and meta.shape == (N_META, D_MODEL)
    assert seq % PROJ_ROWS == 0 and seq % Q_TILE == 0 and PROJ_ROWS % t == 0
    assert (dec_batch * t) % PROJ_ROWS == 0 and t >= POOL_STATE
    assert POOL_STATE == max(POOL_WINDOWS) - 1 <= POOL_HISTORY <= N_META
    assert (N_META + seq) % (PROMPT_TILES * BF16_ROWS) == 0
    prompt_rows = (N_META + seq) // PROMPT_TILES
    n_cache = cache_k.shape[2]
    assert min(Q_TILE, SAMPLE_TAIL) >= _BUCKET_THRESHOLDS[-1]
    assert (n_cache - N_META) % CHUNK == 0 and t <= CHUNK
    assert n_cache > SAMPLE_TAIL and (n_cache - SAMPLE_TAIL) % HEAD_DIM == 0

    w_pool_b = w_pool[0].astype(BF16)
    g_mix2 = g_mix[0].reshape(1, D_MODEL)
    g_ffn2 = g_ffn[0].reshape(1, D_MODEL)
    g_fin2 = g_final.reshape(1, D_MODEL)
    g_sub2 = g_subln[0].reshape(1, HEAD_DIM)
    ps2 = pool_scale[0].reshape(1, POOL_WIDTH)

    diag_b, meta_b, tail_b, lam = _bias_tables(
        rel_bias, lambda_q1, lambda_k1, lambda_q2, lambda_k2, t)

    xp = x_prompt.reshape(batch * seq, D_MODEL)
    xs = x_sample.reshape(dec_batch * t, D_MODEL)
    u_s, q_s, k_s, v_s, w_in_b = _project_rows(xs, g_mix2, w_in[0], PROJ_ROWS)
    u_p, q_p, k_all, v_all, kb_p, vb_p = _project_prompt(x_prompt, meta, g_mix2, w_in_b, prompt_rows)

    b_p = _attn_prompt(q_p, kb_p, vb_p, diag_b, meta_b, lam, g_sub2, batch, seq)
    b_s = _attn_sample(q_s, k_s, v_s,
                       cache_k[0].reshape(dec_batch, n_cache * HEADS, HEAD_DIM),
                       cache_v[0].reshape(dec_batch, n_cache * HEADS, HEAD_DIM),
                       tail_b, lam, g_sub2, dec_batch, t)

    history = jnp.pad(state_pool[0], ((0, 0), (POOL_HISTORY - POOL_STATE, 0), (0, 0)))
    h_s, w_out_b = _out_sample(u_s, history, b_s, xs, w_pool_b, ps2, w_out[0], t)
    h_p = _out_prompt(u_p, b_p, xp, w_pool_b, ps2, w_out_b, batch, seq)

    y_s, wg_b, wu_b, wd_b = _ffn(h_s, g_ffn2, w_gate[0], w_up[0], w_down[0], g_fin2, FFN_COLS_F32)
    y_p = _ffn(h_p, g_ffn2, wg_b, wu_b, wd_b, g_fin2, FFN_COLS)

    hd = (HEADS, HEAD_DIM)
    return (
        y_p.reshape(batch, seq, D_MODEL),
        y_s.reshape(dec_batch, t, D_MODEL),
        k_all.reshape(1, batch, N_META + seq, *hd),
        v_all.reshape(1, batch, N_META + seq, *hd),
        u_p[None, :, N_META + seq - POOL_STATE:],
        k_s.reshape(1, dec_batch, t, *hd),
        v_s.reshape(1, dec_batch, t, *hd),
        u_s.reshape(dec_batch, t, POOL_WIDTH)[None, :, t - POOL_STATE:],
    )
```

```python
import functools
import math

import jax
import jax.numpy as jnp
from jax import lax
from jax.experimental import pallas as pl
from jax.experimental.pallas import tpu as pltpu

F32 = jnp.float32
BF16 = jnp.bfloat16

D_MODEL = 2048
CHUNK = 64
N_META = 16
POOL_WINDOWS = (2, 4, 8, 16)
POOL_GROUP_DIM = 256
POOL_WIDTH = 1024
POOL_STATE = 15
POOL_HISTORY = 16
HEADS = 8
HEAD_DIM = 128
HALF_DIM = 64
ATTN_WIDTH = 1024
ATT_SCALE = HALF_DIM ** -0.5
LOG2E = math.log2(math.e)
D_FF = 5632
EPS = 1e-6
SUBLN_EPS = 1e-5
NEG_INF = -1e30
LAMBDA_INIT = 0.8 - 0.6 * math.exp(-0.3 * 0)

_BUCKET_THRESHOLDS = (1, 2, 3, 4, 5, 6, 7, 8, 12, 16, 23, 32, 46, 64, 91)

Q_TILE = 256
SOFTMAX_ROWS = 32
PROJ_ROWS = 512
OUT_PROMPT_ROWS = 1024
PROMPT_TILES = 3
FFN_ROWS = 1024
FFN_COLS = 512
FFN_COLS_F32 = 256
SAMPLE_TAIL = 272
VMEM_LIMIT = 56 * 1024 * 1024
BF16_ROWS = 16


def _rms(x, g, eps):
    ms = jnp.mean(x * x, axis=-1, keepdims=True)
    return x * lax.rsqrt(ms + eps) * g


def _dot(a, b):
    return jnp.dot(a, b, preferred_element_type=F32)


def _dot_nt(a, b):
    return lax.dot_general(a, b, (((1,), (1,)), ((), ())), preferred_element_type=F32)


def _resident(shape):
    nd = len(shape)
    return pl.BlockSpec(shape, lambda *_: (0,) * nd, pipeline_mode=pl.Buffered(1))


def _smem():
    return pl.BlockSpec(memory_space=pltpu.SMEM)


def _bias_of_rel(rel, table_ref, h):
    n = jnp.abs(rel)
    far = table_ref[len(_BUCKET_THRESHOLDS), h]

    def entry(b):
        return (table_ref[b, h] - far) * LOG2E

    neg = jnp.full(rel.shape, entry(0), F32)
    pos = jnp.full(rel.shape, entry(16), F32)
    for b, thr in enumerate(_BUCKET_THRESHOLDS, start=1):
        ge = n >= thr
        neg = jnp.where(ge, entry(b), neg)
        pos = jnp.where(ge, entry(16 + b), pos)
    return jnp.where(rel > 0, pos, neg)


def _bias_kernel(table_ref, lq1_ref, lk1_ref, lq2_ref, lk2_ref,
                 diag_ref, meta_ref, tail_ref, lam_ref):
    h = pl.program_id(0)
    t = Q_TILE
    qi = lax.broadcasted_iota(jnp.int32, (t, t), 0)
    kj = lax.broadcasted_iota(jnp.int32, (t, t), 1)
    visible = (kj // CHUNK) <= (qi // CHUNK)
    diag_ref[0, 0] = jnp.where(visible, _bias_of_rel(kj - qi, table_ref, h), NEG_INF)
    diag_ref[0, 1] = _bias_of_rel(kj - t - qi, table_ref, h)
    qm = lax.broadcasted_iota(jnp.int32, (t, N_META), 0)
    km = lax.broadcasted_iota(jnp.int32, (t, N_META), 1)
    meta_ref[0] = _bias_of_rel(km - N_META - qm, table_ref, h)
    qs = lax.broadcasted_iota(jnp.int32, tail_ref.shape[1:], 0)
    ks = lax.broadcasted_iota(jnp.int32, tail_ref.shape[1:], 1)
    tail_ref[0] = _bias_of_rel(ks - SAMPLE_TAIL - qs, table_ref, h)
    s1 = jnp.sum(lq1_ref[...] * lk1_ref[...], axis=-1, keepdims=True)
    s2 = jnp.sum(lq2_ref[...] * lk2_ref[...], axis=-1, keepdims=True)
    lam_ref[...] = jnp.exp(s1) - jnp.exp(s2) + LAMBDA_INIT


def _bias_tables(rel_bias, lq1, lk1, lq2, lk2, n_new):
    tail_w = SAMPLE_TAIL + n_new
    vec = pl.BlockSpec((1, HALF_DIM), lambda h: (0, 0))
    return pl.pallas_call(
        _bias_kernel,
        grid=(HEADS,),
        in_specs=[_smem(), vec, vec, vec, vec],
        out_specs=[
            pl.BlockSpec((1, 2, Q_TILE, Q_TILE), lambda h: (h, 0, 0, 0)),
            pl.BlockSpec((1, Q_TILE, N_META), lambda h: (h, 0, 0)),
            pl.BlockSpec((1, n_new, tail_w), lambda h: (h, 0, 0)),
            pl.BlockSpec((1, 1), lambda h: (0, 0)),
        ],
        out_shape=[
            jax.ShapeDtypeStruct((HEADS, 2, Q_TILE, Q_TILE), F32),
            jax.ShapeDtypeStruct((HEADS, Q_TILE, N_META), F32),
            jax.ShapeDtypeStruct((HEADS, n_new, tail_w), F32),
            jax.ShapeDtypeStruct((1, 1), F32),
        ],
        name="bias_tables",
    )(rel_bias, lq1, lk1, lq2, lk2)


def _store_heads(o_ref, x, row0=0):
    rows = x.shape[0]
    for h in range(HEADS):
        o_ref[pl.ds(row0 * HEADS + h, rows, stride=HEADS), :] = x[:, h * HEAD_DIM:(h + 1) * HEAD_DIM]


def _store_column(c, y, u_ref, q_ref, k_ref, v_ref, row0=0, by_head=None):
    rows = slice(row0, row0 + y.shape[0])
    if c == 0:
        u_ref[rows, :] = y
    elif c == 1:
        q = (y * (ATT_SCALE * LOG2E)).astype(BF16)
        if by_head is None:
            q_ref[rows, :] = q
        else:
            for h in range(HEADS):
                q_ref[h, rows, :] = q[:, h * HEAD_DIM:(h + 1) * HEAD_DIM]
    else:
        _store_heads(k_ref if c == 2 else v_ref, y, row0)
        if by_head is not None:
            for h in range(HEADS):
                by_head[c - 2][h, rows, :] = y[:, h * HEAD_DIM:(h + 1) * HEAD_DIM].astype(BF16)


def _project_store(xn, w_ref, u_ref, q_ref, k_ref, v_ref, row0=0, by_head=None):
    w = POOL_WIDTH
    for c in range(4):
        _store_column(c, _dot(xn, w_ref[:, c * w:(c + 1) * w]), u_ref, q_ref, k_ref, v_ref, row0, by_head)


def _proj_rows_kernel(x_ref, g_ref, w_ref, u_ref, q_ref, k_ref, v_ref, wb_ref, xn_ref):
    j = pl.program_id(0)
    rows = x_ref.shape[0]
    tile = pl.ds(pl.multiple_of(pl.program_id(1) * rows, rows), rows)

    @pl.when(j == 0)
    def _():
        xn_ref[tile, :] = _rms(x_ref[...], g_ref[...], EPS).astype(BF16)

    wb = w_ref[...].astype(BF16)
    wb_ref[...] = wb
    y = _dot(xn_ref[tile, :], wb)
    for c in range(4):
        @pl.when(j == c)
        def _(c=c):
            _store_column(c, y, u_ref, q_ref, k_ref, v_ref)


def _proj_prompt_kernel(x_ref, meta_ref, g_ref, w_ref, u_ref, q_ref, k_ref, v_ref, kb_ref, vb_ref,
                        xn_ref, carry_ref):
    i = pl.program_id(1)
    g = g_ref[...]
    rows = xn_ref.shape[0]
    split = -(-(rows // 2) // BF16_ROWS) * BF16_ROWS

    @pl.when(i == 0)
    def _():
        xn_ref[0:N_META, :] = _rms(meta_ref[...], g, EPS).astype(BF16)

    @pl.when(i > 0)
    def _():
        xn_ref[0:N_META, :] = carry_ref[...]

    xn_ref[N_META:split, :] = _rms(x_ref[0:split - N_META, :], g, EPS).astype(BF16)
    by_head = (kb_ref, vb_ref)
    _project_store(xn_ref[0:split, :], w_ref, u_ref, q_ref, k_ref, v_ref, 0, by_head)
    xn_ref[split:rows, :] = _rms(x_ref[split - N_META:rows - N_META, :], g, EPS).astype(BF16)
    _project_store(xn_ref[split:rows, :], w_ref, u_ref, q_ref, k_ref, v_ref, split, by_head)

    @pl.when(i < pl.num_programs(1) - 1)
    def _():
        carry_ref[...] = _rms(x_ref[rows - N_META:rows, :], g, EPS).astype(BF16)


def _proj_out_shapes(lead, n):
    return [jax.ShapeDtypeStruct((*lead, n, POOL_WIDTH), F32),
            jax.ShapeDtypeStruct((*lead, n, POOL_WIDTH), BF16),
            jax.ShapeDtypeStruct((*lead, n * HEADS, HEAD_DIM), F32),
            jax.ShapeDtypeStruct((*lead, n * HEADS, HEAD_DIM), F32)]


def _project_rows(x, g, w_f32, rows):
    n = x.shape[0]
    w = POOL_WIDTH
    last = n // rows - 1

    def rows_of(c):
        return lambda j, i: (jnp.where(j < c, 0, jnp.where(j == c, i, last)), 0)

    uq = [pl.BlockSpec((rows, w), rows_of(c)) for c in (0, 1)]
    kv = [pl.BlockSpec((rows * HEADS, HEAD_DIM), rows_of(c)) for c in (2, 3)]
    w_spec = pl.BlockSpec((D_MODEL, w), lambda j, i: (0, j))
    return pl.pallas_call(
        _proj_rows_kernel,
        grid=(w_f32.shape[1] // w, n // rows),
        in_specs=[pl.BlockSpec((rows, D_MODEL), rows_of(0)), _resident((1, D_MODEL)), w_spec],
        out_specs=[*uq, *kv, w_spec],
        out_shape=_proj_out_shapes((), n) + [jax.ShapeDtypeStruct(w_f32.shape, BF16)],
        scratch_shapes=[pltpu.VMEM((n, D_MODEL), BF16)],
        compiler_params=pltpu.CompilerParams(
            dimension_semantics=("arbitrary", "arbitrary"), vmem_limit_bytes=VMEM_LIMIT),
        name="in_proj_rows",
    )(x, g, w_f32)


def _project_prompt(x, meta, g, w_bf16, rows):
    batch, seq, _ = x.shape
    total = N_META + seq
    x_spec = pl.BlockSpec((None, rows, D_MODEL), lambda b, i: (b, i, 0))
    uq = pl.BlockSpec((None, rows, POOL_WIDTH), lambda b, i: (b, i, 0))
    kv = pl.BlockSpec((None, rows * HEADS, HEAD_DIM), lambda b, i: (b, i, 0))
    kvb = pl.BlockSpec((None, HEADS, rows, HEAD_DIM), lambda b, i: (b, 0, i, 0))
    u_shape, _, k_shape, v_shape = _proj_out_shapes((batch,), total)
    by_head = jax.ShapeDtypeStruct((batch, HEADS, total, HEAD_DIM), BF16)
    return pl.pallas_call(
        _proj_prompt_kernel,
        grid=(batch, total // rows),
        in_specs=[x_spec, _resident(meta.shape), _resident((1, D_MODEL)), _resident(w_bf16.shape)],
        out_specs=[uq, kvb, kv, kv, kvb, kvb],
        out_shape=[u_shape, by_head, k_shape, v_shape, by_head, by_head],
        scratch_shapes=[pltpu.VMEM((rows, D_MODEL), BF16), pltpu.VMEM((N_META, D_MODEL), BF16)],
        compiler_params=pltpu.CompilerParams(
            dimension_semantics=("parallel", "arbitrary"), vmem_limit_bytes=VMEM_LIMIT),
        name="in_proj_prompt",
    )(x, meta, g, w_bf16)


def _stack_maps(q):
    lane = lax.broadcasted_iota(jnp.int32, q.shape, 1)
    zero = jnp.zeros_like(q)
    return jnp.concatenate(
        [jnp.where(lane < HALF_DIM, q, zero), jnp.where(lane >= HALF_DIM, q, zero)], axis=0)


def _finish_head(o, gs):
    return (_rms(o, gs, SUBLN_EPS) * (1.0 - LAMBDA_INIT)).astype(BF16)


def _attn_prompt_kernel(lam_ref, q_ref, kb_ref, vb_ref, diag_ref, meta_ref, gs_ref,
                        o_ref, q2_ref, s_ref, sm_ref, w_ref, wm_ref, rl_ref):
    tq = Q_TILE
    nt = o_ref.shape[0] // tq
    rb = SOFTMAX_ROWS
    lam = lam_ref[0, 0]

    def score_steps(c):
        slot = c % 2

        def stack():
            q2_ref[slot] = _stack_maps(q_ref[N_META + c * tq:N_META + (c + 1) * tq, :])

        def frames(j):
            d = _dot_nt(q2_ref[slot], kb_ref[N_META + j * tq:N_META + (j + 1) * tq, :])
            cols = slice(j * tq, (j + 1) * tq)
            if j >= c - 1:
                bias = diag_ref[0, c - j]
                s_ref[slot, :tq, cols] = d[:tq] + bias
                s_ref[slot, tq:, cols] = d[tq:] + bias
            else:
                s_ref[slot, :, cols] = d

        def metas():
            sm = _dot_nt(q2_ref[slot], kb_ref[0:N_META, :])
            if c == 0:
                sm = sm + jnp.concatenate([meta_ref[0], meta_ref[0]], axis=0)
            sm_ref[slot] = sm

        return [stack, metas] + [functools.partial(frames, j) for j in range(c + 1)]

    def softmax_step(c, i):
        slot = c % 2
        n = (c + 1) * tq
        last_chunk = ((i + 1) * rb - 1) // CHUNK
        seen = c * tq + -(-(last_chunk + 1) * CHUNK // HEAD_DIM) * HEAD_DIM
        rows = (slice(i * rb, (i + 1) * rb), slice(tq + i * rb, tq + (i + 1) * rb))
        p, pm, l = [], [], []
        for rmap in rows:
            sm = sm_ref[slot, rmap, :]
            m = jnp.maximum(jnp.max(s_ref[slot, rmap, :seen], axis=-1, keepdims=True),
                            jnp.max(sm, axis=-1, keepdims=True))
            p.append(jnp.exp2(s_ref[slot, rmap, :seen] - m))
            pm.append(jnp.exp2(sm - m))
            l.append(jnp.sum(p[-1], axis=-1, keepdims=True) + jnp.sum(pm[-1], axis=-1, keepdims=True))
        rho = lam * l[0] / l[1]
        w_ref[slot, rows[0], :seen] = (p[0] - p[1] * rho).astype(BF16)
        if seen < n:
            w_ref[slot, rows[0], seen:n] = jnp.zeros((rb, n - seen), BF16)
        wm_ref[slot, rows[0], :] = (pm[0] - pm[1] * rho).astype(BF16)
        rl_ref[slot, rows[0], :] = 1.0 / l[0]

    def value_step(c):
        slot = c % 2
        n = (c + 1) * tq
        o = _dot(w_ref[slot, :, :n], vb_ref[N_META:N_META + n, :]) + _dot(wm_ref[slot], vb_ref[0:N_META, :])
        o_ref[c * tq:(c + 1) * tq, :] = _finish_head(o * rl_ref[slot], gs_ref[...])

    for step in score_steps(0):
        step()
    for c in range(nt):
        ahead = score_steps(c + 1) if c + 1 < nt else []
        blocks = tq // rb
        for i in range(blocks):
            softmax_step(c, i)
            for step in ahead[i * len(ahead) // blocks:(i + 1) * len(ahead) // blocks]:
                step()
        value_step(c)


def _attn_prompt(q, kb, vb, diag, meta_b, lam, gs, batch, seq):
    tq = Q_TILE
    kv = pl.BlockSpec((None, None, N_META + seq, HEAD_DIM), lambda b, h: (b, h, 0, 0))
    qo = pl.BlockSpec((None, None, seq, HEAD_DIM), lambda b, h: (b, h, 0, 0))
    return pl.pallas_call(
        _attn_prompt_kernel,
        grid=(batch, HEADS),
        in_specs=[
            _smem(), kv, kv, kv,
            pl.BlockSpec((1, 2, tq, tq), lambda b, h: (h, 0, 0, 0)),
            pl.BlockSpec((1, tq, N_META), lambda b, h: (h, 0, 0)),
            pl.BlockSpec((1, HEAD_DIM), lambda b, h: (0, 0)),
        ],
        out_specs=qo,
        out_shape=jax.ShapeDtypeStruct((batch, HEADS, seq, HEAD_DIM), BF16),
        scratch_shapes=[
            pltpu.VMEM((2, 2 * tq, HEAD_DIM), BF16),
            pltpu.VMEM((2, 2 * tq, seq), F32),
            pltpu.VMEM((2, 2 * tq, N_META), F32),
            pltpu.VMEM((2, tq, seq), BF16),
            pltpu.VMEM((2, tq, N_META), BF16),
            pltpu.VMEM((2, tq, 1), F32),
        ],
        compiler_params=pltpu.CompilerParams(
            dimension_semantics=("parallel", "parallel"), vmem_limit_bytes=VMEM_LIMIT),
        name="attn_prompt",
    )(lam, q, kb, vb, diag, meta_b, gs)


def _attn_sample_kernel(lam_ref, q_ref, kn_ref, vn_ref, ck_ref, cv_ref, tail_ref, gs_ref,
                        o_ref, qd_ref, s_ref, w_ref):
    t = q_ref.shape[0]
    n_cache = ck_ref.shape[1] // HEADS
    split = n_cache - SAMPLE_TAIL
    width = n_cache + t
    lam = lam_ref[0, 0]
    hd = HEAD_DIM

    def gather(c_ref, n_ref, h):
        far = pl.ds(h, split, stride=HEADS)
        near = pl.ds(split * HEADS + h, SAMPLE_TAIL, stride=HEADS)
        new = pl.ds(h, t, stride=HEADS)
        return (c_ref[0, far, :].astype(BF16),
                jnp.concatenate([c_ref[0, near, :].astype(BF16), n_ref[new, :].astype(BF16)], axis=0))

    def pair_rows(c_ref, n_ref, hp):
        fa, ta = gather(c_ref, n_ref, 2 * hp)
        fb, tb = gather(c_ref, n_ref, 2 * hp + 1)
        return jnp.concatenate([fa, fb], axis=1), jnp.concatenate([ta, tb], axis=1)

    def score_steps(hp):
        slot = hp % 2

        def queries():
            zero = jnp.zeros((2 * t, hd), BF16)
            qa = _stack_maps(q_ref[:, 2 * hp * hd:(2 * hp + 1) * hd])
            qb = _stack_maps(q_ref[:, (2 * hp + 1) * hd:(2 * hp + 2) * hd])
            qd_ref[slot] = jnp.concatenate(
                [jnp.concatenate([qa, zero], axis=1), jnp.concatenate([zero, qb], axis=1)], axis=0)

        def scores():
            k_far, k_tail = pair_rows(ck_ref, kn_ref, hp)
            s_ref[slot, :, 0:split] = _dot_nt(qd_ref[slot], k_far)
            ba, bb = tail_ref[2 * hp], tail_ref[2 * hp + 1]
            s_ref[slot, :, split:width] = (
                _dot_nt(qd_ref[slot], k_tail) + jnp.concatenate([ba, ba, bb, bb], axis=0))

        return [queries, scores]

    def softmax_step(hp, j):
        slot = hp % 2
        r, p = [], []
        for rows in (slice(2 * t * j, 2 * t * j + t), slice(2 * t * j + t, 2 * t * (j + 1))):
            m = jnp.max(s_ref[slot, rows, 0:width], axis=-1, keepdims=True)
            p.append(jnp.exp2(s_ref[slot, rows, 0:width] - m))
            r.append(1.0 / jnp.sum(p[-1], axis=-1, keepdims=True))
        w_ref[slot, t * j:t * (j + 1), 0:width] = (p[0] * r[0] - p[1] * (lam * r[1])).astype(BF16)

    def value_step(hp):
        slot = hp % 2
        v_far, v_tail = pair_rows(cv_ref, vn_ref, hp)
        o = _dot(w_ref[slot, :, 0:split], v_far) + _dot(w_ref[slot, :, split:width], v_tail)
        gs = gs_ref[...]
        o_ref[:, 2 * hp * hd:(2 * hp + 1) * hd] = _finish_head(o[0:t, 0:hd], gs)
        o_ref[:, (2 * hp + 1) * hd:(2 * hp + 2) * hd] = _finish_head(o[t:2 * t, hd:2 * hd], gs)

    pairs = HEADS // 2
    for step in score_steps(0):
        step()
    for hp in range(pairs):
        ahead = score_steps(hp + 1) if hp + 1 < pairs else [lambda: None, lambda: None]
        softmax_step(hp, 0)
        ahead[0]()
        ahead[1]()
        softmax_step(hp, 1)
        value_step(hp)


def _attn_sample(q, k_new, v_new, cache_k, cache_v, tail_b, lam, gs, batch, t):
    n_cache = cache_k.shape[1] // HEADS
    row = pl.BlockSpec((t, ATTN_WIDTH), lambda b: (b, 0))
    new = pl.BlockSpec((t * HEADS, HEAD_DIM), lambda b: (b, 0))
    cache = pl.BlockSpec((1,) + cache_k.shape[1:], lambda b: (b, 0, 0))
    width = -(-(n_cache + t) // HEAD_DIM) * HEAD_DIM
    return pl.pallas_call(
        _attn_sample_kernel,
        grid=(batch,),
        in_specs=[_smem(), row, new, new, cache, cache,
                  _resident(tail_b.shape), _resident((1, HEAD_DIM))],
        out_specs=row,
        out_shape=jax.ShapeDtypeStruct((batch * t, ATTN_WIDTH), BF16),
        scratch_shapes=[
            pltpu.VMEM((2, 4 * t, 2 * HEAD_DIM), BF16),
            pltpu.VMEM((2, 4 * t, width), F32),
            pltpu.VMEM((2, 2 * t, width), BF16),
        ],
        compiler_params=pltpu.CompilerParams(
            dimension_semantics=("parallel",), vmem_limit_bytes=VMEM_LIMIT),
        name="attn_sample",
    )(lam, q, k_new, v_new, cache_k, cache_v, tail_b, gs)


def _pool_mix(ext, wp_ref, scale):
    outs = []
    for g, w in enumerate(POOL_WINDOWS):
        xg = ext[:, g * POOL_GROUP_DIM:(g + 1) * POOL_GROUP_DIM]
        s = xg
        span = 1
        while span < w:
            s = s + pltpu.roll(s, span, axis=0)
            span *= 2
        p = (s[POOL_HISTORY:] * (1.0 / w) - xg[POOL_HISTORY:]).astype(BF16)
        outs.append(_dot(p, wp_ref[g]))
    return jnp.concatenate(outs, axis=-1) * scale


def _mix_out(a, b_ref, x_ref, wo, o_ref):
    mix = jnp.concatenate([a.astype(BF16), b_ref[...]], axis=-1)
    o_ref[...] = x_ref[...] + _dot(mix, wo)


def _out_prompt_kernel(u_ref, b_ref, x_ref, wp_ref, ps_ref, wo_ref, o_ref):
    half = x_ref.shape[0] // 2
    for r in (0, half):
        a = _pool_mix(u_ref[r:r + POOL_HISTORY + half, :], wp_ref, ps_ref[...])
        b = [b_ref[h, r:r + half, :] for h in range(HEADS)]
        mix = jnp.concatenate([a.astype(BF16), *b], axis=-1)
        o_ref[r:r + half, :] = x_ref[r:r + half, :] + _dot(mix, wo_ref[...])


def _out_sample_kernel(u_ref, state_ref, b_ref, x_ref, wp_ref, ps_ref, wo_ref, o_ref, wob_ref):
    n_seq, t = state_ref.shape[0], u_ref.shape[0] // state_ref.shape[0]
    parts = []
    for s in range(n_seq):
        ext = jnp.concatenate([state_ref[s], u_ref[s * t:(s + 1) * t, :]], axis=0)
        parts.append(_pool_mix(ext, wp_ref, ps_ref[...]))
    wo = wo_ref[...].astype(BF16)
    wob_ref[...] = wo
    _mix_out(jnp.concatenate(parts, axis=0), b_ref, x_ref, wo, o_ref)


def _out_common_specs(rows, idx):
    return (pl.BlockSpec((rows, ATTN_WIDTH), idx), pl.BlockSpec((rows, D_MODEL), idx))


def _out_prompt(u, b, x, wp, ps, wo, batch, seq):
    rows = OUT_PROMPT_ROWS
    nt = seq // rows
    idx = lambda bi, i: (bi * nt + i, 0)
    u_spec = pl.BlockSpec(
        (pl.Squeezed(), pl.Element(POOL_HISTORY + rows), pl.Element(POOL_WIDTH)),
        lambda bi, i: (bi, N_META - POOL_HISTORY + i * rows, 0))
    _, x_spec = _out_common_specs(rows, idx)
    b_spec = pl.BlockSpec((None, HEADS, rows, HEAD_DIM), lambda bi, i: (bi, 0, i, 0))
    return pl.pallas_call(
        _out_prompt_kernel,
        grid=(batch, nt),
        in_specs=[u_spec, b_spec, x_spec,
                  _resident(wp.shape), _resident(ps.shape), _resident(wo.shape)],
        out_specs=x_spec,
        out_shape=jax.ShapeDtypeStruct(x.shape, F32),
        compiler_params=pltpu.CompilerParams(
            dimension_semantics=("parallel", "parallel"), vmem_limit_bytes=VMEM_LIMIT),
        name="out_proj_prompt",
    )(u, b, x, wp, ps, wo)


def _out_sample(u, history, b, x, wp, ps, wo, t):
    rows = PROJ_ROWS
    n_seq = rows // t
    idx = lambda i: (i, 0)
    b_spec, x_spec = _out_common_specs(rows, idx)
    return pl.pallas_call(
        _out_sample_kernel,
        grid=(x.shape[0] // rows,),
        in_specs=[
            pl.BlockSpec((rows, POOL_WIDTH), idx),
            pl.BlockSpec((n_seq, POOL_HISTORY, POOL_WIDTH), lambda i: (i, 0, 0)),
            b_spec, x_spec,
            _resident(wp.shape), _resident(ps.shape), _resident(wo.shape),
        ],
        out_specs=[x_spec, pl.BlockSpec(wo.shape, lambda i: (0, 0))],
        out_shape=[jax.ShapeDtypeStruct(x.shape, F32), jax.ShapeDtypeStruct(wo.shape, BF16)],
        compiler_params=pltpu.CompilerParams(
            dimension_semantics=("arbitrary",), vmem_limit_bytes=VMEM_LIMIT),
        name="out_proj_sample",
    )(u, history, b, x, wp, ps, wo)


def _ffn_kernel(h_ref, gf_ref, wg_ref, wu_ref, wd_ref, gfin_ref, y_ref, *rest):
    n_ref = rest[-1]
    f = pl.program_id(1)

    @pl.when(f == 0)
    def _():
        h = h_ref[...]
        n_ref[...] = _rms(h, gf_ref[...], EPS).astype(BF16)
        y_ref[...] = h

    weights = [wg_ref[...], wu_ref[...], wd_ref[...]]
    if len(rest) > 1:
        weights = [w.astype(BF16) for w in weights]
        for out_ref, w in zip(rest[:3], weights):
            out_ref[...] = w
    wg, wu, wd = weights
    n = n_ref[...]
    g = _dot(n, wg)
    u = _dot(n, wu)
    act = (g * (1.0 / (1.0 + jnp.exp(-g))) * u).astype(BF16)
    y_ref[...] += _dot(act, wd)

    @pl.when(f == pl.num_programs(1) - 1)
    def _():
        y_ref[...] = _rms(y_ref[...], gfin_ref[...], EPS)


def _ffn(h, gf, wg, wu, wd, gfin, cols):
    rows = FFN_ROWS
    emit = wg.dtype == F32
    row = pl.BlockSpec((rows, D_MODEL), lambda i, f: (i, 0))
    vec = pl.BlockSpec((1, D_MODEL), lambda i, f: (0, 0))
    w_in = pl.BlockSpec((D_MODEL, cols), lambda i, f: (0, f))
    w_out = pl.BlockSpec((cols, D_MODEL), lambda i, f: (f, 0))
    out_specs, out_shape = [row], [jax.ShapeDtypeStruct(h.shape, F32)]
    if emit:
        out_specs += [w_in, w_in, w_out]
        out_shape += [jax.ShapeDtypeStruct(w.shape, BF16) for w in (wg, wu, wd)]
    outs = pl.pallas_call(
        _ffn_kernel,
        grid=(h.shape[0] // rows, D_FF // cols),
        in_specs=[row, vec, w_in, w_in, w_out, vec],
        out_specs=out_specs,
        out_shape=out_shape,
        scratch_shapes=[pltpu.VMEM((rows, D_MODEL), BF16)],
        compiler_params=pltpu.CompilerParams(
            dimension_semantics=("arbitrary" if emit else "parallel", "arbitrary"),
            vmem_limit_bytes=VMEM_LIMIT),
        name="swiglu_f32w" if emit else "swiglu",
    )(h, gf, wg, wu, wd, gfin)
    return outs if emit else outs[0]


def kernel(x_prompt, x_sample, cache_k, cache_v, state_pool, meta, g_mix, w_in, w_pool,
           pool_scale, lambda_q1, lambda_k1, lambda_q2, lambda_k2, g_subln, w_out, g_ffn,
           w_gate, w_up, w_down, rel_bias, g_final):
    batch, seq, d = x_prompt.shape
    dec_batch, t, _ = x_sample.shape
    depth = w_in.shape[0]
    assert depth == 1 and d == D_MODEL and meta.shape == (N_META, D_MODEL)
    assert seq % OUT_PROMPT_ROWS == 0 and seq % Q_TILE == 0 and PROJ_ROWS % t == 0
    assert (dec_batch * t) % PROJ_ROWS == 0 and t >= POOL_STATE
    assert POOL_STATE == max(POOL_WINDOWS) - 1 <= POOL_HISTORY <= N_META
    assert (N_META + seq) % (PROMPT_TILES * BF16_ROWS) == 0
    prompt_rows = (N_META + seq) // PROMPT_TILES
    n_cache = cache_k.shape[2]
    assert min(Q_TILE, SAMPLE_TAIL) >= _BUCKET_THRESHOLDS[-1]
    assert (n_cache - N_META) % CHUNK == 0 and t <= CHUNK
    assert n_cache > SAMPLE_TAIL and (n_cache - SAMPLE_TAIL) % HEAD_DIM == 0

    w_pool_b = w_pool[0].astype(BF16)
    g_mix2 = g_mix[0].reshape(1, D_MODEL)
    g_ffn2 = g_ffn[0].reshape(1, D_MODEL)
    g_fin2 = g_final.reshape(1, D_MODEL)
    g_sub2 = g_subln[0].reshape(1, HEAD_DIM)
    ps2 = pool_scale[0].reshape(1, POOL_WIDTH)

    diag_b, meta_b, tail_b, lam = _bias_tables(
        rel_bias, lambda_q1, lambda_k1, lambda_q2, lambda_k2, t)

    xp = x_prompt.reshape(batch * seq, D_MODEL)
    xs = x_sample.reshape(dec_batch * t, D_MODEL)
    u_s, q_s, k_s, v_s, w_in_b = _project_rows(xs, g_mix2, w_in[0], PROJ_ROWS)
    u_p, q_p, k_all, v_all, kb_p, vb_p = _project_prompt(x_prompt, meta, g_mix2, w_in_b, prompt_rows)

    b_p = _attn_prompt(q_p, kb_p, vb_p, diag_b, meta_b, lam, g_sub2, batch, seq)
    b_s = _attn_sample(q_s, k_s, v_s,
                       cache_k[0].reshape(dec_batch, n_cache * HEADS, HEAD_DIM),
                       cache_v[0].reshape(dec_batch, n_cache * HEADS, HEAD_DIM),
                       tail_b, lam, g_sub2, dec_batch, t)

    history = jnp.pad(state_pool[0], ((0, 0), (POOL_HISTORY - POOL_STATE, 0), (0, 0)))
    h_s, w_out_b = _out_sample(u_s, history, b_s, xs, w_pool_b, ps2, w_out[0], t)
    h_p = _out_prompt(u_p, b_p, xp, w_pool_b, ps2, w_out_b, batch, seq)

    y_s, wg_b, wu_b, wd_b = _ffn(h_s, g_ffn2, w_gate[0], w_up[0], w_down[0], g_fin2, FFN_COLS_F32)
    y_p = _ffn(h_p, g_ffn2, wg_b, wu_b, wd_b, g_fin2, FFN_COLS)

    hd = (HEADS, HEAD_DIM)
    return (
        y_p.reshape(batch, seq, D_MODEL),
        y_s.reshape(dec_batch, t, D_MODEL),
        k_all.reshape(1, batch, N_META + seq, *hd),
        v_all.reshape(1, batch, N_META + seq, *hd),
        u_p[None, :, N_META + seq - POOL_STATE:],
        k_s.reshape(1, dec_batch, t, *hd),
        v_s.reshape(1, dec_batch, t, *hd),
        u_s.reshape(dec_batch, t, POOL_WIDTH)[None, :, t - POOL_STATE:],
    )
```

```python
import functools
import math

import jax
import jax.numpy as jnp
from jax import lax
from jax.experimental import pallas as pl
from jax.experimental.pallas import tpu as pltpu

F32 = jnp.float32
BF16 = jnp.bfloat16

D_MODEL = 2048
CHUNK = 64
N_META = 16
POOL_WINDOWS = (2, 4, 8, 16)
POOL_GROUP_DIM = 256
POOL_WIDTH = 1024
POOL_STATE = 15
POOL_HISTORY = 16
HEADS = 8
HEAD_DIM = 128
HALF_DIM = 64
ATTN_WIDTH = 1024
ATT_SCALE = HALF_DIM ** -0.5
LOG2E = math.log2(math.e)
D_FF = 5632
EPS = 1e-6
SUBLN_EPS = 1e-5
NEG_INF = -1e30
LAMBDA_INIT = 0.8 - 0.6 * math.exp(-0.3 * 0)

_BUCKET_THRESHOLDS = (1, 2, 3, 4, 5, 6, 7, 8, 12, 16, 23, 32, 46, 64, 91)

Q_TILE = 256
SOFTMAX_ROWS = 32
PROJ_ROWS = 512
PROMPT_TILES = 3
FFN_ROWS = 1024
FFN_COLS = 512
FFN_COLS_F32 = 256
SAMPLE_TAIL = 272
VMEM_LIMIT = 56 * 1024 * 1024
BF16_ROWS = 16


def _rms(x, g, eps):
    ms = jnp.mean(x * x, axis=-1, keepdims=True)
    return x * lax.rsqrt(ms + eps) * g


def _dot(a, b):
    return jnp.dot(a, b, preferred_element_type=F32)


def _dot_nt(a, b):
    return lax.dot_general(a, b, (((1,), (1,)), ((), ())), preferred_element_type=F32)


def _resident(shape):
    nd = len(shape)
    return pl.BlockSpec(shape, lambda *_: (0,) * nd, pipeline_mode=pl.Buffered(1))


def _smem():
    return pl.BlockSpec(memory_space=pltpu.SMEM)


def _bias_of_rel(rel, table_ref, h):
    n = jnp.abs(rel)
    far = table_ref[len(_BUCKET_THRESHOLDS), h]

    def entry(b):
        return (table_ref[b, h] - far) * LOG2E

    neg = jnp.full(rel.shape, entry(0), F32)
    pos = jnp.full(rel.shape, entry(16), F32)
    for b, thr in enumerate(_BUCKET_THRESHOLDS, start=1):
        ge = n >= thr
        neg = jnp.where(ge, entry(b), neg)
        pos = jnp.where(ge, entry(16 + b), pos)
    return jnp.where(rel > 0, pos, neg)


def _bias_kernel(table_ref, lq1_ref, lk1_ref, lq2_ref, lk2_ref,
                 diag_ref, meta_ref, tail_ref, lam_ref):
    h = pl.program_id(0)
    t = Q_TILE
    qi = lax.broadcasted_iota(jnp.int32, (t, t), 0)
    kj = lax.broadcasted_iota(jnp.int32, (t, t), 1)
    visible = (kj // CHUNK) <= (qi // CHUNK)
    diag_ref[0, 0] = jnp.where(visible, _bias_of_rel(kj - qi, table_ref, h), NEG_INF)
    near = -(-_BUCKET_THRESHOLDS[-1] // 8) * 8
    diag_ref[0, 1] = jnp.zeros((t, t), F32)
    qc = lax.broadcasted_iota(jnp.int32, (near, HEAD_DIM), 0)
    kc = lax.broadcasted_iota(jnp.int32, (near, HEAD_DIM), 1) + (t - HEAD_DIM)
    diag_ref[0, 1, 0:near, t - HEAD_DIM:t] = _bias_of_rel(kc - t - qc, table_ref, h)
    qm = lax.broadcasted_iota(jnp.int32, (t, N_META), 0)
    km = lax.broadcasted_iota(jnp.int32, (t, N_META), 1)
    meta_ref[0] = _bias_of_rel(km - N_META - qm, table_ref, h)
    qs = lax.broadcasted_iota(jnp.int32, tail_ref.shape[1:], 0)
    ks = lax.broadcasted_iota(jnp.int32, tail_ref.shape[1:], 1)
    tail_ref[0] = _bias_of_rel(ks - SAMPLE_TAIL - qs, table_ref, h)
    s1 = jnp.sum(lq1_ref[...] * lk1_ref[...], axis=-1, keepdims=True)
    s2 = jnp.sum(lq2_ref[...] * lk2_ref[...], axis=-1, keepdims=True)
    lam_ref[...] = jnp.exp(s1) - jnp.exp(s2) + LAMBDA_INIT


def _bias_tables(rel_bias, lq1, lk1, lq2, lk2, n_new):
    tail_w = SAMPLE_TAIL + n_new
    vec = pl.BlockSpec((1, HALF_DIM), lambda h: (0, 0))
    return pl.pallas_call(
        _bias_kernel,
        grid=(HEADS,),
        in_specs=[_smem(), vec, vec, vec, vec],
        out_specs=[
            pl.BlockSpec((1, 2, Q_TILE, Q_TILE), lambda h: (h, 0, 0, 0)),
            pl.BlockSpec((1, Q_TILE, N_META), lambda h: (h, 0, 0)),
            pl.BlockSpec((1, n_new, tail_w), lambda h: (h, 0, 0)),
            pl.BlockSpec((1, 1), lambda h: (0, 0)),
        ],
        out_shape=[
            jax.ShapeDtypeStruct((HEADS, 2, Q_TILE, Q_TILE), F32),
            jax.ShapeDtypeStruct((HEADS, Q_TILE, N_META), F32),
            jax.ShapeDtypeStruct((HEADS, n_new, tail_w), F32),
            jax.ShapeDtypeStruct((1, 1), F32),
        ],
        name="bias_tables",
    )(rel_bias, lq1, lk1, lq2, lk2)


def _store_heads(o_ref, x, row0=0):
    rows = x.shape[0]
    for h in range(HEADS):
        o_ref[pl.ds(row0 * HEADS + h, rows, stride=HEADS), :] = x[:, h * HEAD_DIM:(h + 1) * HEAD_DIM]


def _store_column(c, y, u_ref, q_ref, k_ref, v_ref, row0=0, by_head=None):
    rows = slice(row0, row0 + y.shape[0])
    if c == 0:
        u_ref[rows, :] = y
    elif c == 1:
        q = (y * (ATT_SCALE * LOG2E)).astype(BF16)
        if by_head is None:
            q_ref[rows, :] = q
        else:
            for h in range(HEADS):
                q_ref[h, rows, :] = q[:, h * HEAD_DIM:(h + 1) * HEAD_DIM]
    else:
        _store_heads(k_ref if c == 2 else v_ref, y, row0)
        if by_head is not None:
            for h in range(HEADS):
                by_head[c - 2][h, rows, :] = y[:, h * HEAD_DIM:(h + 1) * HEAD_DIM].astype(BF16)


def _project_store(xn, w_ref, u_ref, q_ref, k_ref, v_ref, row0=0, by_head=None):
    w = POOL_WIDTH
    for c in range(4):
        _store_column(c, _dot(xn, w_ref[:, c * w:(c + 1) * w]), u_ref, q_ref, k_ref, v_ref, row0, by_head)


def _proj_rows_kernel(x_ref, g_ref, w_ref, u_ref, q_ref, k_ref, v_ref, wb_ref, xn_ref):
    j = pl.program_id(0)
    rows = x_ref.shape[0]
    tile = pl.ds(pl.multiple_of(pl.program_id(1) * rows, rows), rows)

    @pl.when(j == 0)
    def _():
        xn_ref[tile, :] = _rms(x_ref[...], g_ref[...], EPS).astype(BF16)

    wb = w_ref[...].astype(BF16)
    wb_ref[...] = wb
    y = _dot(xn_ref[tile, :], wb)
    for c in range(4):
        @pl.when(j == c)
        def _(c=c):
            _store_column(c, y, u_ref, q_ref, k_ref, v_ref)


def _proj_prompt_kernel(x_ref, meta_ref, g_ref, w_ref, u_ref, q_ref, k_ref, v_ref, kb_ref, vb_ref,
                        xn_ref, carry_ref):
    i = pl.program_id(1)
    g = g_ref[...]
    rows = xn_ref.shape[0]
    split = -(-(rows // 2) // BF16_ROWS) * BF16_ROWS

    @pl.when(i == 0)
    def _():
        xn_ref[0:N_META, :] = _rms(meta_ref[...], g, EPS).astype(BF16)

    @pl.when(i > 0)
    def _():
        xn_ref[0:N_META, :] = carry_ref[...]

    xn_ref[N_META:split, :] = _rms(x_ref[0:split - N_META, :], g, EPS).astype(BF16)
    by_head = (kb_ref, vb_ref)
    _project_store(xn_ref[0:split, :], w_ref, u_ref, q_ref, k_ref, v_ref, 0, by_head)
    xn_ref[split:rows, :] = _rms(x_ref[split - N_META:rows - N_META, :], g, EPS).astype(BF16)
    _project_store(xn_ref[split:rows, :], w_ref, u_ref, q_ref, k_ref, v_ref, split, by_head)

    @pl.when(i < pl.num_programs(1) - 1)
    def _():
        carry_ref[...] = _rms(x_ref[rows - N_META:rows, :], g, EPS).astype(BF16)


def _proj_out_shapes(lead, n):
    return [jax.ShapeDtypeStruct((*lead, n, POOL_WIDTH), F32),
            jax.ShapeDtypeStruct((*lead, n, POOL_WIDTH), BF16),
            jax.ShapeDtypeStruct((*lead, n * HEADS, HEAD_DIM), F32),
            jax.ShapeDtypeStruct((*lead, n * HEADS, HEAD_DIM), F32)]


def _project_rows(x, g, w_f32, rows):
    n = x.shape[0]
    w = POOL_WIDTH
    last = n // rows - 1

    def rows_of(c):
        return lambda j, i: (jnp.where(j < c, 0, jnp.where(j == c, i, last)), 0)

    uq = [pl.BlockSpec((rows, w), rows_of(c)) for c in (0, 1)]
    kv = [pl.BlockSpec((rows * HEADS, HEAD_DIM), rows_of(c)) for c in (2, 3)]
    w_spec = pl.BlockSpec((D_MODEL, w), lambda j, i: (0, j))
    return pl.pallas_call(
        _proj_rows_kernel,
        grid=(w_f32.shape[1] // w, n // rows),
        in_specs=[pl.BlockSpec((rows, D_MODEL), rows_of(0)), _resident((1, D_MODEL)), w_spec],
        out_specs=[*uq, *kv, w_spec],
        out_shape=_proj_out_shapes((), n) + [jax.ShapeDtypeStruct(w_f32.shape, BF16)],
        scratch_shapes=[pltpu.VMEM((n, D_MODEL), BF16)],
        compiler_params=pltpu.CompilerParams(
            dimension_semantics=("arbitrary", "arbitrary"), vmem_limit_bytes=VMEM_LIMIT),
        name="in_proj_rows",
    )(x, g, w_f32)


def _project_prompt(x, meta, g, w_bf16, rows):
    batch, seq, _ = x.shape
    total = N_META + seq
    x_spec = pl.BlockSpec((None, rows, D_MODEL), lambda b, i: (b, i, 0))
    uq = pl.BlockSpec((None, rows, POOL_WIDTH), lambda b, i: (b, i, 0))
    kv = pl.BlockSpec((None, rows * HEADS, HEAD_DIM), lambda b, i: (b, i, 0))
    kvb = pl.BlockSpec((None, HEADS, rows, HEAD_DIM), lambda b, i: (b, 0, i, 0))
    u_shape, _, k_shape, v_shape = _proj_out_shapes((batch,), total)
    by_head = jax.ShapeDtypeStruct((batch, HEADS, total, HEAD_DIM), BF16)
    return pl.pallas_call(
        _proj_prompt_kernel,
        grid=(batch, total // rows),
        in_specs=[x_spec, _resident(meta.shape), _resident((1, D_MODEL)), _resident(w_bf16.shape)],
        out_specs=[uq, kvb, kv, kv, kvb, kvb],
        out_shape=[u_shape, by_head, k_shape, v_shape, by_head, by_head],
        scratch_shapes=[pltpu.VMEM((rows, D_MODEL), BF16), pltpu.VMEM((N_META, D_MODEL), BF16)],
        compiler_params=pltpu.CompilerParams(
            dimension_semantics=("parallel", "arbitrary"), vmem_limit_bytes=VMEM_LIMIT),
        name="in_proj_prompt",
    )(x, meta, g, w_bf16)


def _stack_maps(q):
    lane = lax.broadcasted_iota(jnp.int32, q.shape, 1)
    zero = jnp.zeros_like(q)
    return jnp.concatenate(
        [jnp.where(lane < HALF_DIM, q, zero), jnp.where(lane >= HALF_DIM, q, zero)], axis=0)


def _finish_head(o, gs):
    return (_rms(o, gs, SUBLN_EPS) * (1.0 - LAMBDA_INIT)).astype(BF16)


def _attn_prompt_kernel(lam_ref, q_ref, kb_ref, vb_ref, diag_ref, meta_ref, gs_ref,
                        o_ref, q2_ref, s_ref, sm_ref, w_ref, wm_ref, rl_ref):
    tq = Q_TILE
    nt = o_ref.shape[0] // tq
    rb = SOFTMAX_ROWS
    lam = lam_ref[0, 0]

    def score_steps(c):
        slot = c % 2

        def stack():
            q2_ref[slot] = _stack_maps(q_ref[N_META + c * tq:N_META + (c + 1) * tq, :])

        def frames(j):
            d = _dot_nt(q2_ref[slot], kb_ref[N_META + j * tq:N_META + (j + 1) * tq, :])
            cols = slice(j * tq, (j + 1) * tq)
            if j >= c - 1:
                bias = diag_ref[0, c - j]
                s_ref[slot, :tq, cols] = d[:tq] + bias
                s_ref[slot, tq:, cols] = d[tq:] + bias
            else:
                s_ref[slot, :, cols] = d

        def metas():
            sm = _dot_nt(q2_ref[slot], kb_ref[0:N_META, :])
            if c == 0:
                sm = sm + jnp.concatenate([meta_ref[0], meta_ref[0]], axis=0)
            sm_ref[slot] = sm

        return [stack, metas] + [functools.partial(frames, j) for j in range(c + 1)]

    def softmax_step(c, i):
        slot = c % 2
        n = (c + 1) * tq
        last_chunk = ((i + 1) * rb - 1) // CHUNK
        seen = c * tq + -(-(last_chunk + 1) * CHUNK // HEAD_DIM) * HEAD_DIM
        rows = (slice(i * rb, (i + 1) * rb), slice(tq + i * rb, tq + (i + 1) * rb))
        p, pm, l = [], [], []
        for rmap in rows:
            sm = sm_ref[slot, rmap, :]
            m = jnp.maximum(jnp.max(s_ref[slot, rmap, :seen], axis=-1, keepdims=True),
                            jnp.max(sm, axis=-1, keepdims=True))
            p.append(jnp.exp2(s_ref[slot, rmap, :seen] - m))
            pm.append(jnp.exp2(sm - m))
            l.append(jnp.sum(p[-1], axis=-1, keepdims=True) + jnp.sum(pm[-1], axis=-1, keepdims=True))
        rho = lam * l[0] / l[1]
        w_ref[slot, rows[0], :seen] = (p[0] - p[1] * rho).astype(BF16)
        if seen < n:
            w_ref[slot, rows[0], seen:n] = jnp.zeros((rb, n - seen), BF16)
        wm_ref[slot, rows[0], :] = (pm[0] - pm[1] * rho).astype(BF16)
        rl_ref[slot, rows[0], :] = 1.0 / l[0]

    def value_step(c):
        slot = c % 2
        n = (c + 1) * tq
        o = _dot(w_ref[slot, :, :n], vb_ref[N_META:N_META + n, :]) + _dot(wm_ref[slot], vb_ref[0:N_META, :])
        o_ref[c * tq:(c + 1) * tq, :] = _finish_head(o * rl_ref[slot], gs_ref[...])

    for step in score_steps(0):
        step()
    for c in range(nt):
        ahead = score_steps(c + 1) if c + 1 < nt else []
        blocks = tq // rb
        for i in range(blocks):
            softmax_step(c, i)
            for step in ahead[i * len(ahead) // blocks:(i + 1) * len(ahead) // blocks]:
                step()
        value_step(c)


def _attn_prompt(q, kb, vb, diag, meta_b, lam, gs, batch, seq):
    tq = Q_TILE
    kv = pl.BlockSpec((None, None, N_META + seq, HEAD_DIM), lambda b, h: (b, h, 0, 0))
    qo = pl.BlockSpec((None, None, seq, HEAD_DIM), lambda b, h: (b, h, 0, 0))
    return pl.pallas_call(
        _attn_prompt_kernel,
        grid=(batch, HEADS),
        in_specs=[
            _smem(), kv, kv, kv,
            pl.BlockSpec((1, 2, tq, tq), lambda b, h: (h, 0, 0, 0)),
            pl.BlockSpec((1, tq, N_META), lambda b, h: (h, 0, 0)),
            pl.BlockSpec((1, HEAD_DIM), lambda b, h: (0, 0)),
        ],
        out_specs=qo,
        out_shape=jax.ShapeDtypeStruct((batch, HEADS, seq, HEAD_DIM), BF16),
        scratch_shapes=[
            pltpu.VMEM((2, 2 * tq, HEAD_DIM), BF16),
            pltpu.VMEM((2, 2 * tq, seq), F32),
            pltpu.VMEM((2, 2 * tq, N_META), F32),
            pltpu.VMEM((2, tq, seq), BF16),
            pltpu.VMEM((2, tq, N_META), BF16),
            pltpu.VMEM((2, tq, 1), F32),
        ],
        compiler_params=pltpu.CompilerParams(
            dimension_semantics=("parallel", "parallel"), vmem_limit_bytes=VMEM_LIMIT),
        name="attn_prompt",
    )(lam, q, kb, vb, diag, meta_b, gs)


def _attn_sample_kernel(lam_ref, q_ref, kn_ref, vn_ref, ck_ref, cv_ref, tail_ref, gs_ref,
                        o_ref, qd_ref, s_ref, w_ref):
    t = q_ref.shape[0]
    n_cache = ck_ref.shape[1] // HEADS
    split = n_cache - SAMPLE_TAIL
    width = n_cache + t
    lam = lam_ref[0, 0]
    hd = HEAD_DIM

    def gather(c_ref, n_ref, h):
        far = pl.ds(h, split, stride=HEADS)
        near = pl.ds(split * HEADS + h, SAMPLE_TAIL, stride=HEADS)
        new = pl.ds(h, t, stride=HEADS)
        return (c_ref[0, far, :].astype(BF16),
                jnp.concatenate([c_ref[0, near, :].astype(BF16), n_ref[new, :].astype(BF16)], axis=0))

    def pair_rows(c_ref, n_ref, hp):
        fa, ta = gather(c_ref, n_ref, 2 * hp)
        fb, tb = gather(c_ref, n_ref, 2 * hp + 1)
        return jnp.concatenate([fa, fb], axis=1), jnp.concatenate([ta, tb], axis=1)

    def score_steps(hp):
        slot = hp % 2

        def queries():
            zero = jnp.zeros((2 * t, hd), BF16)
            qa = _stack_maps(q_ref[:, 2 * hp * hd:(2 * hp + 1) * hd])
            qb = _stack_maps(q_ref[:, (2 * hp + 1) * hd:(2 * hp + 2) * hd])
            qd_ref[slot] = jnp.concatenate(
                [jnp.concatenate([qa, zero], axis=1), jnp.concatenate([zero, qb], axis=1)], axis=0)

        def scores():
            k_far, k_tail = pair_rows(ck_ref, kn_ref, hp)
            s_ref[slot, :, 0:split] = _dot_nt(qd_ref[slot], k_far)
            ba, bb = tail_ref[2 * hp], tail_ref[2 * hp + 1]
            s_ref[slot, :, split:width] = (
                _dot_nt(qd_ref[slot], k_tail) + jnp.concatenate([ba, ba, bb, bb], axis=0))

        return [queries, scores]

    def softmax_step(hp, j):
        slot = hp % 2
        r, p = [], []
        for rows in (slice(2 * t * j, 2 * t * j + t), slice(2 * t * j + t, 2 * t * (j + 1))):
            m = jnp.max(s_ref[slot, rows, 0:width], axis=-1, keepdims=True)
            p.append(jnp.exp2(s_ref[slot, rows, 0:width] - m))
            r.append(1.0 / jnp.sum(p[-1], axis=-1, keepdims=True))
        w_ref[slot, t * j:t * (j + 1), 0:width] = (p[0] * r[0] - p[1] * (lam * r[1])).astype(BF16)

    def value_step(hp):
        slot = hp % 2
        v_far, v_tail = pair_rows(cv_ref, vn_ref, hp)
        o = _dot(w_ref[slot, :, 0:split], v_far) + _dot(w_ref[slot, :, split:width], v_tail)
        gs = gs_ref[...]
        o_ref[:, 2 * hp * hd:(2 * hp + 1) * hd] = _finish_head(o[0:t, 0:hd], gs)
        o_ref[:, (2 * hp + 1) * hd:(2 * hp + 2) * hd] = _finish_head(o[t:2 * t, hd:2 * hd], gs)

    pairs = HEADS // 2
    for step in score_steps(0):
        step()
    for hp in range(pairs):
        ahead = score_steps(hp + 1) if hp + 1 < pairs else [lambda: None, lambda: None]
        softmax_step(hp, 0)
        ahead[0]()
        ahead[1]()
        softmax_step(hp, 1)
        value_step(hp)


def _attn_sample(q, k_new, v_new, cache_k, cache_v, tail_b, lam, gs, batch, t):
    n_cache = cache_k.shape[1] // HEADS
    row = pl.BlockSpec((t, ATTN_WIDTH), lambda b: (b, 0))
    new = pl.BlockSpec((t * HEADS, HEAD_DIM), lambda b: (b, 0))
    cache = pl.BlockSpec((1,) + cache_k.shape[1:], lambda b: (b, 0, 0))
    width = -(-(n_cache + t) // HEAD_DIM) * HEAD_DIM
    return pl.pallas_call(
        _attn_sample_kernel,
        grid=(batch,),
        in_specs=[_smem(), row, new, new, cache, cache,
                  _resident(tail_b.shape), _resident((1, HEAD_DIM))],
        out_specs=row,
        out_shape=jax.ShapeDtypeStruct((batch * t, ATTN_WIDTH), BF16),
        scratch_shapes=[
            pltpu.VMEM((2, 4 * t, 2 * HEAD_DIM), BF16),
            pltpu.VMEM((2, 4 * t, width), F32),
            pltpu.VMEM((2, 2 * t, width), BF16),
        ],
        compiler_params=pltpu.CompilerParams(
            dimension_semantics=("parallel",), vmem_limit_bytes=VMEM_LIMIT),
        name="attn_sample",
    )(lam, q, k_new, v_new, cache_k, cache_v, tail_b, gs)


def _pool_mix(ext, wp_ref, scale):
    outs = []
    for g, w in enumerate(POOL_WINDOWS):
        xg = ext[:, g * POOL_GROUP_DIM:(g + 1) * POOL_GROUP_DIM]
        s = xg
        span = 1
        while span < w:
            s = s + pltpu.roll(s, span, axis=0)
            span *= 2
        p = (s[POOL_HISTORY:] * (1.0 / w) - xg[POOL_HISTORY:]).astype(BF16)
        outs.append(_dot(p, wp_ref[g]))
    return jnp.concatenate(outs, axis=-1) * scale


def _mix_out(a, b_ref, x_ref, wo, o_ref):
    mix = jnp.concatenate([a.astype(BF16), b_ref[...]], axis=-1)
    o_ref[...] = x_ref[...] + _dot(mix, wo)


def _out_prompt_kernel(u_ref, b_ref, x_ref, wp_ref, ps_ref, wo_ref, o_ref):
    half = x_ref.shape[0] // 2
    for r in (0, half):
        a = _pool_mix(u_ref[r:r + POOL_HISTORY + half, :], wp_ref, ps_ref[...])
        b = [b_ref[h, r:r + half, :] for h in range(HEADS)]
        mix = jnp.concatenate([a.astype(BF16), *b], axis=-1)
        o_ref[r:r + half, :] = x_ref[r:r + half, :] + _dot(mix, wo_ref[...])


def _out_sample_kernel(u_ref, state_ref, b_ref, x_ref, wp_ref, ps_ref, wo_ref, o_ref, wob_ref):
    n_seq, t = state_ref.shape[0], u_ref.shape[0] // state_ref.shape[0]
    parts = []
    for s in range(n_seq):
        ext = jnp.concatenate([state_ref[s], u_ref[s * t:(s + 1) * t, :]], axis=0)
        parts.append(_pool_mix(ext, wp_ref, ps_ref[...]))
    wo = wo_ref[...].astype(BF16)
    wob_ref[...] = wo
    _mix_out(jnp.concatenate(parts, axis=0), b_ref, x_ref, wo, o_ref)


def _out_common_specs(rows, idx):
    return (pl.BlockSpec((rows, ATTN_WIDTH), idx), pl.BlockSpec((rows, D_MODEL), idx))


def _out_prompt(u, b, x, wp, ps, wo, batch, seq):
    rows = PROJ_ROWS
    nt = seq // rows
    idx = lambda bi, i: (bi * nt + i, 0)
    u_spec = pl.BlockSpec(
        (pl.Squeezed(), pl.Element(POOL_HISTORY + rows), pl.Element(POOL_WIDTH)),
        lambda bi, i: (bi, N_META - POOL_HISTORY + i * rows, 0))
    _, x_spec = _out_common_specs(rows, idx)
    b_spec = pl.BlockSpec((None, HEADS, rows, HEAD_DIM), lambda bi, i: (bi, 0, i, 0))
    return pl.pallas_call(
        _out_prompt_kernel,
        grid=(batch, nt),
        in_specs=[u_spec, b_spec, x_spec,
                  _resident(wp.shape), _resident(ps.shape), _resident(wo.shape)],
        out_specs=x_spec,
        out_shape=jax.ShapeDtypeStruct(x.shape, F32),
        compiler_params=pltpu.CompilerParams(
            dimension_semantics=("parallel", "parallel"), vmem_limit_bytes=VMEM_LIMIT),
        name="out_proj_prompt",
    )(u, b, x, wp, ps, wo)


def _out_sample(u, history, b, x, wp, ps, wo, t):
    rows = PROJ_ROWS
    n_seq = rows // t
    idx = lambda i: (i, 0)
    b_spec, x_spec = _out_common_specs(rows, idx)
    return pl.pallas_call(
        _out_sample_kernel,
        grid=(x.shape[0] // rows,),
        in_specs=[
            pl.BlockSpec((rows, POOL_WIDTH), idx),
            pl.BlockSpec((n_seq, POOL_HISTORY, POOL_WIDTH), lambda i: (i, 0, 0)),
            b_spec, x_spec,
            _resident(wp.shape), _resident(ps.shape), _resident(wo.shape),
        ],
        out_specs=[x_spec, pl.BlockSpec(wo.shape, lambda i: (0, 0))],
        out_shape=[jax.ShapeDtypeStruct(x.shape, F32), jax.ShapeDtypeStruct(wo.shape, BF16)],
        compiler_params=pltpu.CompilerParams(
            dimension_semantics=("arbitrary",), vmem_limit_bytes=VMEM_LIMIT),
        name="out_proj_sample",
    )(u, history, b, x, wp, ps, wo)


def _ffn_kernel(h_ref, gf_ref, wg_ref, wu_ref, wd_ref, gfin_ref, y_ref, *rest):
    n_ref = rest[-1]
    f = pl.program_id(1)

    @pl.when(f == 0)
    def _():
        h = h_ref[...]
        n_ref[...] = _rms(h, gf_ref[...], EPS).astype(BF16)
        y_ref[...] = h

    weights = [wg_ref[...], wu_ref[...], wd_ref[...]]
    if len(rest) > 1:
        weights = [w.astype(BF16) for w in weights]
        for out_ref, w in zip(rest[:3], weights):
            out_ref[...] = w
    wg, wu, wd = weights
    n = n_ref[...]
    g = _dot(n, wg)
    u = _dot(n, wu)
    act = (g * (1.0 / (1.0 + jnp.exp(-g))) * u).astype(BF16)
    y_ref[...] += _dot(act, wd)

    @pl.when(f == pl.num_programs(1) - 1)
    def _():
        y_ref[...] = _rms(y_ref[...], gfin_ref[...], EPS)


def _ffn(h, gf, wg, wu, wd, gfin, cols):
    rows = FFN_ROWS
    emit = wg.dtype == F32
    row = pl.BlockSpec((rows, D_MODEL), lambda i, f: (i, 0))
    vec = pl.BlockSpec((1, D_MODEL), lambda i, f: (0, 0))
    w_in = pl.BlockSpec((D_MODEL, cols), lambda i, f: (0, f))
    w_out = pl.BlockSpec((cols, D_MODEL), lambda i, f: (f, 0))
    out_specs, out_shape = [row], [jax.ShapeDtypeStruct(h.shape, F32)]
    if emit:
        out_specs += [w_in, w_in, w_out]
        out_shape += [jax.ShapeDtypeStruct(w.shape, BF16) for w in (wg, wu, wd)]
    outs = pl.pallas_call(
        _ffn_kernel,
        grid=(h.shape[0] // rows, D_FF // cols),
        in_specs=[row, vec, w_in, w_in, w_out, vec],
        out_specs=out_specs,
        out_shape=out_shape,
        scratch_shapes=[pltpu.VMEM((rows, D_MODEL), BF16)],
        compiler_params=pltpu.CompilerParams(
            dimension_semantics=("arbitrary" if emit else "parallel", "arbitrary"),
            vmem_limit_bytes=VMEM_LIMIT),
        name="swiglu_f32w" if emit else "swiglu",
    )(h, gf, wg, wu, wd, gfin)
    return outs if emit else outs[0]


def kernel(x_prompt, x_sample, cache_k, cache_v, state_pool, meta, g_mix, w_in, w_pool,
           pool_scale, lambda_q1, lambda_k1, lambda_q2, lambda_k2, g_subln, w_out, g_ffn,
           w_gate, w_up, w_down, rel_bias, g_final):
    batch, seq, d = x_prompt.shape
    dec_batch, t, _ = x_sample.shape
    depth = w_in.shape[0]
    assert depth == 1 and d == D_MODEL and meta.shape == (N_META, D_MODEL)
    assert seq % PROJ_ROWS == 0 and seq % Q_TILE == 0 and PROJ_ROWS % t == 0
    assert (dec_batch * t) % PROJ_ROWS == 0 and t >= POOL_STATE
    assert POOL_STATE == max(POOL_WINDOWS) - 1 <= POOL_HISTORY <= N_META
    assert (N_META + seq) % (PROMPT_TILES * BF16_ROWS) == 0
    prompt_rows = (N_META + seq) // PROMPT_TILES
    n_cache = cache_k.shape[2]
    assert min(Q_TILE, SAMPLE_TAIL) >= _BUCKET_THRESHOLDS[-1]
    assert (n_cache - N_META) % CHUNK == 0 and t <= CHUNK
    assert n_cache > SAMPLE_TAIL and (n_cache - SAMPLE_TAIL) % HEAD_DIM == 0

    w_pool_b = w_pool[0].astype(BF16)
    g_mix2 = g_mix[0].reshape(1, D_MODEL)
    g_ffn2 = g_ffn[0].reshape(1, D_MODEL)
    g_fin2 = g_final.reshape(1, D_MODEL)
    g_sub2 = g_subln[0].reshape(1, HEAD_DIM)
    ps2 = pool_scale[0].reshape(1, POOL_WIDTH)

    diag_b, meta_b, tail_b, lam = _bias_tables(
        rel_bias, lambda_q1, lambda_k1, lambda_q2, lambda_k2, t)

    xp = x_prompt.reshape(batch * seq, D_MODEL)
    xs = x_sample.reshape(dec_batch * t, D_MODEL)
    u_s, q_s, k_s, v_s, w_in_b = _project_rows(xs, g_mix2, w_in[0], PROJ_ROWS)
    u_p, q_p, k_all, v_all, kb_p, vb_p = _project_prompt(x_prompt, meta, g_mix2, w_in_b, prompt_rows)

    b_p = _attn_prompt(q_p, kb_p, vb_p, diag_b, meta_b, lam, g_sub2, batch, seq)
    b_s = _attn_sample(q_s, k_s, v_s,
                       cache_k[0].reshape(dec_batch, n_cache * HEADS, HEAD_DIM),
                       cache_v[0].reshape(dec_batch, n_cache * HEADS, HEAD_DIM),
                       tail_b, lam, g_sub2, dec_batch, t)

    history = jnp.pad(state_pool[0], ((0, 0), (POOL_HISTORY - POOL_STATE, 0), (0, 0)))
    h_s, w_out_b = _out_sample(u_s, history, b_s, xs, w_pool_b, ps2, w_out[0], t)
    h_p = _out_prompt(u_p, b_p, xp, w_pool_b, ps2, w_out_b, batch, seq)

    y_s, wg_b, wu_b, wd_b = _ffn(h_s, g_ffn2, w_gate[0], w_up[0], w_down[0], g_fin2, FFN_COLS_F32)
    y_p = _ffn(h_p, g_ffn2, wg_b, wu_b, wd_b, g_fin2, FFN_COLS)

    hd = (HEADS, HEAD_DIM)
    return (
        y_p.reshape(batch, seq, D_MODEL),
        y_s.reshape(dec_batch, t, D_MODEL),
        k_all.reshape(1, batch, N_META + seq, *hd),
        v_all.reshape(1, batch, N_META + seq, *hd),
        u_p[None, :, N_META + seq - POOL_STATE:],
        k_s.reshape(1, dec_batch, t, *hd),
        v_s.reshape(1, dec_batch, t, *hd),
        u_s.reshape(dec_batch, t, POOL_WIDTH)[None, :, t - POOL_STATE:],
    )
```
